```python
import math
import jax, jax.numpy as jnp
from jax import lax
import numpy as np

D_MODEL = 4096
BATCH = 2
SEQ = 8192
DEPTH = 2

HEAD_DIM = 128
D_MIX = D_MODEL
GROUP_WIDTH = D_MIX // 4
SC_KERNEL = 3
SC_GROUPS = GROUP_WIDTH // HEAD_DIM
DIFF_HEADS = GROUP_WIDTH // (2 * HEAD_DIM)
DN_HEADS = GROUP_WIDTH // HEAD_DIM
DN_CONV = 4
DN_CHUNK = 64
FOX_HEADS = GROUP_WIDTH // HEAD_DIM
BLOCK_Q = 128
D_FF = ((8 * D_MODEL // 3 + 255) // 256) * 256
FFN_KERNEL = 3
EPS = 1e-6
IN_WIDTHS = (
    GROUP_WIDTH, GROUP_WIDTH, GROUP_WIDTH,
    GROUP_WIDTH, GROUP_WIDTH, GROUP_WIDTH,
    GROUP_WIDTH, GROUP_WIDTH, GROUP_WIDTH, GROUP_WIDTH,
    DN_HEADS, DN_HEADS,
    GROUP_WIDTH, GROUP_WIDTH, GROUP_WIDTH,
    FOX_HEADS,
)
D_IN = sum(IN_WIDTHS)

kernel_name = "hybrid_parallel_groups_trunk"


def rms_norm(x, g):
    xf = x.astype(jnp.float32)
    y = xf * lax.rsqrt(jnp.mean(xf * xf, axis=-1, keepdims=True) + EPS)
    return (y * g.astype(jnp.float32)).astype(x.dtype)


def l2_norm(x):
    return x * lax.rsqrt(jnp.sum(x * x, axis=-1, keepdims=True) + EPS)


def causal_dwconv(x, w):
    K = w.shape[0]
    T = x.shape[1]
    xp = jnp.pad(x, ((0, 0), (K - 1, 0), (0, 0)))
    w = w.astype(x.dtype)
    return sum(xp[:, j:j + T] * w[j] for j in range(K))


def split_columns(p):
    offsets = np.cumsum(np.array(IN_WIDTHS))[:-1].tolist()
    return jnp.split(p, offsets, axis=-1)


def diff_attention(q, k, v, lam, lam_init, norm_g):
    B, T, _ = q.shape
    H, d = DIFF_HEADS, HEAD_DIM
    nb = T // BLOCK_Q
    q = q.reshape(B, T, H, 2, d)
    k = k.reshape(B, T, H, 2, d)
    v = v.reshape(B, T, H, 2 * d)
    qb = q.reshape(B, nb, BLOCK_Q, H, 2, d).transpose(1, 0, 2, 3, 4, 5)
    scale = d ** -0.5
    kpos = jnp.arange(T)

    def block(args):
        qi, i = args
        s = jnp.einsum('bqhmd,bkhmd->bhmqk', qi, k).astype(jnp.float32) * scale
        qpos = i * BLOCK_Q + jnp.arange(BLOCK_Q)
        causal = kpos[None, :] <= qpos[:, None]
        p = jax.nn.softmax(jnp.where(causal, s, -jnp.inf), axis=-1)
        wts = p[:, :, 0] - lam * p[:, :, 1]
        return jnp.einsum('bhqk,bkhe->bqhe', wts.astype(v.dtype), v)

    o = lax.map(block, (qb, jnp.arange(nb)))
    o = o.transpose(1, 0, 2, 3, 4).reshape(B, T, H, 2 * d)
    o = rms_norm(o, norm_g) * (1.0 - lam_init)
    return o.reshape(B, T, H * 2 * d)


def forgetting_attention(q, k, v, f_logit):
    B, T, _ = q.shape
    H, d = FOX_HEADS, HEAD_DIM
    nb = T // BLOCK_Q
    q = q.reshape(B, T, H, d)
    k = k.reshape(B, T, H, d)
    v = v.reshape(B, T, H, d)
    cum = jnp.cumsum(jax.nn.log_sigmoid(f_logit.astype(jnp.float32)), axis=1)
    cum_k = cum.transpose(0, 2, 1)
    qb = q.reshape(B, nb, BLOCK_Q, H, d).transpose(1, 0, 2, 3, 4)
    cb = cum.reshape(B, nb, BLOCK_Q, H).transpose(1, 0, 3, 2)
    scale = d ** -0.5
    kpos = jnp.arange(T)

    def block(args):
        qi, ci, i = args
        s = jnp.einsum('bqhd,bkhd->bhqk', qi, k).astype(jnp.float32) * scale
        s = s + ci[..., :, None] - cum_k[:, :, None, :]
        qpos = i * BLOCK_Q + jnp.arange(BLOCK_Q)
        causal = kpos[None, :] <= qpos[:, None]
        p = jax.nn.softmax(jnp.where(causal, s, -jnp.inf), axis=-1)
        return jnp.einsum('bhqk,bkhd->bqhd', p.astype(v.dtype), v)

    o = lax.map(block, (qb, cb, jnp.arange(nb)))
    return o.transpose(1, 0, 2, 3, 4).reshape(B, T, H * d)


def chunk_gated_delta_rule(q, k, v, g, beta):
    B, T, H, dk = q.shape
    dv = v.shape[-1]
    C = DN_CHUNK
    N = T // C
    f32 = jnp.float32

    def to_chunks(t):
        return t.astype(f32).transpose(0, 2, 1, 3).reshape(B, H, N, C, t.shape[-1])

    q, k, v = to_chunks(q), to_chunks(k), to_chunks(v)
    g = g.astype(f32).transpose(0, 2, 1).reshape(B, H, N, C)
    beta = beta.astype(f32).transpose(0, 2, 1).reshape(B, H, N, C)
    q = q * dk ** -0.5
    gc = jnp.cumsum(g, axis=-1)
    idx = jnp.arange(C)
    incl = idx[:, None] >= idx[None, :]
    strict = idx[:, None] > idx[None, :]
    decay = jnp.exp(jnp.where(incl, gc[..., :, None] - gc[..., None, :], -jnp.inf))
    kb = k * beta[..., None]
    a = jnp.where(strict, jnp.einsum('bhnid,bhnjd->bhnij', kb, k) * decay, 0.0)
    tmat = a + jnp.eye(C, dtype=f32)

    def solve(rhs):
        return lax.linalg.triangular_solve(tmat, rhs, left_side=True, lower=True, unit_diagonal=True)

    w = solve(kb * jnp.exp(gc)[..., None])
    u = solve(v * beta[..., None])
    aqk = jnp.where(incl, jnp.einsum('bhnid,bhnjd->bhnij', q, k) * decay, 0.0)
    xs = tuple(jnp.moveaxis(t, 2, 0) for t in (q, k, u, w, gc, aqk))

    def step(S, inp):
        qi, ki, ui, wi, gi, ai = inp
        v_new = ui - jnp.einsum('bhck,bhkv->bhcv', wi, S)
        o = jnp.einsum('bhck,bhkv->bhcv', qi * jnp.exp(gi)[..., None], S) + jnp.einsum('bhcs,bhsv->bhcv', ai, v_new)
        g_last = gi[..., -1]
        k_dec = ki * jnp.exp(g_last[..., None] - gi)[..., None]
        S = S * jnp.exp(g_last)[..., None, None] + jnp.einsum('bhck,bhcv->bhkv', k_dec, v_new)
        return S, o

    S0 = jnp.zeros((B, H, dk, dv), f32)
    _, o = lax.scan(step, S0, xs)
    return jnp.moveaxis(o, 0, 2).reshape(B, H, T, dv).transpose(0, 2, 1, 3)


def gated_deltanet(q, k, v, z, a_in, b_in, conv_w, a_log, dt_bias, norm_g):
    B, T, _ = q.shape
    H, d = DN_HEADS, HEAD_DIM
    qkv = jax.nn.silu(causal_dwconv(jnp.concatenate([q, k, v], axis=-1), conv_w))
    q, k, v = jnp.split(qkv, 3, axis=-1)
    q = l2_norm(q.reshape(B, T, H, d).astype(jnp.float32))
    k = l2_norm(k.reshape(B, T, H, d).astype(jnp.float32))
    v = v.reshape(B, T, H, d)
    g = -jnp.exp(a_log.astype(jnp.float32)) * jax.nn.softplus(a_in.astype(jnp.float32) + dt_bias.astype(jnp.float32))
    beta = jax.nn.sigmoid(b_in.astype(jnp.float32))
    o = chunk_gated_delta_rule(q, k, v, g, beta)
    o = rms_norm(o, norm_g) * jax.nn.silu(z.reshape(B, T, H, d).astype(jnp.float32))
    return o.reshape(B, T, H * d).astype(z.dtype)


def hybrid_layer(x, layer, attn_norm, w_in, sc_conv, lam_q1, lam_k1, lam_q2, lam_k2, diff_norm,
                 dn_conv, dn_a_log, dn_dt_bias, dn_norm, fox_bias, w_out,
                 ffn_norm, w_gate, w_up, ffn_conv, w_down):
    h = rms_norm(x, attn_norm)
    p = h @ w_in
    (sc_h, sc_c, sc_b, df_q, df_k, df_v, dn_q, dn_k, dn_v, dn_z, dn_a, dn_b,
     fx_q, fx_k, fx_v, fx_f) = split_columns(p)
    y_sc = sc_b * causal_dwconv(sc_c * sc_h, sc_conv)
    lam_init = 0.8 - 0.6 * math.exp(-0.3 * layer)
    lam = (jnp.exp(jnp.sum(lam_q1.astype(jnp.float32) * lam_k1.astype(jnp.float32)))
           - jnp.exp(jnp.sum(lam_q2.astype(jnp.float32) * lam_k2.astype(jnp.float32))) + lam_init)
    y_df = diff_attention(df_q, df_k, df_v, lam, lam_init, diff_norm)
    y_dn = gated_deltanet(dn_q, dn_k, dn_v, dn_z, dn_a, dn_b, dn_conv, dn_a_log, dn_dt_bias, dn_norm)
    y_fx = forgetting_attention(fx_q, fx_k, fx_v, fx_f + fox_bias)
    y = jnp.concatenate([y_sc, y_df, y_dn, y_fx], axis=-1) @ w_out
    x = x + y
    h = rms_norm(x, ffn_norm)
    a = causal_dwconv(h @ w_gate, ffn_conv)
    x = x + (jax.nn.silu(a) * (h @ w_up)) @ w_down
    return x


def setup_inputs(seed: int = 0) -> dict:
    key = jax.random.key(seed)
    ks = jax.random.split(key, 24)
    f32 = jnp.float32
    L = DEPTH

    def normal(k, shape, scale):
        return jax.random.normal(k, shape, f32) * scale

    def gain(k, shape):
        return 1.0 + 0.02 * jax.random.normal(k, shape, f32)

    dt = jnp.exp(jax.random.uniform(ks[12], (L, DN_HEADS), f32, math.log(1e-3), math.log(1e-1)))
    return {
        "x": jax.random.normal(ks[0], (BATCH, SEQ, D_MODEL), f32),
        "attn_norm": gain(ks[1], (L, D_MODEL)),
        "w_in": normal(ks[2], (L, D_MODEL, D_IN), D_MODEL ** -0.5),
        "sc_conv": normal(ks[3], (L, SC_KERNEL, GROUP_WIDTH), SC_KERNEL ** -0.5),
        "lam_q1": normal(ks[4], (L, HEAD_DIM), 0.1),
        "lam_k1": normal(ks[5], (L, HEAD_DIM), 0.1),
        "lam_q2": normal(ks[6], (L, HEAD_DIM), 0.1),
        "lam_k2": normal(ks[7], (L, HEAD_DIM), 0.1),
        "diff_norm": gain(ks[8], (L, 2 * HEAD_DIM)),
        "dn_conv": normal(ks[9], (L, DN_CONV, 3 * GROUP_WIDTH), DN_CONV ** -0.5),
        "dn_a_log": jnp.log(jax.random.uniform(ks[10], (L, DN_HEADS), f32, 1.0, 16.0)),
        "dn_dt_bias": dt + jnp.log(-jnp.expm1(-dt)),
        "dn_norm": gain(ks[11], (L, HEAD_DIM)),
        "fox_bias": 2.0 + 0.1 * jax.random.normal(ks[13], (L, FOX_HEADS), f32),
        "w_out": normal(ks[14], (L, D_MIX, D_MODEL), D_MIX ** -0.5),
        "ffn_norm": gain(ks[15], (L, D_MODEL)),
        "w_gate": normal(ks[16], (L, D_MODEL, D_FF), D_MODEL ** -0.5),
        "w_up": normal(ks[17], (L, D_MODEL, D_FF), D_MODEL ** -0.5),
        "ffn_conv": normal(ks[18], (L, FFN_KERNEL, D_FF), FFN_KERNEL ** -0.5),
        "w_down": normal(ks[19], (L, D_FF, D_MODEL), D_FF ** -0.5),
        "final_norm": gain(ks[20], (D_MODEL,)),
    }


def reference(x, attn_norm, w_in, sc_conv, lam_q1, lam_k1, lam_q2, lam_k2, diff_norm,
              dn_conv, dn_a_log, dn_dt_bias, dn_norm, fox_bias, w_out,
              ffn_norm, w_gate, w_up, ffn_conv, w_down, final_norm):
    for l in range(DEPTH):
        x = hybrid_layer(x, l, attn_norm[l], w_in[l], sc_conv[l], lam_q1[l], lam_k1[l], lam_q2[l], lam_k2[l],
                         diff_norm[l], dn_conv[l], dn_a_log[l], dn_dt_bias[l], dn_norm[l], fox_bias[l], w_out[l],
                         ffn_norm[l], w_gate[l], w_up[l], ffn_conv[l], w_down[l])
    return rms_norm(x, final_norm)
```

```python
import functools
import math

import jax
import jax.numpy as jnp
from jax import lax
from jax.experimental import pallas as pl
from jax.experimental.pallas import tpu as pltpu

_F32 = jnp.float32
_BF16 = jnp.bfloat16
_EPS = 1e-6
_NEG_INF = float("-inf")

_V7X_VMEM_BYTES = 64 * 1024 * 1024
_V7X_LANES = 128
_V7X_SUBLANES = 8
_BF16_ROWS = 16
_DN_CHUNK = 128
_FF_ALIGN = 1024


def _tile(n, target, align):
    t = min(target, n)
    t -= t % align
    while t >= align:
        if n % t == 0:
            return t
        t -= align
    return n


def _params(semantics, vmem_bytes):
    limit = min(int(vmem_bytes) + (8 << 20), _V7X_VMEM_BYTES - (4 << 20))
    return pltpu.CompilerParams(dimension_semantics=semantics, vmem_limit_bytes=limit)


def _silu(x):
    return x * (1.0 / (1.0 + jnp.exp(-x)))


def _softplus(x):
    return jnp.maximum(x, 0.0) + jnp.log1p(jnp.exp(-jnp.abs(x)))


def _shift_rows(x, prev8, s):
    xs = pltpu.roll(x, s, 0)
    ps = pltpu.roll(prev8, s, 0)
    row = lax.broadcasted_iota(jnp.int32, prev8.shape, 0)
    head = jnp.where(row < s, ps, xs[:_V7X_SUBLANES])
    return jnp.concatenate([head, xs[_V7X_SUBLANES:]], axis=0)


def _causal_conv(x, prev8, w):
    k = w.shape[0]
    out = x * w[k - 1:k]
    for j in range(k - 1):
        out = out + _shift_rows(x, prev8, k - 1 - j) * w[j:j + 1]
    return out


def _rmsnorm_kernel(x_ref, g_ref, o_ref):
    x = x_ref[...]
    y = x * lax.rsqrt(jnp.mean(x * x, axis=-1, keepdims=True) + _EPS)
    o_ref[...] = (y * g_ref[...]).astype(o_ref.dtype)


def _rmsnorm(x, g, out_dtype):
    n, d = x.shape
    tm = _tile(n, 256, _BF16_ROWS)
    vmem = 2 * tm * d * (4 + jnp.dtype(out_dtype).itemsize) + 2 * d * 4
    return pl.pallas_call(
        _rmsnorm_kernel,
        grid=(n // tm,),
        in_specs=[pl.BlockSpec((tm, d), lambda i: (i, 0)), pl.BlockSpec((1, d), lambda i: (0, 0))],
        out_specs=pl.BlockSpec((tm, d), lambda i: (i, 0)),
        out_shape=jax.ShapeDtypeStruct((n, d), out_dtype),
        compiler_params=_params(("parallel",), vmem),
        name="rmsnorm",
    )(x, g.reshape(1, d).astype(_F32))


def _mm_kernel(*refs, n_pairs, has_res):
    a_refs, w_refs = refs[:n_pairs], refs[n_pairs:2 * n_pairs]
    o_ref = refs[-1]
    acc = jnp.dot(a_refs[0][...], w_refs[0][...], preferred_element_type=_F32)
    for a_ref, w_ref in zip(a_refs[1:], w_refs[1:]):
        acc = acc + jnp.dot(a_ref[...], w_ref[...], preferred_element_type=_F32)
    if has_res:
        acc = acc + refs[2 * n_pairs][...]
    o_ref[...] = acc.astype(o_ref.dtype)


def _matmul(a_list, w_list, out_dtype, residual=None, tm_target=1024, tn_target=1024, name="matmul"):
    m, n = a_list[0].shape[0], w_list[0].shape[1]
    tm = _tile(m, tm_target, _BF16_ROWS)
    tn = _tile(n, tn_target, _V7X_LANES)
    ktot = sum(a.shape[1] for a in a_list)
    in_specs = [pl.BlockSpec((tm, a.shape[1]), lambda i, j: (i, 0)) for a in a_list]
    in_specs += [pl.BlockSpec((w.shape[0], tn), lambda i, j: (0, j)) for w in w_list]
    args = list(a_list) + list(w_list)
    vmem = 2 * 2 * ktot * (tm + tn) + 2 * tm * tn * jnp.dtype(out_dtype).itemsize + tm * tn * 4
    if residual is not None:
        in_specs.append(pl.BlockSpec((tm, tn), lambda i, j: (i, j)))
        args.append(residual)
        vmem += 2 * tm * tn * 4
    return pl.pallas_call(
        functools.partial(_mm_kernel, n_pairs=len(a_list), has_res=residual is not None),
        grid=(m // tm, n // tn),
        in_specs=in_specs,
        out_specs=pl.BlockSpec((tm, tn), lambda i, j: (i, j)),
        out_shape=jax.ShapeDtypeStruct((m, n), out_dtype),
        compiler_params=_params(("parallel", "parallel"), vmem),
        name=name,
    )(*args)


def _mm_acc_kernel(a_ref, w_ref, r_ref, o_ref, acc_ref):
    k = pl.program_id(2)

    @pl.when(k == 0)
    def _():
        acc_ref[...] = r_ref[...]

    acc_ref[...] += jnp.dot(a_ref[...], w_ref[...], preferred_element_type=_F32)

    @pl.when(k == pl.num_programs(2) - 1)
    def _():
        o_ref[...] = acc_ref[...]


def _matmul_acc(a, w, residual, tm_target=1024, tn_target=1024, tk_target=2816, name="matmul_acc"):
    m, kdim = a.shape
    n = w.shape[1]
    tm = _tile(m, tm_target, _BF16_ROWS)
    tn = _tile(n, tn_target, _V7X_LANES)
    tk = _tile(kdim, tk_target, _V7X_LANES)
    vmem = 2 * 2 * tk * (tm + tn) + 5 * tm * tn * 4
    return pl.pallas_call(
        _mm_acc_kernel,
        grid=(m // tm, n // tn, kdim // tk),
        in_specs=[pl.BlockSpec((tm, tk), lambda i, j, k: (i, k)),
                  pl.BlockSpec((tk, tn), lambda i, j, k: (k, j)),
                  pl.BlockSpec((tm, tn), lambda i, j, k: (i, j))],
        out_specs=pl.BlockSpec((tm, tn), lambda i, j, k: (i, j)),
        out_shape=jax.ShapeDtypeStruct((m, n), _F32),
        scratch_shapes=[pltpu.VMEM((tm, tn), _F32)],
        compiler_params=_params(("parallel", "parallel", "arbitrary"), vmem),
        name=name,
    )(a, w, residual)


def _sconv_kernel(h_ref, c_ref, b_ref, hh_ref, ch_ref, w_ref, o_ref, *, blocks_per_seq):
    i = pl.program_id(0)
    x = c_ref[...].astype(_F32) * h_ref[...].astype(_F32)
    halo = ch_ref[...].astype(_F32) * hh_ref[...].astype(_F32)
    prev8 = jnp.where(i % blocks_per_seq == 0, 0.0, halo[_BF16_ROWS - _V7X_SUBLANES:])
    y = b_ref[...].astype(_F32) * _causal_conv(x, prev8, w_ref[...])
    o_ref[...] = y.astype(o_ref.dtype)


def _short_conv(p, conv_w, seq_len, col_h, col_c, col_b, width):
    n = p.shape[0]
    tm = _tile(seq_len, 512, _BF16_ROWS)
    hb = tm // _BF16_ROWS
    cur = lambda c: pl.BlockSpec((tm, width), lambda i: (i, c))
    halo = lambda c: pl.BlockSpec((_BF16_ROWS, width), lambda i: (jnp.maximum(i * hb - 1, 0), c))
    k = conv_w.shape[0]
    vmem = 2 * (4 * tm + 2 * _BF16_ROWS) * width * 2 + 8 * tm * width * 4
    return pl.pallas_call(
        functools.partial(_sconv_kernel, blocks_per_seq=seq_len // tm),
        grid=(n // tm,),
        in_specs=[cur(col_h), cur(col_c), cur(col_b), halo(col_h), halo(col_c),
                  pl.BlockSpec((k, width), lambda i: (0, 0))],
        out_specs=pl.BlockSpec((tm, width), lambda i: (i, 0)),
        out_shape=jax.ShapeDtypeStruct((n, width), _BF16),
        compiler_params=_params(("parallel",), vmem),
        name="short_conv",
    )(p, p, p, p, p, conv_w.astype(_F32))


def _online_softmax_step(s, v, m_ref, l_ref, acc_ref):
    m_prev = m_ref[...]
    m_new = jnp.maximum(m_prev, jnp.max(s, axis=-1, keepdims=True))
    alpha = jnp.exp(m_prev - m_new)
    p = jnp.exp(s - m_new)
    l_ref[...] = alpha * l_ref[...] + jnp.sum(p, axis=-1, keepdims=True)
    acc_ref[...] = alpha * acc_ref[...] + jnp.dot(p.astype(v.dtype), v, preferred_element_type=_F32)
    m_ref[...] = m_new


def _causal_mask(s, q0, k0):
    row = q0 + lax.broadcasted_iota(jnp.int32, s.shape, 0)
    col = k0 + lax.broadcasted_iota(jnp.int32, s.shape, 1)
    return jnp.where(col <= row, s, _NEG_INF)


def _qk(q, k):
    return lax.dot_general(q, k, (((1,), (1,)), ((), ())), preferred_element_type=_F32)


def _kv_sweep(qi, bq, bk, step):
    r = bq // bk

    def full(j, carry):
        step(j, False)
        return carry

    lax.fori_loop(0, qi * r, full, 0)
    for d in range(r):
        step(qi * r + d, True)


def _diff_kernel(q_ref, k_ref, v_ref, lq1_ref, lk1_ref, lq2_ref, lk2_ref, g_ref, o_ref,
                 m_ref, l_ref, acc_ref, *, bq, bk, d, scale, lam_init):
    qi = pl.program_id(2)
    for mi in range(2):
        m_ref[mi] = jnp.full(m_ref.shape[1:], _NEG_INF, _F32)
        l_ref[mi] = jnp.zeros(l_ref.shape[1:], _F32)
        acc_ref[mi] = jnp.zeros(acc_ref.shape[1:], _F32)
    q = q_ref[...]

    def step(j, masked):
        k0 = pl.multiple_of(j * bk, bk)
        k = k_ref[pl.ds(k0, bk), :]
        v = v_ref[pl.ds(k0, bk), :]
        for mi in range(2):
            s = _qk(q[:, mi * d:(mi + 1) * d], k[:, mi * d:(mi + 1) * d]) * scale
            if masked:
                s = _causal_mask(s, qi * bq, k0)
            _online_softmax_step(s, v, m_ref.at[mi], l_ref.at[mi], acc_ref.at[mi])

    _kv_sweep(qi, bq, bk, step)
    lam = (jnp.exp(jnp.sum(lq1_ref[...] * lk1_ref[...], axis=-1, keepdims=True))
           - jnp.exp(jnp.sum(lq2_ref[...] * lk2_ref[...], axis=-1, keepdims=True)) + lam_init)
    o = acc_ref[0] / l_ref[0] - lam * (acc_ref[1] / l_ref[1])
    o = o * lax.rsqrt(jnp.mean(o * o, axis=-1, keepdims=True) + _EPS) * g_ref[...]
    o_ref[...] = (o * (1.0 - lam_init)).astype(o_ref.dtype)


def _diff_attention(p, lam_vecs, norm_g, lam_init, batch, seq_len, col_q, col_k, col_v, heads, d):
    n = p.shape[0]
    hw = 2 * d
    bq = _tile(seq_len, 512, _BF16_ROWS)
    bk = _tile(bq, 256, _BF16_ROWS)
    nq = seq_len // bq
    vec = pl.BlockSpec((1, d), lambda b, h, i: (0, 0))
    vmem = 2 * 2 * seq_len * hw * 2 + 4 * bq * hw * 2 + 2 * bq * (hw + 2 * _V7X_LANES) * 4 + 6 * bq * bk * 4
    return pl.pallas_call(
        functools.partial(_diff_kernel, bq=bq, bk=bk, d=d, scale=d ** -0.5, lam_init=lam_init),
        grid=(batch, heads, nq),
        in_specs=[pl.BlockSpec((bq, hw), lambda b, h, i: (b * nq + i, col_q // hw + h)),
                  pl.BlockSpec((seq_len, hw), lambda b, h, i: (b, col_k // hw + h)),
                  pl.BlockSpec((seq_len, hw), lambda b, h, i: (b, col_v // hw + h)),
                  vec, vec, vec, vec,
                  pl.BlockSpec((1, hw), lambda b, h, i: (0, 0))],
        out_specs=pl.BlockSpec((bq, hw), lambda b, h, i: (b * nq + i, h)),
        out_shape=jax.ShapeDtypeStruct((n, heads * hw), _BF16),
        scratch_shapes=[pltpu.VMEM((2, bq, 1), _F32), pltpu.VMEM((2, bq, 1), _F32),
                        pltpu.VMEM((2, bq, hw), _F32)],
        compiler_params=_params(("parallel", "parallel", "arbitrary"), vmem),
        name="diff_attention",
    )(p, p, p, *[v.reshape(1, d).astype(_F32) for v in lam_vecs], norm_g.reshape(1, hw).astype(_F32))


def _fox_cum_kernel(f_ref, bias_ref, o_ref):
    x = f_ref[...] + bias_ref[...]
    ls = jnp.minimum(x, 0.0) - jnp.log1p(jnp.exp(-jnp.abs(x)))
    r = lax.broadcasted_iota(jnp.int32, (_V7X_LANES, _V7X_LANES), 0)
    c = lax.broadcasted_iota(jnp.int32, (_V7X_LANES, _V7X_LANES), 1)
    upper = (r <= c).astype(_F32)
    carry = jnp.zeros((x.shape[0], 1), _F32)
    for t in range(x.shape[1] // _V7X_LANES):
        sl = slice(t * _V7X_LANES, (t + 1) * _V7X_LANES)
        w = jnp.dot(ls[:, sl], upper, precision=lax.Precision.HIGHEST, preferred_element_type=_F32) + carry
        o_ref[:, sl] = w
        carry = w[:, _V7X_LANES - 1:]


def _fox_cumsum(f_rows, bias_rows):
    return pl.pallas_call(
        _fox_cum_kernel,
        out_shape=jax.ShapeDtypeStruct(f_rows.shape, _F32),
        name="fox_cumsum",
    )(f_rows, bias_rows)


def _fox_kernel(q_ref, k_ref, v_ref, cq_ref, ck_ref, o_ref, m_ref, l_ref, acc_ref, *, bq, bk, scale):
    qi = pl.program_id(2)
    m_ref[...] = jnp.full(m_ref.shape, _NEG_INF, _F32)
    l_ref[...] = jnp.zeros(l_ref.shape, _F32)
    acc_ref[...] = jnp.zeros(acc_ref.shape, _F32)
    q = q_ref[...]
    cq = cq_ref[0, 0]

    def step(j, masked):
        k0 = pl.multiple_of(j * bk, bk)
        s = _qk(q, k_ref[pl.ds(k0, bk), :]) * scale
        s = s + cq - ck_ref[0, 0, pl.ds(j, 1), :]
        if masked:
            s = _causal_mask(s, qi * bq, k0)
        _online_softmax_step(s, v_ref[pl.ds(k0, bk), :], m_ref, l_ref, acc_ref)

    _kv_sweep(qi, bq, bk, step)
    o_ref[...] = (acc_ref[...] / l_ref[...]).astype(o_ref.dtype)


def _fox_attention(p, cum, batch, seq_len, col_q, col_k, col_v, heads, d):
    n = p.shape[0]
    bq = _tile(seq_len, 512, _BF16_ROWS)
    bk = _tile(bq, 256, _V7X_LANES)
    nq, nk = seq_len // bq, seq_len // bk
    cum_rows = cum.reshape(batch, heads, nk, bk)
    cum_cols = cum.reshape(batch, heads, seq_len, 1)
    vmem = (2 * 2 * seq_len * d * 2 + 4 * bq * d * 2 + 2 * seq_len * 4 + 2 * bq * _V7X_LANES * 4
            + bq * (d + 2 * _V7X_LANES) * 4 + 6 * bq * bk * 4)
    return pl.pallas_call(
        functools.partial(_fox_kernel, bq=bq, bk=bk, scale=d ** -0.5),
        grid=(batch, heads, nq),
        in_specs=[pl.BlockSpec((bq, d), lambda b, h, i: (b * nq + i, col_q // d + h)),
                  pl.BlockSpec((seq_len, d), lambda b, h, i: (b, col_k // d + h)),
                  pl.BlockSpec((seq_len, d), lambda b, h, i: (b, col_v // d + h)),
                  pl.BlockSpec((1, 1, bq, 1), lambda b, h, i: (b, h, i, 0)),
                  pl.BlockSpec((1, 1, nk, bk), lambda b, h, i: (b, h, 0, 0))],
        out_specs=pl.BlockSpec((bq, d), lambda b, h, i: (b * nq + i, h)),
        out_shape=jax.ShapeDtypeStruct((n, heads * d), _BF16),
        scratch_shapes=[pltpu.VMEM((bq, 1), _F32), pltpu.VMEM((bq, 1), _F32), pltpu.VMEM((bq, d), _F32)],
        compiler_params=_params(("parallel", "parallel", "arbitrary"), vmem),
        name="fox_attention",
    )(p, p, p, cum_cols, cum_rows)


def _lane_column(x, h):
    lane = lax.broadcasted_iota(jnp.int32, x.shape, 1)
    return jnp.sum(jnp.where(lane == h, x, 0.0), axis=-1, keepdims=True)


def _bdot(a, b):
    return jnp.dot(a.astype(_BF16), b.astype(_BF16), preferred_element_type=_F32)


def _deltanet_kernel(q_ref, k_ref, v_ref, z_ref, gc_ref, gr_ref, cw_ref, alog_c_ref, dtb_c_ref,
                     alog_r_ref, dtb_r_ref, ng_ref, o_ref, s_ref, tail_ref, *, heads, d, c):
    t = pl.program_id(1)
    width = heads * d

    @pl.when(t == 0)
    def _():
        s_ref[...] = jnp.zeros(s_ref.shape, _F32)
        tail_ref[...] = jnp.zeros(tail_ref.shape, _F32)

    cw = cw_ref[...]
    conv = []
    for idx, ref in enumerate((q_ref, k_ref, v_ref)):
        raw = ref[...].astype(_F32)
        cols = slice(idx * width, (idx + 1) * width)
        conv.append(_silu(_causal_conv(raw, tail_ref[:, cols], cw[:, cols])))
        tail_ref[:, cols] = raw[c - _V7X_SUBLANES:]

    gates_c = gc_ref[...]
    g_cols = -jnp.exp(alog_c_ref[...]) * _softplus(gates_c + dtb_c_ref[...])
    beta_cols = 1.0 / (1.0 + jnp.exp(-gates_c))
    gates_r = gr_ref[0]
    g_rows = -jnp.exp(alog_r_ref[...]) * _softplus(gates_r[:heads] + dtb_r_ref[...])
    ri = lax.broadcasted_iota(jnp.int32, (c, c), 0)
    ci = lax.broadcasted_iota(jnp.int32, (c, c), 1)
    incl = ri >= ci
    strict = ri > ci
    eye = (ri == ci).astype(_F32)
    hp = lax.Precision.HIGHEST
    gcum_cols = jnp.dot(incl.astype(_F32), g_cols, precision=hp, preferred_element_type=_F32)
    gcum_rows = jnp.dot(g_rows, (ri <= ci).astype(_F32), precision=hp, preferred_element_type=_F32)

    for h in range(heads):
        hs = slice(h * d, (h + 1) * d)
        qh, kh, vh = conv[0][:, hs], conv[1][:, hs], conv[2][:, hs]
        qh = qh * lax.rsqrt(jnp.sum(qh * qh, axis=-1, keepdims=True) + _EPS) * (d ** -0.5)
        kh = kh * lax.rsqrt(jnp.sum(kh * kh, axis=-1, keepdims=True) + _EPS)
        gc = _lane_column(gcum_cols, h)
        gcr = gcum_rows[h:h + 1]
        beta = _lane_column(beta_cols, heads + h)
        decay = jnp.exp(jnp.where(incl, gc - gcr, _NEG_INF))
        kb = kh * beta
        a = jnp.where(strict, _qk(kb.astype(_BF16), kh.astype(_BF16)) * decay, 0.0)
        pw = -a
        tinv = eye + pw
        n_sq = max(1, math.ceil(math.log2(c)) - 1)
        for _ in range(n_sq):
            pw = _bdot(pw, pw)
            tinv = tinv + _bdot(tinv, pw)
        egc = jnp.exp(gc)
        wu = _bdot(tinv, jnp.concatenate([kb * egc, vh * beta], axis=1))
        w, u = wu[:, :d], wu[:, d:]
        aqk = jnp.where(incl, _qk(qh.astype(_BF16), kh.astype(_BF16)) * decay, 0.0)
        s = s_ref[h]
        ws_qs = _bdot(jnp.concatenate([w, qh * egc], axis=0), s)
        v_new = u - ws_qs[:c]
        o = ws_qs[c:] + _bdot(aqk, v_new)
        g_last = gc[c - 1:c]
        k_dec = kh * jnp.exp(g_last - gc)
        s_ref[h] = s * jnp.exp(g_last) + lax.dot_general(
            k_dec.astype(_BF16), v_new.astype(_BF16), (((0,), (0,)), ((), ())), preferred_element_type=_F32)
        on = o * lax.rsqrt(jnp.mean(o * o, axis=-1, keepdims=True) + _EPS) * ng_ref[...]
        o_ref[:, hs] = (on * _silu(z_ref[:, hs].astype(_F32))).astype(o_ref.dtype)


def _gated_deltanet(p, gates_cols, gates_rows, conv_w, a_log, dt_bias, norm_g, batch, seq_len,
                    col_q, col_k, col_v, col_z, heads, d):
    n = p.shape[0]
    c = _tile(seq_len, _DN_CHUNK, _BF16_ROWS)
    nt = seq_len // c
    width = heads * d
    lanes = gates_cols.shape[1]
    kw = conv_w.shape[0]
    col = lambda off: pl.BlockSpec((c, width), lambda b, t: (b * nt + t, off // width))
    const = lambda shape: pl.BlockSpec(shape, lambda b, t: (0,) * len(shape))
    pad_lane = lambda v: jnp.zeros((1, lanes), _F32).at[0, :heads].set(v.astype(_F32))
    vmem = 2 * 5 * c * width * 2 + 12 * c * width * 4 + heads * d * d * 4 + 2 * kw * 3 * width * 4
    return pl.pallas_call(
        functools.partial(_deltanet_kernel, heads=heads, d=d, c=c),
        grid=(batch, nt),
        in_specs=[col(col_q), col(col_k), col(col_v), col(col_z),
                  pl.BlockSpec((c, lanes), lambda b, t: (b * nt + t, 0)),
                  pl.BlockSpec((1, gates_rows.shape[1], c), lambda b, t: (b, 0, t)),
                  const((kw, 3 * width)), const((1, lanes)), const((1, lanes)),
                  const((heads, 1)), const((heads, 1)), const((1, d))],
        out_specs=pl.BlockSpec((c, width), lambda b, t: (b * nt + t, 0)),
        out_shape=jax.ShapeDtypeStruct((n, width), _BF16),
        scratch_shapes=[pltpu.VMEM((heads, d, d), _F32), pltpu.VMEM((_V7X_SUBLANES, 3 * width), _F32)],
        compiler_params=_params(("parallel", "arbitrary"), vmem),
        name="gated_deltanet",
    )(p, p, p, p, gates_cols, gates_rows, conv_w.astype(_F32), pad_lane(a_log), pad_lane(dt_bias),
      a_log.reshape(heads, 1).astype(_F32), dt_bias.reshape(heads, 1).astype(_F32),
      norm_g.reshape(1, d).astype(_F32))


def _ffn_in_kernel(h_ref, halo_ref, wg_ref, wu_ref, cw_ref, o_ref, *, blocks_per_seq):
    i = pl.program_id(0)
    h = h_ref[...]
    g = jnp.dot(h, wg_ref[...], preferred_element_type=_F32)
    u = jnp.dot(h, wu_ref[...], preferred_element_type=_F32)
    gh = jnp.dot(halo_ref[...], wg_ref[...], preferred_element_type=_F32)
    prev8 = jnp.where(i % blocks_per_seq == 0, 0.0, gh[_BF16_ROWS - _V7X_SUBLANES:])
    a = _causal_conv(g, prev8, cw_ref[...])
    o_ref[...] = (_silu(a) * u).astype(o_ref.dtype)


def _ffn_in(h, w_gate, w_up, conv_w, seq_len, tm_target=1024, tn_target=512):
    n, d = h.shape
    ff = w_gate.shape[1]
    tm = _tile(seq_len, tm_target, _BF16_ROWS)
    tn = _tile(ff, tn_target, _V7X_LANES)
    hb = tm // _BF16_ROWS
    kw = conv_w.shape[0]
    vmem = 2 * 2 * d * (tm + _BF16_ROWS + 2 * tn) + 2 * tm * tn * 2 + 6 * tm * tn * 4
    return pl.pallas_call(
        functools.partial(_ffn_in_kernel, blocks_per_seq=seq_len // tm),
        grid=(n // tm, ff // tn),
        in_specs=[pl.BlockSpec((tm, d), lambda i, j: (i, 0)),
                  pl.BlockSpec((_BF16_ROWS, d), lambda i, j: (jnp.maximum(i * hb - 1, 0), 0)),
                  pl.BlockSpec((d, tn), lambda i, j: (0, j)),
                  pl.BlockSpec((d, tn), lambda i, j: (0, j)),
                  pl.BlockSpec((kw, tn), lambda i, j: (0, j))],
        out_specs=pl.BlockSpec((tm, tn), lambda i, j: (i, j)),
        out_shape=jax.ShapeDtypeStruct((n, ff), _BF16),
        compiler_params=_params(("parallel", "parallel"), vmem),
        name="ffn_in",
    )(h, h, w_gate, w_up, conv_w)


def _pad_cols(w, total):
    return jnp.pad(w, ((0, 0), (0, total - w.shape[1])))


def _layer(x, layer, batch, seq_len, attn_norm, w_in, sc_conv, lam_vecs, diff_norm, dn_conv, dn_a_log,
           dn_dt_bias, dn_norm, fox_bias, w_out, ffn_norm, w_gate, w_up, ffn_conv, w_down):
    d = lam_vecs[0].shape[-1]
    gw = w_out.shape[0] // 4
    dn_heads, fox_heads = dn_a_log.shape[-1], fox_bias.shape[-1]
    diff_heads = gw // diff_norm.shape[-1]
    o_dn_a = 10 * gw
    o_dn_b = o_dn_a + dn_heads
    o_fx = o_dn_b + dn_heads
    o_fx_f = o_fx + 3 * gw
    w_wide = jnp.concatenate([w_in[:, :o_dn_a], w_in[:, o_fx:o_fx_f]], axis=1).astype(_BF16)
    w_gates = _pad_cols(jnp.concatenate([w_in[:, o_dn_a:o_fx], w_in[:, o_fx_f:]], axis=1), _V7X_LANES).astype(_BF16)
    col = lambda idx: idx * gw

    h = _rmsnorm(x, attn_norm, _BF16)
    p = _matmul([h], [w_wide], _BF16, name="in_proj")
    gates = _matmul([h], [w_gates], _F32, tn_target=_V7X_LANES, name="in_proj_gates")

    y_sc = _short_conv(p, sc_conv, seq_len, 0, 1, 2, gw)
    lam_init = 0.8 - 0.6 * math.exp(-0.3 * layer)
    y_df = _diff_attention(p, lam_vecs, diff_norm, lam_init, batch, seq_len, col(3), col(4), col(5), diff_heads, d)

    gates_t = gates.reshape(batch, seq_len, _V7X_LANES).transpose(0, 2, 1)
    fox_rows = gates_t[:, 2 * dn_heads:2 * dn_heads + fox_heads].reshape(batch * fox_heads, seq_len)
    fox_bias_rows = jnp.tile(fox_bias.astype(_F32), batch).reshape(batch * fox_heads, 1)
    cum = _fox_cumsum(fox_rows, fox_bias_rows)
    y_fx = _fox_attention(p, cum, batch, seq_len, col(10), col(11), col(12), fox_heads, d)

    y_dn = _gated_deltanet(p, gates, gates_t[:, :2 * dn_heads], dn_conv, dn_a_log, dn_dt_bias, dn_norm,
                           batch, seq_len, col(6), col(7), col(8), col(9), dn_heads, d)

    w_out_b = w_out.astype(_BF16)
    x = _matmul([y_sc, y_df, y_dn, y_fx], [w_out_b[i * gw:(i + 1) * gw] for i in range(4)], _F32,
                residual=x, tn_target=512, name="out_proj")

    ff = w_gate.shape[1]
    ffp = -(-ff // _FF_ALIGN) * _FF_ALIGN if ff > _FF_ALIGN else ff
    h2 = _rmsnorm(x, ffn_norm, _BF16)
    act = _ffn_in(h2, _pad_cols(w_gate, ffp).astype(_BF16), _pad_cols(w_up, ffp).astype(_BF16),
                  _pad_cols(ffn_conv.astype(_F32), ffp), seq_len)
    w_down_b = jnp.pad(w_down, ((0, ffp - ff), (0, 0))).astype(_BF16)
    return _matmul_acc(act, w_down_b, x, name="ffn_out")


def kernel(x, attn_norm, w_in, sc_conv, lam_q1, lam_k1, lam_q2, lam_k2, diff_norm, dn_conv, dn_a_log,
           dn_dt_bias, dn_norm, fox_bias, w_out, ffn_norm, w_gate, w_up, ffn_conv, w_down, final_norm):
    batch, seq_len, d_model = x.shape
    xf = x.reshape(batch * seq_len, d_model).astype(_F32)
    for l in range(w_in.shape[0]):
        xf = _layer(xf, l, batch, seq_len, attn_norm[l], w_in[l], sc_conv[l],
                    (lam_q1[l], lam_k1[l], lam_q2[l], lam_k2[l]), diff_norm[l], dn_conv[l], dn_a_log[l],
                    dn_dt_bias[l], dn_norm[l], fox_bias[l], w_out[l], ffn_norm[l], w_gate[l], w_up[l],
                    ffn_conv[l], w_down[l])
    out = _rmsnorm(xf, final_norm, x.dtype)
    return out.reshape(batch, seq_len, d_model)
```

```python
import functools
import math

import jax
import jax.numpy as jnp
from jax import lax
from jax.experimental import pallas as pl
from jax.experimental.pallas import tpu as pltpu

_F32 = jnp.float32
_BF16 = jnp.bfloat16
_EPS = 1e-6
_NEG_INF = float("-inf")

_V7X_VMEM_BYTES = 64 * 1024 * 1024
_V7X_LANES = 128
_V7X_SUBLANES = 8
_BF16_ROWS = 16
_DN_CHUNK = 128
_FF_ALIGN = 1024


def _tile(n, target, align):
    t = min(target, n)
    t -= t % align
    while t >= align:
        if n % t == 0:
            return t
        t -= align
    return n


def _params(semantics, vmem_bytes):
    limit = min(int(vmem_bytes) + (8 << 20), _V7X_VMEM_BYTES - (4 << 20))
    return pltpu.CompilerParams(dimension_semantics=semantics, vmem_limit_bytes=limit)


def _silu(x):
    return x * (1.0 / (1.0 + jnp.exp(-x)))


def _softplus(x):
    return jnp.maximum(x, 0.0) + jnp.log1p(jnp.exp(-jnp.abs(x)))


def _shift_rows(x, prev8, s):
    xs = pltpu.roll(x, s, 0)
    ps = pltpu.roll(prev8, s, 0)
    row = lax.broadcasted_iota(jnp.int32, prev8.shape, 0)
    head = jnp.where(row < s, ps, xs[:_V7X_SUBLANES])
    return jnp.concatenate([head, xs[_V7X_SUBLANES:]], axis=0)


def _causal_conv(x, prev8, w):
    k = w.shape[0]
    out = x * w[k - 1:k]
    for j in range(k - 1):
        out = out + _shift_rows(x, prev8, k - 1 - j) * w[j:j + 1]
    return out


def _rmsnorm_kernel(x_ref, g_ref, o_ref):
    x = x_ref[...]
    y = x * lax.rsqrt(jnp.mean(x * x, axis=-1, keepdims=True) + _EPS)
    o_ref[...] = (y * g_ref[...]).astype(o_ref.dtype)


def _rmsnorm(x, g, out_dtype):
    n, d = x.shape
    tm = _tile(n, 256, _BF16_ROWS)
    vmem = 2 * tm * d * (4 + jnp.dtype(out_dtype).itemsize) + 2 * d * 4
    return pl.pallas_call(
        _rmsnorm_kernel,
        grid=(n // tm,),
        in_specs=[pl.BlockSpec((tm, d), lambda i: (i, 0)), pl.BlockSpec((1, d), lambda i: (0, 0))],
        out_specs=pl.BlockSpec((tm, d), lambda i: (i, 0)),
        out_shape=jax.ShapeDtypeStruct((n, d), out_dtype),
        compiler_params=_params(("parallel",), vmem),
        name="rmsnorm",
    )(x, g.reshape(1, d).astype(_F32))


def _mm_kernel(*refs, n_pairs, has_res):
    a_refs, w_refs = refs[:n_pairs], refs[n_pairs:2 * n_pairs]
    o_ref = refs[-1]
    acc = jnp.dot(a_refs[0][...], w_refs[0][...], preferred_element_type=_F32)
    for a_ref, w_ref in zip(a_refs[1:], w_refs[1:]):
        acc = acc + jnp.dot(a_ref[...], w_ref[...], preferred_element_type=_F32)
    if has_res:
        acc = acc + refs[2 * n_pairs][...]
    o_ref[...] = acc.astype(o_ref.dtype)


def _matmul(a_list, w_list, out_dtype, residual=None, tm_target=1024, tn_target=1024, name="matmul"):
    m, n = a_list[0].shape[0], w_list[0].shape[1]
    tm = _tile(m, tm_target, _BF16_ROWS)
    tn = _tile(n, tn_target, _V7X_LANES)
    ktot = sum(a.shape[1] for a in a_list)
    in_specs = [pl.BlockSpec((tm, a.shape[1]), lambda i, j: (i, 0)) for a in a_list]
    row_block = [idx if w.shape[0] != a.shape[1] else 0 for idx, (a, w) in enumerate(zip(a_list, w_list))]
    in_specs += [pl.BlockSpec((a.shape[1], tn), functools.partial(lambda i, j, r: (r, j), r=r))
                 for a, r in zip(a_list, row_block)]
    args = list(a_list) + list(w_list)
    vmem = 2 * 2 * ktot * (tm + tn) + 2 * tm * tn * jnp.dtype(out_dtype).itemsize + tm * tn * 4
    if residual is not None:
        in_specs.append(pl.BlockSpec((tm, tn), lambda i, j: (i, j)))
        args.append(residual)
        vmem += 2 * tm * tn * 4
    return pl.pallas_call(
        functools.partial(_mm_kernel, n_pairs=len(a_list), has_res=residual is not None),
        grid=(m // tm, n // tn),
        in_specs=in_specs,
        out_specs=pl.BlockSpec((tm, tn), lambda i, j: (i, j)),
        out_shape=jax.ShapeDtypeStruct((m, n), out_dtype),
        compiler_params=_params(("parallel", "parallel"), vmem),
        name=name,
    )(*args)


def _mm_acc_kernel(a_ref, w_ref, r_ref, o_ref, acc_ref):
    k = pl.program_id(2)

    @pl.when(k == 0)
    def _():
        acc_ref[...] = r_ref[...]

    acc_ref[...] += jnp.dot(a_ref[...], w_ref[...], preferred_element_type=_F32)

    @pl.when(k == pl.num_programs(2) - 1)
    def _():
        o_ref[...] = acc_ref[...]


def _matmul_acc(a, w, residual, tm_target=1024, tn_target=1024, tk_target=2816, name="matmul_acc"):
    m, kdim = a.shape
    n = w.shape[1]
    tm = _tile(m, tm_target, _BF16_ROWS)
    tn = _tile(n, tn_target, _V7X_LANES)
    tk = _tile(kdim, tk_target, _V7X_LANES)
    vmem = 2 * 2 * tk * (tm + tn) + 5 * tm * tn * 4
    return pl.pallas_call(
        _mm_acc_kernel,
        grid=(m // tm, n // tn, kdim // tk),
        in_specs=[pl.BlockSpec((tm, tk), lambda i, j, k: (i, k)),
                  pl.BlockSpec((tk, tn), lambda i, j, k: (k, j)),
                  pl.BlockSpec((tm, tn), lambda i, j, k: (i, j))],
        out_specs=pl.BlockSpec((tm, tn), lambda i, j, k: (i, j)),
        out_shape=jax.ShapeDtypeStruct((m, n), _F32),
        scratch_shapes=[pltpu.VMEM((tm, tn), _F32)],
        compiler_params=_params(("parallel", "parallel", "arbitrary"), vmem),
        name=name,
    )(a, w, residual)


def _sconv_kernel(h_ref, c_ref, b_ref, hh_ref, ch_ref, w_ref, o_ref, *, blocks_per_seq):
    i = pl.program_id(0)
    x = c_ref[...].astype(_F32) * h_ref[...].astype(_F32)
    halo = ch_ref[...].astype(_F32) * hh_ref[...].astype(_F32)
    prev8 = jnp.where(i % blocks_per_seq == 0, 0.0, halo[_BF16_ROWS - _V7X_SUBLANES:])
    y = b_ref[...].astype(_F32) * _causal_conv(x, prev8, w_ref[...])
    o_ref[...] = y.astype(o_ref.dtype)


def _short_conv(p, conv_w, seq_len, col_h, col_c, col_b, width):
    n = p.shape[0]
    tm = _tile(seq_len, 512, _BF16_ROWS)
    hb = tm // _BF16_ROWS
    cur = lambda c: pl.BlockSpec((tm, width), lambda i: (i, c))
    halo = lambda c: pl.BlockSpec((_BF16_ROWS, width), lambda i: (jnp.maximum(i * hb - 1, 0), c))
    k = conv_w.shape[0]
    vmem = 2 * (4 * tm + 2 * _BF16_ROWS) * width * 2 + 8 * tm * width * 4
    return pl.pallas_call(
        functools.partial(_sconv_kernel, blocks_per_seq=seq_len // tm),
        grid=(n // tm,),
        in_specs=[cur(col_h), cur(col_c), cur(col_b), halo(col_h), halo(col_c),
                  pl.BlockSpec((k, width), lambda i: (0, 0))],
        out_specs=pl.BlockSpec((tm, width), lambda i: (i, 0)),
        out_shape=jax.ShapeDtypeStruct((n, width), _BF16),
        compiler_params=_params(("parallel",), vmem),
        name="short_conv",
    )(p, p, p, p, p, conv_w.astype(_F32))


_LOG2E = math.log2(math.e)


def _lane_tiles(x, width):
    return [x[:, c:c + _V7X_LANES] for c in range(0, width, _V7X_LANES)]


def _lane_repeat(x, width):
    return jnp.concatenate([x] * (width // _V7X_LANES), axis=1)


def _softmax_block(s, m_prev):
    m_new = jnp.maximum(m_prev, jnp.max(s, axis=-1, keepdims=True))
    alpha = jnp.exp2(m_prev - m_new)
    p = jnp.concatenate([jnp.exp2(t - m_new) for t in _lane_tiles(s, s.shape[1])], axis=1)
    return m_new, alpha, p


def _causal_mask(s, q0, k0):
    row = q0 + lax.broadcasted_iota(jnp.int32, s.shape, 0)
    col = k0 + lax.broadcasted_iota(jnp.int32, s.shape, 1)
    return jnp.where(col <= row, s, _NEG_INF)


def _qk(q, k):
    return lax.dot_general(q, k, (((1,), (1,)), ((), ())), preferred_element_type=_F32)


def _kv_sweep(qi, bq, bk, step):
    r = bq // bk
    n_full = qi * r

    def pair(i, carry):
        step(2 * i, False)
        step(2 * i + 1, False)
        return carry

    lax.fori_loop(0, n_full // 2, pair, 0)

    @pl.when(n_full % 2 == 1)
    def _():
        step(n_full - 1, False)

    for d in range(r):
        step(n_full + d, True)


def _diff_kernel(q_ref, k_ref, v_ref, lq1_ref, lk1_ref, lq2_ref, lk2_ref, g_ref, o_ref,
                 m_ref, l_ref, acc_ref, *, bq, bk, d, scale, lam_init):
    qi = pl.program_id(2)
    hw = 2 * d
    for mi in range(2):
        m_ref[mi] = jnp.full(m_ref.shape[1:], _NEG_INF, _F32)
        l_ref[mi] = jnp.zeros(l_ref.shape[1:], _F32)
        acc_ref[mi] = jnp.zeros(acc_ref.shape[1:], _F32)
    q = (q_ref[...].astype(_F32) * (scale * _LOG2E)).astype(_BF16)

    def step(j, masked):
        k0 = pl.multiple_of(j * bk, bk)
        k = k_ref[pl.ds(k0, bk), :]
        v = v_ref[pl.ds(k0, bk), :]
        for mi in range(2):
            s = _qk(q[:, mi * d:(mi + 1) * d], k[:, mi * d:(mi + 1) * d])
            if masked:
                s = _causal_mask(s, qi * bq, k0)
            m_new, alpha, p = _softmax_block(s, m_ref[mi])
            l_ref[mi] = alpha * l_ref[mi] + jnp.sum(p, axis=-1, keepdims=True)
            acc_ref[mi] = (_lane_repeat(alpha, hw) * acc_ref[mi]
                           + jnp.dot(p.astype(_BF16), v, preferred_element_type=_F32))
            m_ref[mi] = m_new

    _kv_sweep(qi, bq, bk, step)
    lam = (jnp.exp(jnp.sum(lq1_ref[...] * lk1_ref[...], axis=-1, keepdims=True))
           - jnp.exp(jnp.sum(lq2_ref[...] * lk2_ref[...], axis=-1, keepdims=True)) + lam_init)
    o = acc_ref[0] / _lane_repeat(l_ref[0], hw) - lam * (acc_ref[1] / _lane_repeat(l_ref[1], hw))
    o = o * lax.rsqrt(jnp.mean(o * o, axis=-1, keepdims=True) + _EPS) * g_ref[...]
    o_ref[...] = (o * (1.0 - lam_init)).astype(o_ref.dtype)


def _diff_attention(p, lam_vecs, norm_g, lam_init, batch, seq_len, col_q, col_k, col_v, heads, d):
    n = p.shape[0]
    hw = 2 * d
    bq = _tile(seq_len, 512, _V7X_LANES)
    bk = _tile(bq, 512, _V7X_LANES)
    nq = seq_len // bq
    vec = pl.BlockSpec((1, d), lambda b, h, i: (0, 0))
    vmem = 2 * 2 * seq_len * hw * 2 + 4 * bq * hw * 2 + 2 * bq * (hw + 2 * _V7X_LANES) * 4 + 8 * bq * bk * 4
    return pl.pallas_call(
        functools.partial(_diff_kernel, bq=bq, bk=bk, d=d, scale=d ** -0.5, lam_init=lam_init),
        grid=(batch, heads, nq),
        in_specs=[pl.BlockSpec((bq, hw), lambda b, h, i: (b * nq + i, col_q // hw + h)),
                  pl.BlockSpec((seq_len, hw), lambda b, h, i: (b, col_k // hw + h)),
                  pl.BlockSpec((seq_len, hw), lambda b, h, i: (b, col_v // hw + h)),
                  vec, vec, vec, vec,
                  pl.BlockSpec((1, hw), lambda b, h, i: (0, 0))],
        out_specs=pl.BlockSpec((bq, hw), lambda b, h, i: (b * nq + i, h)),
        out_shape=jax.ShapeDtypeStruct((n, heads * hw), _BF16),
        scratch_shapes=[pltpu.VMEM((2, bq, _V7X_LANES), _F32), pltpu.VMEM((2, bq, _V7X_LANES), _F32),
                        pltpu.VMEM((2, bq, hw), _F32)],
        compiler_params=_params(("parallel", "parallel", "arbitrary"), vmem),
        name="diff_attention",
    )(p, p, p, *[v.reshape(1, d).astype(_F32) for v in lam_vecs], norm_g.reshape(1, hw).astype(_F32))


def _fox_cum_kernel(f_ref, bias_ref, o_ref):
    x = f_ref[...] + bias_ref[...]
    ls = jnp.minimum(x, 0.0) - jnp.log1p(jnp.exp(-jnp.abs(x)))
    r = lax.broadcasted_iota(jnp.int32, (_V7X_LANES, _V7X_LANES), 0)
    c = lax.broadcasted_iota(jnp.int32, (_V7X_LANES, _V7X_LANES), 1)
    upper = (r <= c).astype(_F32)
    carry = jnp.zeros((x.shape[0], 1), _F32)
    for t in range(x.shape[1] // _V7X_LANES):
        sl = slice(t * _V7X_LANES, (t + 1) * _V7X_LANES)
        w = jnp.dot(ls[:, sl], upper, precision=lax.Precision.HIGHEST, preferred_element_type=_F32) + carry
        o_ref[:, sl] = w
        carry = w[:, _V7X_LANES - 1:]


def _fox_cumsum(f_rows, bias_rows):
    return pl.pallas_call(
        _fox_cum_kernel,
        out_shape=jax.ShapeDtypeStruct(f_rows.shape, _F32),
        name="fox_cumsum",
    )(f_rows, bias_rows)


def _fox_kernel(q_ref, k_ref, v_ref, cq_ref, ck_ref, o_ref, m_ref, acc_ref, *, bq, bk, d, scale):
    qi = pl.program_id(2)
    m_ref[...] = jnp.full(m_ref.shape, _NEG_INF, _F32)
    acc_ref[...] = jnp.zeros(acc_ref.shape, _F32)
    q = (q_ref[...].astype(_F32) * (scale * _LOG2E)).astype(_BF16)
    cq = jnp.broadcast_to(cq_ref[0, 0] * _LOG2E, (bq, _V7X_LANES))
    ones = jnp.ones((bk, _V7X_LANES), _BF16)

    def step(j, masked):
        k0 = pl.multiple_of(j * bk, bk)
        s = _qk(q, k_ref[pl.ds(k0, bk), :])
        ck = ck_ref[0, 0, pl.ds(j, 1), :] * _LOG2E
        s = jnp.concatenate([st + (cq - ct) for st, ct in zip(_lane_tiles(s, bk), _lane_tiles(ck, bk))], axis=1)
        if masked:
            s = _causal_mask(s, qi * bq, k0)
        m_new, alpha, p = _softmax_block(s, m_ref[...])
        v_ones = jnp.concatenate([v_ref[pl.ds(k0, bk), :], ones], axis=1)
        acc_ref[...] = (_lane_repeat(alpha, d + _V7X_LANES) * acc_ref[...]
                        + jnp.dot(p.astype(_BF16), v_ones, preferred_element_type=_F32))
        m_ref[...] = m_new

    _kv_sweep(qi, bq, bk, step)
    acc = acc_ref[...]
    o_ref[...] = (acc[:, :d] / _lane_repeat(acc[:, d:], d)).astype(o_ref.dtype)


def _fox_attention(p, cum, batch, seq_len, col_q, col_k, col_v, heads, d):
    n = p.shape[0]
    bq = _tile(seq_len, 512, _V7X_LANES)
    bk = _tile(bq, 512, _V7X_LANES)
    nq, nk = seq_len // bq, seq_len // bk
    cum_rows = cum.reshape(batch, heads, nk, bk)
    cum_cols = cum.reshape(batch, heads, seq_len, 1)
    vmem = (2 * 2 * seq_len * d * 2 + 4 * bq * d * 2 + 2 * seq_len * 4 + 2 * bq * _V7X_LANES * 4
            + bq * (d + 2 * _V7X_LANES) * 4 + 8 * bq * bk * 4)
    return pl.pallas_call(
        functools.partial(_fox_kernel, bq=bq, bk=bk, d=d, scale=d ** -0.5),
        grid=(batch, heads, nq),
        in_specs=[pl.BlockSpec((bq, d), lambda b, h, i: (b * nq + i, col_q // d + h)),
                  pl.BlockSpec((seq_len, d), lambda b, h, i: (b, col_k // d + h)),
                  pl.BlockSpec((seq_len, d), lambda b, h, i: (b, col_v // d + h)),
                  pl.BlockSpec((1, 1, bq, 1), lambda b, h, i: (b, h, i, 0)),
                  pl.BlockSpec((1, 1, nk, bk), lambda b, h, i: (b, h, 0, 0))],
        out_specs=pl.BlockSpec((bq, d), lambda b, h, i: (b * nq + i, h)),
        out_shape=jax.ShapeDtypeStruct((n, heads * d), _BF16),
        scratch_shapes=[pltpu.VMEM((bq, _V7X_LANES), _F32), pltpu.VMEM((bq, d + _V7X_LANES), _F32)],
        compiler_params=_params(("parallel", "parallel", "arbitrary"), vmem),
        name="fox_attention",
    )(p, p, p, cum_cols, cum_rows)


def _lane_column(x, h):
    lane = lax.broadcasted_iota(jnp.int32, x.shape, 1)
    return jnp.sum(jnp.where(lane == h, x, 0.0), axis=-1, keepdims=True)


def _bdot(a, b):
    return jnp.dot(a.astype(_BF16), b.astype(_BF16), preferred_element_type=_F32)


_INV_BASE_LOG2 = 4


def _unit_lower_inverses(mats, ri, ci):
    c = mats[0].shape[0]
    same_block = lambda log2n: (ri >> log2n) == (ci >> log2n)
    eye = (ri == ci).astype(_F32)
    base = same_block(_INV_BASE_LOG2)
    pw = [jnp.where(base, -a, 0.0) for a in mats]
    tinv = [eye + x for x in pw]
    for _ in range(_INV_BASE_LOG2 - 1):
        pw = [_bdot(x, x) for x in pw]
        tinv = [t + _bdot(t, x) for t, x in zip(tinv, pw)]
    log2n = _INV_BASE_LOG2
    while (1 << log2n) < c:
        level = same_block(log2n + 1) & jnp.logical_not(same_block(log2n))
        off = [_bdot(t, jnp.where(level, a, 0.0)) for t, a in zip(tinv, mats)]
        tinv = [t - _bdot(o, t) for t, o in zip(tinv, off)]
        log2n += 1
    return tinv


def _deltanet_kernel(q_ref, k_ref, v_ref, z_ref, gc_ref, gr_ref, cw_ref, alog_c_ref, dtb_c_ref,
                     alog_r_ref, dtb_r_ref, ng_ref, o_ref, s_ref, tail_ref, *, heads, d, c):
    t = pl.program_id(1)
    width = heads * d

    @pl.when(t == 0)
    def _():
        s_ref[...] = jnp.zeros(s_ref.shape, _F32)
        tail_ref[...] = jnp.zeros(tail_ref.shape, _F32)

    cw = cw_ref[...]
    conv = []
    for idx, ref in enumerate((q_ref, k_ref, v_ref)):
        raw = ref[...].astype(_F32)
        cols = slice(idx * width, (idx + 1) * width)
        conv.append(_silu(_causal_conv(raw, tail_ref[:, cols], cw[:, cols])))
        tail_ref[:, cols] = raw[c - _V7X_SUBLANES:]

    gates_c = gc_ref[...]
    g_cols = -jnp.exp(alog_c_ref[...]) * _softplus(gates_c + dtb_c_ref[...])
    beta_cols = 1.0 / (1.0 + jnp.exp(-gates_c))
    gates_r = gr_ref[0]
    g_rows = -jnp.exp(alog_r_ref[...]) * _softplus(gates_r[:heads] + dtb_r_ref[...])
    ri = lax.broadcasted_iota(jnp.int32, (c, c), 0)
    ci = lax.broadcasted_iota(jnp.int32, (c, c), 1)
    incl = ri >= ci
    strict = ri > ci
    hp = lax.Precision.HIGHEST
    gcum_cols = jnp.dot(incl.astype(_F32), g_cols, precision=hp, preferred_element_type=_F32)
    gcum_rows = jnp.dot(g_rows, (ri <= ci).astype(_F32), precision=hp, preferred_element_type=_F32)

    hr = range(heads)
    cols = [slice(h * d, (h + 1) * d) for h in hr]
    unit = lambda x: x * lax.rsqrt(jnp.sum(x * x, axis=-1, keepdims=True) + _EPS)
    q = [unit(conv[0][:, s]) * (d ** -0.5) for s in cols]
    k = [unit(conv[1][:, s]) for s in cols]
    v = [conv[2][:, s] for s in cols]
    gc = [_lane_column(gcum_cols, h) for h in hr]
    beta = [_lane_column(beta_cols, heads + h) for h in hr]
    decay = [jnp.exp(jnp.where(incl, gc[h] - gcum_rows[h:h + 1], _NEG_INF)) for h in hr]
    k16 = [x.astype(_BF16) for x in k]
    kb = [k[h] * beta[h] for h in hr]
    a = [jnp.where(strict, _qk(kb[h].astype(_BF16), k16[h]) * decay[h], 0.0) for h in hr]
    tinv = _unit_lower_inverses(a, ri, ci)
    egc = [jnp.exp(x) for x in gc]
    wu = [_bdot(tinv[h], jnp.concatenate([kb[h] * egc[h], v[h] * beta[h]], axis=1)) for h in hr]
    aqk = [_qk(q[h].astype(_BF16), k16[h]) * decay[h] for h in hr]
    s = [s_ref[h] for h in hr]
    ws_qs = [_bdot(jnp.concatenate([wu[h][:, :d], q[h] * egc[h]], axis=0), s[h]) for h in hr]
    v_new = [wu[h][:, d:] - ws_qs[h][:c] for h in hr]
    o = [ws_qs[h][c:] + _bdot(aqk[h], v_new[h]) for h in hr]
    g_last = [x[c - 1:c] for x in gc]
    k_dec = [k[h] * jnp.exp(g_last[h] - gc[h]) for h in hr]
    for h in hr:
        s_ref[h] = s[h] * jnp.exp(g_last[h]) + lax.dot_general(
            k_dec[h].astype(_BF16), v_new[h].astype(_BF16), (((0,), (0,)), ((), ())), preferred_element_type=_F32)
    ng = ng_ref[...]
    for h in hr:
        on = o[h] * lax.rsqrt(jnp.mean(o[h] * o[h], axis=-1, keepdims=True) + _EPS) * ng
        o_ref[:, cols[h]] = (on * _silu(z_ref[:, cols[h]].astype(_F32))).astype(o_ref.dtype)


def _gated_deltanet(p, gates_cols, gates_rows, conv_w, a_log, dt_bias, norm_g, batch, seq_len,
                    col_q, col_k, col_v, col_z, heads, d):
    n = p.shape[0]
    c = _tile(seq_len, _DN_CHUNK, _BF16_ROWS)
    nt = seq_len // c
    width = heads * d
    lanes = gates_cols.shape[1]
    kw = conv_w.shape[0]
    col = lambda off: pl.BlockSpec((c, width), lambda b, t: (b * nt + t, off // width))
    const = lambda shape: pl.BlockSpec(shape, lambda b, t: (0,) * len(shape))
    pad_lane = lambda v: jnp.zeros((1, lanes), _F32).at[0, :heads].set(v.astype(_F32))
    vmem = 2 * 5 * c * width * 2 + 12 * c * width * 4 + heads * d * d * 4 + 2 * kw * 3 * width * 4
    return pl.pallas_call(
        functools.partial(_deltanet_kernel, heads=heads, d=d, c=c),
        grid=(batch, nt),
        in_specs=[col(col_q), col(col_k), col(col_v), col(col_z),
                  pl.BlockSpec((c, lanes), lambda b, t: (b * nt + t, 0)),
                  pl.BlockSpec((1, gates_rows.shape[1], c), lambda b, t: (b, 0, t)),
                  const((kw, 3 * width)), const((1, lanes)), const((1, lanes)),
                  const((heads, 1)), const((heads, 1)), const((1, d))],
        out_specs=pl.BlockSpec((c, width), lambda b, t: (b * nt + t, 0)),
        out_shape=jax.ShapeDtypeStruct((n, width), _BF16),
        scratch_shapes=[pltpu.VMEM((heads, d, d), _F32), pltpu.VMEM((_V7X_SUBLANES, 3 * width), _F32)],
        compiler_params=_params(("parallel", "arbitrary"), vmem),
        name="gated_deltanet",
    )(p, p, p, p, gates_cols, gates_rows, conv_w.astype(_F32), pad_lane(a_log), pad_lane(dt_bias),
      a_log.reshape(heads, 1).astype(_F32), dt_bias.reshape(heads, 1).astype(_F32),
      norm_g.reshape(1, d).astype(_F32))


def _ffn_in_kernel(h_ref, halo_ref, wg_ref, wu_ref, cw_ref, o_ref, *, blocks_per_seq):
    i = pl.program_id(0)
    h = h_ref[...]
    g = jnp.dot(h, wg_ref[...], preferred_element_type=_F32)
    u = jnp.dot(h, wu_ref[...], preferred_element_type=_F32)
    gh = jnp.dot(halo_ref[...], wg_ref[...], preferred_element_type=_F32)
    prev8 = jnp.where(i % blocks_per_seq == 0, 0.0, gh[_BF16_ROWS - _V7X_SUBLANES:])
    a = _causal_conv(g, prev8, cw_ref[...])
    o_ref[...] = (_silu(a) * u).astype(o_ref.dtype)


def _ffn_in(h, w_gate, w_up, conv_w, seq_len, tm_target=1024, tn_target=512):
    n, d = h.shape
    ff = w_gate.shape[1]
    tm = _tile(seq_len, tm_target, _BF16_ROWS)
    tn = _tile(ff, tn_target, _V7X_LANES)
    hb = tm // _BF16_ROWS
    kw = conv_w.shape[0]
    vmem = 2 * 2 * d * (tm + _BF16_ROWS + 2 * tn) + 2 * tm * tn * 2 + 6 * tm * tn * 4
    return pl.pallas_call(
        functools.partial(_ffn_in_kernel, blocks_per_seq=seq_len // tm),
        grid=(n // tm, ff // tn),
        in_specs=[pl.BlockSpec((tm, d), lambda i, j: (i, 0)),
                  pl.BlockSpec((_BF16_ROWS, d), lambda i, j: (jnp.maximum(i * hb - 1, 0), 0)),
                  pl.BlockSpec((d, tn), lambda i, j: (0, j)),
                  pl.BlockSpec((d, tn), lambda i, j: (0, j)),
                  pl.BlockSpec((kw, tn), lambda i, j: (0, j))],
        out_specs=pl.BlockSpec((tm, tn), lambda i, j: (i, j)),
        out_shape=jax.ShapeDtypeStruct((n, ff), _BF16),
        compiler_params=_params(("parallel", "parallel"), vmem),
        name="ffn_in",
    )(h, h, w_gate, w_up, conv_w)


def _pad_cols(w, total):
    return jnp.pad(w, ((0, 0), (0, total - w.shape[1])))


def _layer(x, layer, batch, seq_len, attn_norm, w_in, sc_conv, lam_vecs, diff_norm, dn_conv, dn_a_log,
           dn_dt_bias, dn_norm, fox_bias, w_out, ffn_norm, w_gate, w_up, ffn_conv, w_down):
    d = lam_vecs[0].shape[-1]
    gw = w_out.shape[0] // 4
    dn_heads, fox_heads = dn_a_log.shape[-1], fox_bias.shape[-1]
    diff_heads = gw // diff_norm.shape[-1]
    o_dn_a = 10 * gw
    o_dn_b = o_dn_a + dn_heads
    o_fx = o_dn_b + dn_heads
    o_fx_f = o_fx + 3 * gw
    w_wide = jnp.concatenate([w_in[:, :o_dn_a], w_in[:, o_fx:o_fx_f]], axis=1).astype(_BF16)
    w_gates = _pad_cols(jnp.concatenate([w_in[:, o_dn_a:o_fx], w_in[:, o_fx_f:]], axis=1), _V7X_LANES).astype(_BF16)
    col = lambda idx: idx * gw

    h = _rmsnorm(x, attn_norm, _BF16)
    p = _matmul([h], [w_wide], _BF16, name="in_proj")
    gates = _matmul([h], [w_gates], _F32, tn_target=_V7X_LANES, name="in_proj_gates")

    y_sc = _short_conv(p, sc_conv, seq_len, 0, 1, 2, gw)
    lam_init = 0.8 - 0.6 * math.exp(-0.3 * layer)
    y_df = _diff_attention(p, lam_vecs, diff_norm, lam_init, batch, seq_len, col(3), col(4), col(5), diff_heads, d)

    gates_t = gates.reshape(batch, seq_len, _V7X_LANES).transpose(0, 2, 1)
    fox_rows = gates_t[:, 2 * dn_heads:2 * dn_heads + fox_heads].reshape(batch * fox_heads, seq_len)
    fox_bias_rows = jnp.tile(fox_bias.astype(_F32), batch).reshape(batch * fox_heads, 1)
    cum = _fox_cumsum(fox_rows, fox_bias_rows)
    y_fx = _fox_attention(p, cum, batch, seq_len, col(10), col(11), col(12), fox_heads, d)

    y_dn = _gated_deltanet(p, gates, gates_t[:, :2 * dn_heads], dn_conv, dn_a_log, dn_dt_bias, dn_norm,
                           batch, seq_len, col(6), col(7), col(8), col(9), dn_heads, d)

    w_out_b = w_out.astype(_BF16)
    x = _matmul([y_sc, y_df, y_dn, y_fx], [w_out_b] * 4, _F32, residual=x, tn_target=512, name="out_proj")

    ff = w_gate.shape[1]
    ffp = -(-ff // _FF_ALIGN) * _FF_ALIGN if ff > _FF_ALIGN else ff
    h2 = _rmsnorm(x, ffn_norm, _BF16)
    act = _ffn_in(h2, _pad_cols(w_gate, ffp).astype(_BF16), _pad_cols(w_up, ffp).astype(_BF16),
                  _pad_cols(ffn_conv.astype(_F32), ffp), seq_len)
    w_down_b = jnp.pad(w_down, ((0, ffp - ff), (0, 0))).astype(_BF16)
    return _matmul_acc(act, w_down_b, x, name="ffn_out")


def kernel(x, attn_norm, w_in, sc_conv, lam_q1, lam_k1, lam_q2, lam_k2, diff_norm, dn_conv, dn_a_log,
           dn_dt_bias, dn_norm, fox_bias, w_out, ffn_norm, w_gate, w_up, ffn_conv, w_down, final_norm):
    batch, seq_len, d_model = x.shape
    xf = x.reshape(batch * seq_len, d_model).astype(_F32)
    for l in range(w_in.shape[0]):
        xf = _layer(xf, l, batch, seq_len, attn_norm[l], w_in[l], sc_conv[l],
                    (lam_q1[l], lam_k1[l], lam_q2[l], lam_k2[l]), diff_norm[l], dn_conv[l], dn_a_log[l],
                    dn_dt_bias[l], dn_norm[l], fox_bias[l], w_out[l], ffn_norm[l], w_gate[l], w_up[l],
                    ffn_conv[l], w_down[l])
    out = _rmsnorm(xf, final_norm, x.dtype)
    return out.reshape(batch, seq_len, d_model)
```

```python
import functools
import math

import jax
import jax.numpy as jnp
from jax import lax
from jax.experimental import pallas as pl
from jax.experimental.pallas import tpu as pltpu

_F32 = jnp.float32
_BF16 = jnp.bfloat16
_EPS = 1e-6
_NEG_INF = float("-inf")

_V7X_VMEM_BYTES = 64 * 1024 * 1024
_V7X_LANES = 128
_V7X_SUBLANES = 8
_BF16_ROWS = 16
_DN_CHUNK = 128
_FF_ALIGN = 1024


def _tile(n, target, align):
    t = min(target, n)
    t -= t % align
    while t >= align:
        if n % t == 0:
            return t
        t -= align
    return n


def _params(semantics, vmem_bytes):
    limit = min(int(vmem_bytes) + (8 << 20), _V7X_VMEM_BYTES - (4 << 20))
    return pltpu.CompilerParams(dimension_semantics=semantics, vmem_limit_bytes=limit)


def _silu(x):
    return x * (1.0 / (1.0 + jnp.exp(-x)))


def _softplus(x):
    return jnp.maximum(x, 0.0) + jnp.log1p(jnp.exp(-jnp.abs(x)))


def _shift_rows(x, prev8, s):
    xs = pltpu.roll(x, s, 0)
    ps = pltpu.roll(prev8, s, 0)
    row = lax.broadcasted_iota(jnp.int32, prev8.shape, 0)
    head = jnp.where(row < s, ps, xs[:_V7X_SUBLANES])
    return jnp.concatenate([head, xs[_V7X_SUBLANES:]], axis=0)


def _causal_conv(x, prev8, w):
    k = w.shape[0]
    out = x * w[k - 1:k]
    for j in range(k - 1):
        out = out + _shift_rows(x, prev8, k - 1 - j) * w[j:j + 1]
    return out


def _rmsnorm_kernel(x_ref, g_ref, o_ref):
    x = x_ref[...]
    y = x * lax.rsqrt(jnp.mean(x * x, axis=-1, keepdims=True) + _EPS)
    o_ref[...] = (y * g_ref[...]).astype(o_ref.dtype)


def _rmsnorm(x, g, out_dtype):
    n, d = x.shape
    tm = _tile(n, 256, _BF16_ROWS)
    vmem = 2 * tm * d * (4 + jnp.dtype(out_dtype).itemsize) + 2 * d * 4
    return pl.pallas_call(
        _rmsnorm_kernel,
        grid=(n // tm,),
        in_specs=[pl.BlockSpec((tm, d), lambda i: (i, 0)), pl.BlockSpec((1, d), lambda i: (0, 0))],
        out_specs=pl.BlockSpec((tm, d), lambda i: (i, 0)),
        out_shape=jax.ShapeDtypeStruct((n, d), out_dtype),
        compiler_params=_params(("parallel",), vmem),
        name="rmsnorm",
    )(x, g.reshape(1, d).astype(_F32))


def _mm_kernel(*refs, n_pairs, has_res):
    a_refs, w_refs = refs[:n_pairs], refs[n_pairs:2 * n_pairs]
    o_ref = refs[-1]
    acc = jnp.dot(a_refs[0][...], w_refs[0][...], preferred_element_type=_F32)
    for a_ref, w_ref in zip(a_refs[1:], w_refs[1:]):
        acc = acc + jnp.dot(a_ref[...], w_ref[...], preferred_element_type=_F32)
    if has_res:
        acc = acc + refs[2 * n_pairs][...]
    o_ref[...] = acc.astype(o_ref.dtype)


def _matmul(a_list, w, layer, out_dtype, residual=None, tm_target=1024, tn_target=1024, name="matmul"):
    m, n = a_list[0].shape[0], w.shape[2]
    kb = a_list[0].shape[1]
    assert all(a.shape[1] == kb for a in a_list) and w.shape[1] == kb * len(a_list)
    tm = _tile(m, tm_target, _BF16_ROWS)
    tn = _tile(n, tn_target, _V7X_LANES)
    ktot = w.shape[1]
    in_specs = [pl.BlockSpec((tm, kb), lambda i, j: (i, 0)) for _ in a_list]
    in_specs += [pl.BlockSpec((None, kb, tn), functools.partial(lambda i, j, r: (layer, r, j), r=r))
                 for r in range(len(a_list))]
    args = list(a_list) + [w] * len(a_list)
    vmem = 2 * 2 * ktot * (tm + tn) + 2 * tm * tn * jnp.dtype(out_dtype).itemsize + tm * tn * 4
    if residual is not None:
        in_specs.append(pl.BlockSpec((tm, tn), lambda i, j: (i, j)))
        args.append(residual)
        vmem += 2 * tm * tn * 4
    return pl.pallas_call(
        functools.partial(_mm_kernel, n_pairs=len(a_list), has_res=residual is not None),
        grid=(m // tm, n // tn),
        in_specs=in_specs,
        out_specs=pl.BlockSpec((tm, tn), lambda i, j: (i, j)),
        out_shape=jax.ShapeDtypeStruct((m, n), out_dtype),
        compiler_params=_params(("parallel", "parallel"), vmem),
        name=name,
    )(*args)


def _mm_acc_kernel(a_ref, w_ref, r_ref, o_ref, acc_ref, *, nk):
    k = pl.program_id(2)
    prod = lambda: jnp.dot(a_ref[...], w_ref[...], preferred_element_type=_F32)
    if nk == 1:
        o_ref[...] = r_ref[...] + prod()
        return

    @pl.when(k == 0)
    def _():
        acc_ref[...] = r_ref[...] + prod()

    @pl.when(jnp.logical_and(k > 0, k < nk - 1))
    def _():
        acc_ref[...] += prod()

    @pl.when(k == nk - 1)
    def _():
        o_ref[...] = acc_ref[...] + prod()


def _matmul_acc(a, w, layer, residual, tm_target=1024, tn_target=1024, tk_target=2816, name="matmul_acc"):
    m, kdim = a.shape
    n = w.shape[2]
    tm = _tile(m, tm_target, _BF16_ROWS)
    tn = _tile(n, tn_target, _V7X_LANES)
    tk = _tile(kdim, tk_target, _V7X_LANES)
    vmem = 2 * 2 * tk * (tm + tn) + 5 * tm * tn * 4
    return pl.pallas_call(
        functools.partial(_mm_acc_kernel, nk=kdim // tk),
        grid=(m // tm, n // tn, kdim // tk),
        in_specs=[pl.BlockSpec((tm, tk), lambda i, j, k: (i, k)),
                  pl.BlockSpec((None, tk, tn), lambda i, j, k: (layer, k, j)),
                  pl.BlockSpec((tm, tn), lambda i, j, k: (i, j))],
        out_specs=pl.BlockSpec((tm, tn), lambda i, j, k: (i, j)),
        out_shape=jax.ShapeDtypeStruct((m, n), _F32),
        scratch_shapes=[pltpu.VMEM((tm, tn), _F32)],
        compiler_params=_params(("parallel", "parallel", "arbitrary"), vmem),
        name=name,
    )(a, w, residual)


def _sconv_kernel(h_ref, c_ref, b_ref, hh_ref, ch_ref, w_ref, o_ref, *, blocks_per_seq):
    i = pl.program_id(0)
    x = c_ref[...].astype(_F32) * h_ref[...].astype(_F32)
    halo = ch_ref[...].astype(_F32) * hh_ref[...].astype(_F32)
    prev8 = jnp.where(i % blocks_per_seq == 0, 0.0, halo[_BF16_ROWS - _V7X_SUBLANES:])
    y = b_ref[...].astype(_F32) * _causal_conv(x, prev8, w_ref[...])
    o_ref[...] = y.astype(o_ref.dtype)


def _short_conv(p, conv_w, seq_len, col_h, col_c, col_b, width):
    n = p.shape[0]
    tm = _tile(seq_len, 512, _BF16_ROWS)
    hb = tm // _BF16_ROWS
    cur = lambda c: pl.BlockSpec((tm, width), lambda i: (i, c))
    halo = lambda c: pl.BlockSpec((_BF16_ROWS, width), lambda i: (jnp.maximum(i * hb - 1, 0), c))
    k = conv_w.shape[0]
    vmem = 2 * (4 * tm + 2 * _BF16_ROWS) * width * 2 + 8 * tm * width * 4
    return pl.pallas_call(
        functools.partial(_sconv_kernel, blocks_per_seq=seq_len // tm),
        grid=(n // tm,),
        in_specs=[cur(col_h), cur(col_c), cur(col_b), halo(col_h), halo(col_c),
                  pl.BlockSpec((k, width), lambda i: (0, 0))],
        out_specs=pl.BlockSpec((tm, width), lambda i: (i, 0)),
        out_shape=jax.ShapeDtypeStruct((n, width), _BF16),
        compiler_params=_params(("parallel",), vmem),
        name="short_conv",
    )(p, p, p, p, p, conv_w.astype(_F32))


_LOG2E = math.log2(math.e)


def _lane_tiles(x, width):
    return [x[:, c:c + _V7X_LANES] for c in range(0, width, _V7X_LANES)]


def _lane_repeat(x, width):
    return jnp.concatenate([x] * (width // _V7X_LANES), axis=1)


def _softmax_block(s, m_prev):
    m_new = jnp.maximum(m_prev, jnp.max(s, axis=-1, keepdims=True))
    alpha = jnp.exp2(m_prev - m_new)
    p = jnp.concatenate([jnp.exp2(t - m_new) for t in _lane_tiles(s, s.shape[1])], axis=1)
    return m_new, alpha, p


def _causal_mask(s, q0, k0):
    row = q0 + lax.broadcasted_iota(jnp.int32, s.shape, 0)
    col = k0 + lax.broadcasted_iota(jnp.int32, s.shape, 1)
    return jnp.where(col <= row, s, _NEG_INF)


def _qk(q, k):
    return lax.dot_general(q, k, (((1,), (1,)), ((), ())), preferred_element_type=_F32)


def _flash_blocks(seq_len):
    bq = _tile(seq_len, 1024, _V7X_LANES)
    bk = bq // 2 if bq % (2 * _V7X_LANES) == 0 else bq
    return bq, bk


def _kv_sweep(qi, r, step):
    def trip(i, carry):
        for d in range(r):
            step(i * r + d, False)
        return carry

    lax.fori_loop(0, qi, trip, 0)
    for d in range(r):
        step(qi * r + d, True)


def _diff_kernel(q_ref, k_ref, v_ref, lq1_ref, lk1_ref, lq2_ref, lk2_ref, g_ref, o_ref,
                 m_ref, l_ref, acc_ref, *, bq, bk, d, scale, lam_init):
    qi = pl.program_id(2)
    hw = 2 * d
    for mi in range(2):
        m_ref[mi] = jnp.full(m_ref.shape[1:], _NEG_INF, _F32)
        l_ref[mi] = jnp.zeros(l_ref.shape[1:], _F32)
        acc_ref[mi] = jnp.zeros(acc_ref.shape[1:], _F32)
    q = (q_ref[...].astype(_F32) * (scale * _LOG2E)).astype(_BF16)

    def step(j, masked):
        k0 = pl.multiple_of(j * bk, bk)
        k = k_ref[pl.ds(k0, bk), :]
        v = v_ref[pl.ds(k0, bk), :]
        for mi in range(2):
            s = _qk(q[:, mi * d:(mi + 1) * d], k[:, mi * d:(mi + 1) * d])
            if masked:
                s = _causal_mask(s, qi * bq, k0)
            m_new, alpha, p = _softmax_block(s, m_ref[mi])
            l_ref[mi] = alpha * l_ref[mi] + jnp.sum(p, axis=-1, keepdims=True)
            acc_ref[mi] = (_lane_repeat(alpha, hw) * acc_ref[mi]
                           + jnp.dot(p.astype(_BF16), v, preferred_element_type=_F32))
            m_ref[mi] = m_new

    _kv_sweep(qi, bq // bk, step)
    lam = (jnp.exp(jnp.sum(lq1_ref[...] * lk1_ref[...], axis=-1, keepdims=True))
           - jnp.exp(jnp.sum(lq2_ref[...] * lk2_ref[...], axis=-1, keepdims=True)) + lam_init)
    o = acc_ref[0] / _lane_repeat(l_ref[0], hw) - lam * (acc_ref[1] / _lane_repeat(l_ref[1], hw))
    o = o * lax.rsqrt(jnp.mean(o * o, axis=-1, keepdims=True) + _EPS) * g_ref[...]
    o_ref[...] = (o * (1.0 - lam_init)).astype(o_ref.dtype)


def _diff_attention(p, lam_vecs, norm_g, lam_init, batch, seq_len, col_q, col_k, col_v, heads, d):
    n = p.shape[0]
    hw = 2 * d
    bq, bk = _flash_blocks(seq_len)
    nq = seq_len // bq
    vec = pl.BlockSpec((1, d), lambda b, h, i: (0, 0))
    vmem = 2 * 2 * seq_len * hw * 2 + 4 * bq * hw * 2 + 2 * bq * (hw + 2 * _V7X_LANES) * 4 + 12 * bq * bk * 4
    return pl.pallas_call(
        functools.partial(_diff_kernel, bq=bq, bk=bk, d=d, scale=d ** -0.5, lam_init=lam_init),
        grid=(batch, heads, nq),
        in_specs=[pl.BlockSpec((bq, hw), lambda b, h, i: (b * nq + i, col_q // hw + h)),
                  pl.BlockSpec((seq_len, hw), lambda b, h, i: (b, col_k // hw + h)),
                  pl.BlockSpec((seq_len, hw), lambda b, h, i: (b, col_v // hw + h)),
                  vec, vec, vec, vec,
                  pl.BlockSpec((1, hw), lambda b, h, i: (0, 0))],
        out_specs=pl.BlockSpec((bq, hw), lambda b, h, i: (b * nq + i, h)),
        out_shape=jax.ShapeDtypeStruct((n, heads * hw), _BF16),
        scratch_shapes=[pltpu.VMEM((2, bq, _V7X_LANES), _F32), pltpu.VMEM((2, bq, _V7X_LANES), _F32),
                        pltpu.VMEM((2, bq, hw), _F32)],
        compiler_params=_params(("parallel", "parallel", "arbitrary"), vmem),
        name="diff_attention",
    )(p, p, p, *[v.reshape(1, d).astype(_F32) for v in lam_vecs], norm_g.reshape(1, hw).astype(_F32))


def _fox_cum_kernel(f_ref, bias_ref, o_ref):
    x = f_ref[...] + bias_ref[...]
    ls = jnp.minimum(x, 0.0) - jnp.log1p(jnp.exp(-jnp.abs(x)))
    r = lax.broadcasted_iota(jnp.int32, (_V7X_LANES, _V7X_LANES), 0)
    c = lax.broadcasted_iota(jnp.int32, (_V7X_LANES, _V7X_LANES), 1)
    upper = (r <= c).astype(_F32)
    carry = jnp.zeros((x.shape[0], 1), _F32)
    for t in range(x.shape[1] // _V7X_LANES):
        sl = slice(t * _V7X_LANES, (t + 1) * _V7X_LANES)
        w = jnp.dot(ls[:, sl], upper, precision=lax.Precision.HIGHEST, preferred_element_type=_F32) + carry
        o_ref[:, sl] = w
        carry = w[:, _V7X_LANES - 1:]


def _fox_cumsum(f_rows, bias_rows):
    return pl.pallas_call(
        _fox_cum_kernel,
        out_shape=jax.ShapeDtypeStruct(f_rows.shape, _F32),
        name="fox_cumsum",
    )(f_rows, bias_rows)


def _fox_kernel(q_ref, k_ref, v_ref, cq_ref, ck_ref, o_ref, m_ref, acc_ref, *, bq, bk, d, scale):
    qi = pl.program_id(2)
    m_ref[...] = jnp.full(m_ref.shape, _NEG_INF, _F32)
    acc_ref[...] = jnp.zeros(acc_ref.shape, _F32)
    q = (q_ref[...].astype(_F32) * (scale * _LOG2E)).astype(_BF16)
    cq = jnp.broadcast_to(cq_ref[0, 0] * _LOG2E, (bq, _V7X_LANES))
    ones = jnp.ones((bk, _V7X_LANES), _BF16)

    def step(j, masked):
        k0 = pl.multiple_of(j * bk, bk)
        s = _qk(q, k_ref[pl.ds(k0, bk), :])
        ck = ck_ref[0, 0, pl.ds(j, 1), :] * _LOG2E
        s = jnp.concatenate([st + (cq - ct) for st, ct in zip(_lane_tiles(s, bk), _lane_tiles(ck, bk))], axis=1)
        if masked:
            s = _causal_mask(s, qi * bq, k0)
        m_new, alpha, p = _softmax_block(s, m_ref[...])
        v_ones = jnp.concatenate([v_ref[pl.ds(k0, bk), :], ones], axis=1)
        acc_ref[...] = (_lane_repeat(alpha, d + _V7X_LANES) * acc_ref[...]
                        + jnp.dot(p.astype(_BF16), v_ones, preferred_element_type=_F32))
        m_ref[...] = m_new

    _kv_sweep(qi, bq // bk, step)
    acc = acc_ref[...]
    o_ref[...] = (acc[:, :d] / _lane_repeat(acc[:, d:], d)).astype(o_ref.dtype)


def _fox_attention(p, cum, batch, seq_len, col_q, col_k, col_v, heads, d):
    n = p.shape[0]
    bq, bk = _flash_blocks(seq_len)
    nq, nk = seq_len // bq, seq_len // bk
    cum_rows = cum.reshape(batch, heads, nk, bk)
    cum_cols = cum.reshape(batch, heads, seq_len, 1)
    vmem = (2 * 2 * seq_len * d * 2 + 4 * bq * d * 2 + 2 * seq_len * 4 + 2 * bq * _V7X_LANES * 4
            + bq * (d + 2 * _V7X_LANES) * 4 + 10 * bq * bk * 4)
    return pl.pallas_call(
        functools.partial(_fox_kernel, bq=bq, bk=bk, d=d, scale=d ** -0.5),
        grid=(batch, heads, nq),
        in_specs=[pl.BlockSpec((bq, d), lambda b, h, i: (b * nq + i, col_q // d + h)),
                  pl.BlockSpec((seq_len, d), lambda b, h, i: (b, col_k // d + h)),
                  pl.BlockSpec((seq_len, d), lambda b, h, i: (b, col_v // d + h)),
                  pl.BlockSpec((1, 1, bq, 1), lambda b, h, i: (b, h, i, 0)),
                  pl.BlockSpec((1, 1, nk, bk), lambda b, h, i: (b, h, 0, 0))],
        out_specs=pl.BlockSpec((bq, d), lambda b, h, i: (b * nq + i, h)),
        out_shape=jax.ShapeDtypeStruct((n, heads * d), _BF16),
        scratch_shapes=[pltpu.VMEM((bq, _V7X_LANES), _F32), pltpu.VMEM((bq, d + _V7X_LANES), _F32)],
        compiler_params=_params(("parallel", "parallel", "arbitrary"), vmem),
        name="fox_attention",
    )(p, p, p, cum_cols, cum_rows)


def _lane_column(x, h):
    lane = lax.broadcasted_iota(jnp.int32, x.shape, 1)
    return jnp.sum(jnp.where(lane == h, x, 0.0), axis=-1, keepdims=True)


def _bdot(a, b):
    return jnp.dot(a.astype(_BF16), b.astype(_BF16), preferred_element_type=_F32)


_INV_BASE_LOG2 = 4


def _unit_lower_inverses(mats, ri, ci):
    c = mats[0].shape[0]
    same_block = lambda log2n: (ri >> log2n) == (ci >> log2n)
    eye = (ri == ci).astype(_F32)
    base = same_block(_INV_BASE_LOG2)
    pw = [jnp.where(base, -a, 0.0) for a in mats]
    tinv = [eye + x for x in pw]
    for _ in range(_INV_BASE_LOG2 - 1):
        pw = [_bdot(x, x) for x in pw]
        tinv = [t + _bdot(t, x) for t, x in zip(tinv, pw)]
    log2n = _INV_BASE_LOG2
    while (1 << log2n) < c:
        level = same_block(log2n + 1) & jnp.logical_not(same_block(log2n))
        off = [_bdot(t, jnp.where(level, a, 0.0)) for t, a in zip(tinv, mats)]
        tinv = [t - _bdot(o, t) for t, o in zip(tinv, off)]
        log2n += 1
    return tinv


def _deltanet_kernel(q_ref, k_ref, v_ref, z_ref, gc_ref, gr_ref, cw_ref, alog_c_ref, dtb_c_ref,
                     alog_r_ref, dtb_r_ref, ng_ref, o_ref, s_ref, tail_ref, *, heads, d, c):
    t = pl.program_id(1)
    width = heads * d

    @pl.when(t == 0)
    def _():
        s_ref[...] = jnp.zeros(s_ref.shape, _F32)
        tail_ref[...] = jnp.zeros(tail_ref.shape, _F32)

    cw = cw_ref[...]
    conv = []
    for idx, ref in enumerate((q_ref, k_ref, v_ref)):
        raw = ref[...].astype(_F32)
        cols = slice(idx * width, (idx + 1) * width)
        conv.append(_silu(_causal_conv(raw, tail_ref[:, cols], cw[:, cols])))
        tail_ref[:, cols] = raw[c - _V7X_SUBLANES:]

    gates_c = gc_ref[...]
    g_cols = -jnp.exp(alog_c_ref[...]) * _softplus(gates_c + dtb_c_ref[...])
    beta_cols = 1.0 / (1.0 + jnp.exp(-gates_c))
    gates_r = gr_ref[0]
    g_rows = -jnp.exp(alog_r_ref[...]) * _softplus(gates_r[:heads] + dtb_r_ref[...])
    ri = lax.broadcasted_iota(jnp.int32, (c, c), 0)
    ci = lax.broadcasted_iota(jnp.int32, (c, c), 1)
    incl = ri >= ci
    strict = ri > ci
    hp = lax.Precision.HIGHEST
    gcum_cols = jnp.dot(incl.astype(_F32), g_cols, precision=hp, preferred_element_type=_F32)
    gcum_rows = jnp.dot(g_rows, (ri <= ci).astype(_F32), precision=hp, preferred_element_type=_F32)

    hr = range(heads)
    cols = [slice(h * d, (h + 1) * d) for h in hr]
    unit = lambda x: x * lax.rsqrt(jnp.sum(x * x, axis=-1, keepdims=True) + _EPS)
    q = [unit(conv[0][:, s]) * (d ** -0.5) for s in cols]
    k = [unit(conv[1][:, s]) for s in cols]
    v = [conv[2][:, s] for s in cols]
    gc = [_lane_column(gcum_cols, h) for h in hr]
    beta = [_lane_column(beta_cols, heads + h) for h in hr]
    decay = [jnp.exp(jnp.where(incl, gc[h] - gcum_rows[h:h + 1], _NEG_INF)) for h in hr]
    k16 = [x.astype(_BF16) for x in k]
    kb = [k[h] * beta[h] for h in hr]
    a = [jnp.where(strict, _qk(kb[h].astype(_BF16), k16[h]) * decay[h], 0.0) for h in hr]
    tinv = _unit_lower_inverses(a, ri, ci)
    egc = [jnp.exp(x) for x in gc]
    wu = [_bdot(tinv[h], jnp.concatenate([kb[h] * egc[h], v[h] * beta[h]], axis=1)) for h in hr]
    aqk = [_qk(q[h].astype(_BF16), k16[h]) * decay[h] for h in hr]
    s = [s_ref[h] for h in hr]
    ws_qs = [_bdot(jnp.concatenate([wu[h][:, :d], q[h] * egc[h]], axis=0), s[h]) for h in hr]
    v_new = [wu[h][:, d:] - ws_qs[h][:c] for h in hr]
    o = [ws_qs[h][c:] + _bdot(aqk[h], v_new[h]) for h in hr]
    g_last = [x[c - 1:c] for x in gc]
    k_dec = [k[h] * jnp.exp(g_last[h] - gc[h]) for h in hr]
    for h in hr:
        s_ref[h] = s[h] * jnp.exp(g_last[h]) + lax.dot_general(
            k_dec[h].astype(_BF16), v_new[h].astype(_BF16), (((0,), (0,)), ((), ())), preferred_element_type=_F32)
    ng = ng_ref[...]
    for h in hr:
        on = o[h] * lax.rsqrt(jnp.mean(o[h] * o[h], axis=-1, keepdims=True) + _EPS) * ng
        o_ref[:, cols[h]] = (on * _silu(z_ref[:, cols[h]].astype(_F32))).astype(o_ref.dtype)


def _gated_deltanet(p, gates_cols, gates_rows, conv_w, a_log, dt_bias, norm_g, batch, seq_len,
                    col_q, col_k, col_v, col_z, heads, d):
    n = p.shape[0]
    c = _tile(seq_len, _DN_CHUNK, _BF16_ROWS)
    nt = seq_len // c
    width = heads * d
    lanes = gates_cols.shape[1]
    kw = conv_w.shape[0]
    col = lambda off: pl.BlockSpec((c, width), lambda b, t: (b * nt + t, off // width))
    const = lambda shape: pl.BlockSpec(shape, lambda b, t: (0,) * len(shape))
    pad_lane = lambda v: jnp.zeros((1, lanes), _F32).at[0, :heads].set(v.astype(_F32))
    vmem = 2 * 5 * c * width * 2 + 12 * c * width * 4 + heads * d * d * 4 + 2 * kw * 3 * width * 4
    return pl.pallas_call(
        functools.partial(_deltanet_kernel, heads=heads, d=d, c=c),
        grid=(batch, nt),
        in_specs=[col(col_q), col(col_k), col(col_v), col(col_z),
                  pl.BlockSpec((c, lanes), lambda b, t: (b * nt + t, 0)),
                  pl.BlockSpec((1, gates_rows.shape[1], c), lambda b, t: (b, 0, t)),
                  const((kw, 3 * width)), const((1, lanes)), const((1, lanes)),
                  const((heads, 1)), const((heads, 1)), const((1, d))],
        out_specs=pl.BlockSpec((c, width), lambda b, t: (b * nt + t, 0)),
        out_shape=jax.ShapeDtypeStruct((n, width), _BF16),
        scratch_shapes=[pltpu.VMEM((heads, d, d), _F32), pltpu.VMEM((_V7X_SUBLANES, 3 * width), _F32)],
        compiler_params=_params(("parallel", "arbitrary"), vmem),
        name="gated_deltanet",
    )(p, p, p, p, gates_cols, gates_rows, conv_w.astype(_F32), pad_lane(a_log), pad_lane(dt_bias),
      a_log.reshape(heads, 1).astype(_F32), dt_bias.reshape(heads, 1).astype(_F32),
      norm_g.reshape(1, d).astype(_F32))


def _ffn_in_kernel(h_ref, wg_ref, wu_ref, cw_ref, o_ref, tail_ref, *, blocks_per_seq, parts):
    i, j = pl.program_id(0), pl.program_id(1)
    tm, tn = o_ref.shape

    @pl.when(i % blocks_per_seq == 0)
    def _():
        tail_ref[j] = jnp.zeros(tail_ref.shape[1:], _F32)

    h = h_ref[...]
    for c in range(parts):
        cols = slice(c * (tn // parts), (c + 1) * (tn // parts))
        g = jnp.dot(h, wg_ref[:, cols], preferred_element_type=_F32)
        u = jnp.dot(h, wu_ref[:, cols], preferred_element_type=_F32)
        a = _causal_conv(g, tail_ref[j, :, cols], cw_ref[:, cols])
        tail_ref[j, :, cols] = g[tm - _V7X_SUBLANES:]
        o_ref[:, cols] = (_silu(a) * u).astype(o_ref.dtype)


def _ffn_in(h, w_gate, w_up, conv_w, layer, seq_len, tm_target=1024, tn_target=512):
    n, d = h.shape
    ff = w_gate.shape[2]
    tm = _tile(seq_len, tm_target, _BF16_ROWS)
    tn = _tile(ff, tn_target, _V7X_LANES)
    parts = 2 if tn % (2 * _V7X_LANES) == 0 else 1
    kw = conv_w.shape[1]
    wspec = pl.BlockSpec((None, d, tn), lambda i, j: (layer, 0, j))
    vmem = 2 * 2 * d * (tm + 2 * tn) + 2 * tm * tn * 2 + 6 * tm * tn * 4 + ff * _V7X_SUBLANES * 4
    return pl.pallas_call(
        functools.partial(_ffn_in_kernel, blocks_per_seq=seq_len // tm, parts=parts),
        grid=(n // tm, ff // tn),
        in_specs=[pl.BlockSpec((tm, d), lambda i, j: (i, 0)), wspec, wspec,
                  pl.BlockSpec((None, kw, tn), lambda i, j: (layer, 0, j))],
        out_specs=pl.BlockSpec((tm, tn), lambda i, j: (i, j)),
        out_shape=jax.ShapeDtypeStruct((n, ff), _BF16),
        scratch_shapes=[pltpu.VMEM((ff // tn, _V7X_SUBLANES, tn), _F32)],
        compiler_params=_params(("arbitrary", "arbitrary"), vmem),
        name="ffn_in",
    )(h, w_gate, w_up, conv_w)


def _pad_last(w, total):
    return jnp.pad(w, ((0, 0),) * (w.ndim - 1) + ((0, total - w.shape[-1]),))


def _bf16_weights(w_in, w_out, w_gate, w_up, ffn_conv, w_down, gw, dn_heads):
    o_dn_a = 10 * gw
    o_fx = o_dn_a + 2 * dn_heads
    o_fx_f = o_fx + 3 * gw
    wide = jnp.concatenate([w_in[..., :o_dn_a], w_in[..., o_fx:o_fx_f]], axis=-1).astype(_BF16)
    gates = _pad_last(jnp.concatenate([w_in[..., o_dn_a:o_fx], w_in[..., o_fx_f:]], axis=-1).astype(_BF16),
                      _V7X_LANES)
    ff = w_gate.shape[-1]
    ffp = -(-ff // _FF_ALIGN) * _FF_ALIGN if ff > _FF_ALIGN else ff
    return dict(
        wide=wide, gates=gates, out=w_out.astype(_BF16),
        gate=_pad_last(w_gate.astype(_BF16), ffp), up=_pad_last(w_up.astype(_BF16), ffp),
        conv=_pad_last(ffn_conv.astype(_F32), ffp),
        down=jnp.pad(w_down.astype(_BF16), ((0, 0), (0, ffp - ff), (0, 0))))


def _layer(x, layer, batch, seq_len, wts, attn_norm, sc_conv, lam_vecs, diff_norm, dn_conv, dn_a_log,
           dn_dt_bias, dn_norm, fox_bias, ffn_norm):
    d = lam_vecs[0].shape[-1]
    gw = wts["out"].shape[1] // 4
    dn_heads, fox_heads = dn_a_log.shape[-1], fox_bias.shape[-1]
    diff_heads = gw // diff_norm.shape[-1]
    col = lambda idx: idx * gw

    h = _rmsnorm(x, attn_norm, _BF16)
    p = _matmul([h], wts["wide"], layer, _BF16, name="in_proj")
    gates = _matmul([h], wts["gates"], layer, _F32, name="in_proj_gates")

    y_sc = _short_conv(p, sc_conv, seq_len, 0, 1, 2, gw)
    lam_init = 0.8 - 0.6 * math.exp(-0.3 * layer)
    y_df = _diff_attention(p, lam_vecs, diff_norm, lam_init, batch, seq_len, col(3), col(4), col(5), diff_heads, d)

    gates_t = gates.reshape(batch, seq_len, _V7X_LANES).transpose(0, 2, 1)
    fox_rows = gates_t[:, 2 * dn_heads:2 * dn_heads + fox_heads].reshape(batch * fox_heads, seq_len)
    fox_bias_rows = jnp.tile(fox_bias.astype(_F32), batch).reshape(batch * fox_heads, 1)
    cum = _fox_cumsum(fox_rows, fox_bias_rows)
    y_fx = _fox_attention(p, cum, batch, seq_len, col(10), col(11), col(12), fox_heads, d)

    y_dn = _gated_deltanet(p, gates, gates_t[:, :2 * dn_heads], dn_conv, dn_a_log, dn_dt_bias, dn_norm,
                           batch, seq_len, col(6), col(7), col(8), col(9), dn_heads, d)

    x = _matmul([y_sc, y_df, y_dn, y_fx], wts["out"], layer, _F32, residual=x, tn_target=512, name="out_proj")

    h2 = _rmsnorm(x, ffn_norm, _BF16)
    act = _ffn_in(h2, wts["gate"], wts["up"], wts["conv"], layer, seq_len)
    return _matmul_acc(act, wts["down"], layer, x, name="ffn_out")


def kernel(x, attn_norm, w_in, sc_conv, lam_q1, lam_k1, lam_q2, lam_k2, diff_norm, dn_conv, dn_a_log,
           dn_dt_bias, dn_norm, fox_bias, w_out, ffn_norm, w_gate, w_up, ffn_conv, w_down, final_norm):
    batch, seq_len, d_model = x.shape
    xf = x.reshape(batch * seq_len, d_model).astype(_F32)
    wts = _bf16_weights(w_in, w_out, w_gate, w_up, ffn_conv, w_down, w_out.shape[1] // 4, dn_a_log.shape[-1])
    for l in range(w_in.shape[0]):
        xf = _layer(xf, l, batch, seq_len, wts, attn_norm[l], sc_conv[l],
                    (lam_q1[l], lam_k1[l], lam_q2[l], lam_k2[l]), diff_norm[l], dn_conv[l], dn_a_log[l],
                    dn_dt_bias[l], dn_norm[l], fox_bias[l], ffn_norm[l])
    out = _rmsnorm(xf, final_norm, x.dtype)
    return out.reshape(batch, seq_len, d_model)
```

```python
import functools
import math

import jax
import jax.numpy as jnp
from jax import lax
from jax.experimental import pallas as pl
from jax.experimental.pallas import tpu as pltpu

_F32 = jnp.float32
_BF16 = jnp.bfloat16
_EPS = 1e-6
_NEG_INF = float("-inf")

_V7X_VMEM_BYTES = 64 * 1024 * 1024
_V7X_LANES = 128
_V7X_SUBLANES = 8
_BF16_ROWS = 16
_DN_CHUNK = 128
_FF_ALIGN = 1024


def _tile(n, target, align):
    t = min(target, n)
    t -= t % align
    while t >= align:
        if n % t == 0:
            return t
        t -= align
    return n


def _params(semantics, vmem_bytes):
    limit = min(int(vmem_bytes) + (8 << 20), _V7X_VMEM_BYTES - (4 << 20))
    return pltpu.CompilerParams(dimension_semantics=semantics, vmem_limit_bytes=limit)


def _silu(x):
    return x * (1.0 / (1.0 + jnp.exp(-x)))


def _softplus(x):
    return jnp.maximum(x, 0.0) + jnp.log1p(jnp.exp(-jnp.abs(x)))


def _shift_rows(x, prev8, s):
    xs = pltpu.roll(x, s, 0)
    ps = pltpu.roll(prev8, s, 0)
    row = lax.broadcasted_iota(jnp.int32, prev8.shape, 0)
    head = jnp.where(row < s, ps, xs[:_V7X_SUBLANES])
    return jnp.concatenate([head, xs[_V7X_SUBLANES:]], axis=0)


def _causal_conv(x, prev8, w):
    k = w.shape[0]
    out = x * w[k - 1:k]
    for j in range(k - 1):
        out = out + _shift_rows(x, prev8, k - 1 - j) * w[j:j + 1]
    return out


def _rmsnorm_kernel(x_ref, g_ref, o_ref):
    x = x_ref[...]
    y = x * lax.rsqrt(jnp.mean(x * x, axis=-1, keepdims=True) + _EPS)
    o_ref[...] = (y * g_ref[...]).astype(o_ref.dtype)


def _rmsnorm(x, g, out_dtype):
    n, d = x.shape
    tm = _tile(n, 256, _BF16_ROWS)
    vmem = 2 * tm * d * (4 + jnp.dtype(out_dtype).itemsize) + 2 * d * 4
    return pl.pallas_call(
        _rmsnorm_kernel,
        grid=(n // tm,),
        in_specs=[pl.BlockSpec((tm, d), lambda i: (i, 0)), pl.BlockSpec((1, d), lambda i: (0, 0))],
        out_specs=pl.BlockSpec((tm, d), lambda i: (i, 0)),
        out_shape=jax.ShapeDtypeStruct((n, d), out_dtype),
        compiler_params=_params(("parallel",), vmem),
        name="rmsnorm",
    )(x, g.reshape(1, d).astype(_F32))


def _mm_kernel(*refs, n_pairs, has_res):
    a_refs, w_refs = refs[:n_pairs], refs[n_pairs:2 * n_pairs]
    o_ref = refs[-1]
    acc = jnp.dot(a_refs[0][...], w_refs[0][...], preferred_element_type=_F32)
    for a_ref, w_ref in zip(a_refs[1:], w_refs[1:]):
        acc = acc + jnp.dot(a_ref[...], w_ref[...], preferred_element_type=_F32)
    if has_res:
        acc = acc + refs[2 * n_pairs][...]
    o_ref[...] = acc.astype(o_ref.dtype)


def _matmul(a_list, w, layer, out_dtype, residual=None, tm_target=1024, tn_target=1024, name="matmul"):
    m, n = a_list[0].shape[0], w.shape[2]
    kb = a_list[0].shape[1]
    assert all(a.shape[1] == kb for a in a_list) and w.shape[1] == kb * len(a_list)
    tm = _tile(m, tm_target, _BF16_ROWS)
    tn = _tile(n, tn_target, _V7X_LANES)
    ktot = w.shape[1]
    in_specs = [pl.BlockSpec((tm, kb), lambda i, j: (i, 0)) for _ in a_list]
    in_specs += [pl.BlockSpec((None, kb, tn), functools.partial(lambda i, j, r: (layer, r, j), r=r))
                 for r in range(len(a_list))]
    args = list(a_list) + [w] * len(a_list)
    vmem = 2 * 2 * ktot * (tm + tn) + 2 * tm * tn * jnp.dtype(out_dtype).itemsize + tm * tn * 4
    if residual is not None:
        in_specs.append(pl.BlockSpec((tm, tn), lambda i, j: (i, j)))
        args.append(residual)
        vmem += 2 * tm * tn * 4
    return pl.pallas_call(
        functools.partial(_mm_kernel, n_pairs=len(a_list), has_res=residual is not None),
        grid=(m // tm, n // tn),
        in_specs=in_specs,
        out_specs=pl.BlockSpec((tm, tn), lambda i, j: (i, j)),
        out_shape=jax.ShapeDtypeStruct((m, n), out_dtype),
        compiler_params=_params(("parallel", "parallel"), vmem),
        name=name,
    )(*args)


def _mm_acc_kernel(a_ref, w_ref, r_ref, o_ref, acc_ref, *, nk):
    k = pl.program_id(2)
    prod = lambda: jnp.dot(a_ref[...], w_ref[...], preferred_element_type=_F32)
    if nk == 1:
        o_ref[...] = r_ref[...] + prod()
        return

    @pl.when(k == 0)
    def _():
        acc_ref[...] = r_ref[...] + prod()

    @pl.when(jnp.logical_and(k > 0, k < nk - 1))
    def _():
        acc_ref[...] += prod()

    @pl.when(k == nk - 1)
    def _():
        o_ref[...] = acc_ref[...] + prod()


def _matmul_acc(a, w, layer, residual, tm_target=1024, tn_target=1024, tk_target=2816, name="matmul_acc"):
    m, kdim = a.shape
    n = w.shape[2]
    tm = _tile(m, tm_target, _BF16_ROWS)
    tn = _tile(n, tn_target, _V7X_LANES)
    tk = _tile(kdim, tk_target, _V7X_LANES)
    vmem = 2 * 2 * tk * (tm + tn) + 5 * tm * tn * 4
    return pl.pallas_call(
        functools.partial(_mm_acc_kernel, nk=kdim // tk),
        grid=(m // tm, n // tn, kdim // tk),
        in_specs=[pl.BlockSpec((tm, tk), lambda i, j, k: (i, k)),
                  pl.BlockSpec((None, tk, tn), lambda i, j, k: (layer, k, j)),
                  pl.BlockSpec((tm, tn), lambda i, j, k: (i, j))],
        out_specs=pl.BlockSpec((tm, tn), lambda i, j, k: (i, j)),
        out_shape=jax.ShapeDtypeStruct((m, n), _F32),
        scratch_shapes=[pltpu.VMEM((tm, tn), _F32)],
        compiler_params=_params(("parallel", "parallel", "arbitrary"), vmem),
        name=name,
    )(a, w, residual)


def _sconv_kernel(h_ref, c_ref, b_ref, hh_ref, ch_ref, w_ref, o_ref, *, blocks_per_seq):
    i = pl.program_id(0)
    x = c_ref[...].astype(_F32) * h_ref[...].astype(_F32)
    halo = ch_ref[...].astype(_F32) * hh_ref[...].astype(_F32)
    prev8 = jnp.where(i % blocks_per_seq == 0, 0.0, halo[_BF16_ROWS - _V7X_SUBLANES:])
    y = b_ref[...].astype(_F32) * _causal_conv(x, prev8, w_ref[...])
    o_ref[...] = y.astype(o_ref.dtype)


def _short_conv(p, conv_w, seq_len, col_h, col_c, col_b, width):
    n = p.shape[0]
    tm = _tile(seq_len, 512, _BF16_ROWS)
    hb = tm // _BF16_ROWS
    cur = lambda c: pl.BlockSpec((tm, width), lambda i: (i, c))
    halo = lambda c: pl.BlockSpec((_BF16_ROWS, width), lambda i: (jnp.maximum(i * hb - 1, 0), c))
    k = conv_w.shape[0]
    vmem = 2 * (4 * tm + 2 * _BF16_ROWS) * width * 2 + 8 * tm * width * 4
    return pl.pallas_call(
        functools.partial(_sconv_kernel, blocks_per_seq=seq_len // tm),
        grid=(n // tm,),
        in_specs=[cur(col_h), cur(col_c), cur(col_b), halo(col_h), halo(col_c),
                  pl.BlockSpec((k, width), lambda i: (0, 0))],
        out_specs=pl.BlockSpec((tm, width), lambda i: (i, 0)),
        out_shape=jax.ShapeDtypeStruct((n, width), _BF16),
        compiler_params=_params(("parallel",), vmem),
        name="short_conv",
    )(p, p, p, p, p, conv_w.astype(_F32))


_LOG2E = math.log2(math.e)


def _lane_tiles(x, width):
    return [x[:, c:c + _V7X_LANES] for c in range(0, width, _V7X_LANES)]


def _lane_repeat(x, width):
    return jnp.concatenate([x] * (width // _V7X_LANES), axis=1)


def _softmax_block(s, m_prev):
    m_new = jnp.maximum(m_prev, jnp.max(s, axis=-1, keepdims=True))
    alpha = jnp.exp2(m_prev - m_new)
    p = jnp.concatenate([jnp.exp2(t - m_new) for t in _lane_tiles(s, s.shape[1])], axis=1)
    return m_new, alpha, p


def _causal_mask(s, q0, k0):
    row = q0 + lax.broadcasted_iota(jnp.int32, s.shape, 0)
    col = k0 + lax.broadcasted_iota(jnp.int32, s.shape, 1)
    return jnp.where(col <= row, s, _NEG_INF)


def _qk(q, k):
    return lax.dot_general(q, k, (((1,), (1,)), ((), ())), preferred_element_type=_F32)


def _flash_blocks(seq_len):
    bq = _tile(seq_len, 1024, _V7X_LANES)
    bk = bq // 2 if bq % (2 * _V7X_LANES) == 0 else bq
    return bq, bk


def _kv_sweep(qi, r, bk, step):
    def trip(i, carry):
        for d in range(r):
            step(i * r + d, False, 0)
        return carry

    lax.fori_loop(0, qi, trip, 0)
    for d in range(r):
        step(qi * r + d, True, d * bk)


def _diff_kernel(q_ref, k_ref, v_ref, lq1_ref, lk1_ref, lq2_ref, lk2_ref, g_ref, o_ref,
                 m_ref, l_ref, acc_ref, *, bq, bk, d, scale, lam_init):
    qi = pl.program_id(2)
    hw = 2 * d
    for mi in range(2):
        m_ref[mi] = jnp.full(m_ref.shape[1:], _NEG_INF, _F32)
        l_ref[mi] = jnp.zeros(l_ref.shape[1:], _F32)
        acc_ref[mi] = jnp.zeros(acc_ref.shape[1:], _F32)
    q = (q_ref[...].astype(_F32) * (scale * _LOG2E)).astype(_BF16)

    def step(j, masked, row0):
        k0 = pl.multiple_of(j * bk, bk)
        k = k_ref[pl.ds(k0, bk), :]
        v = v_ref[pl.ds(k0, bk), :]
        rows = slice(row0, bq)
        for mi in range(2):
            s = _qk(q[rows, mi * d:(mi + 1) * d], k[:, mi * d:(mi + 1) * d])
            if masked:
                s = _causal_mask(s, qi * bq + row0, k0)
            m_new, alpha, p = _softmax_block(s, m_ref[mi, rows])
            l_ref[mi, rows] = alpha * l_ref[mi, rows] + jnp.sum(p, axis=-1, keepdims=True)
            acc_ref[mi, rows] = (_lane_repeat(alpha, hw) * acc_ref[mi, rows]
                                 + jnp.dot(p.astype(_BF16), v, preferred_element_type=_F32))
            m_ref[mi, rows] = m_new

    _kv_sweep(qi, bq // bk, bk, step)
    lam = (jnp.exp(jnp.sum(lq1_ref[...] * lk1_ref[...], axis=-1, keepdims=True))
           - jnp.exp(jnp.sum(lq2_ref[...] * lk2_ref[...], axis=-1, keepdims=True)) + lam_init)
    o = acc_ref[0] / _lane_repeat(l_ref[0], hw) - lam * (acc_ref[1] / _lane_repeat(l_ref[1], hw))
    o = o * lax.rsqrt(jnp.mean(o * o, axis=-1, keepdims=True) + _EPS) * g_ref[...]
    o_ref[...] = (o * (1.0 - lam_init)).astype(o_ref.dtype)


def _diff_attention(p, lam_vecs, norm_g, lam_init, batch, seq_len, col_q, col_k, col_v, heads, d):
    n = p.shape[0]
    hw = 2 * d
    bq, bk = _flash_blocks(seq_len)
    nq = seq_len // bq
    vec = pl.BlockSpec((1, d), lambda b, h, i: (0, 0))
    vmem = 2 * 2 * seq_len * hw * 2 + 4 * bq * hw * 2 + 2 * bq * (hw + 2 * _V7X_LANES) * 4 + 12 * bq * bk * 4
    return pl.pallas_call(
        functools.partial(_diff_kernel, bq=bq, bk=bk, d=d, scale=d ** -0.5, lam_init=lam_init),
        grid=(batch, heads, nq),
        in_specs=[pl.BlockSpec((bq, hw), lambda b, h, i: (b * nq + i, col_q // hw + h)),
                  pl.BlockSpec((seq_len, hw), lambda b, h, i: (b, col_k // hw + h)),
                  pl.BlockSpec((seq_len, hw), lambda b, h, i: (b, col_v // hw + h)),
                  vec, vec, vec, vec,
                  pl.BlockSpec((1, hw), lambda b, h, i: (0, 0))],
        out_specs=pl.BlockSpec((bq, hw), lambda b, h, i: (b * nq + i, h)),
        out_shape=jax.ShapeDtypeStruct((n, heads * hw), _BF16),
        scratch_shapes=[pltpu.VMEM((2, bq, _V7X_LANES), _F32), pltpu.VMEM((2, bq, _V7X_LANES), _F32),
                        pltpu.VMEM((2, bq, hw), _F32)],
        compiler_params=_params(("parallel", "parallel", "arbitrary"), vmem),
        name="diff_attention",
    )(p, p, p, *[v.reshape(1, d).astype(_F32) for v in lam_vecs], norm_g.reshape(1, hw).astype(_F32))


def _fox_cum_kernel(f_ref, bias_ref, o_ref):
    x = f_ref[...] + bias_ref[...]
    ls = jnp.minimum(x, 0.0) - jnp.log1p(jnp.exp(-jnp.abs(x)))
    r = lax.broadcasted_iota(jnp.int32, (_V7X_LANES, _V7X_LANES), 0)
    c = lax.broadcasted_iota(jnp.int32, (_V7X_LANES, _V7X_LANES), 1)
    upper = (r <= c).astype(_F32)
    carry = jnp.zeros((x.shape[0], 1), _F32)
    for t in range(x.shape[1] // _V7X_LANES):
        sl = slice(t * _V7X_LANES, (t + 1) * _V7X_LANES)
        w = jnp.dot(ls[:, sl], upper, precision=lax.Precision.HIGHEST, preferred_element_type=_F32) + carry
        o_ref[:, sl] = w
        carry = w[:, _V7X_LANES - 1:]


def _fox_cumsum(f_rows, bias_rows):
    return pl.pallas_call(
        _fox_cum_kernel,
        out_shape=jax.ShapeDtypeStruct(f_rows.shape, _F32),
        name="fox_cumsum",
    )(f_rows, bias_rows)


def _fox_kernel(q_ref, k_ref, v_ref, cq_ref, ck_ref, o_ref, m_ref, acc_ref, *, bq, bk, d, scale):
    qi = pl.program_id(2)
    m_ref[...] = jnp.full(m_ref.shape, _NEG_INF, _F32)
    acc_ref[...] = jnp.zeros(acc_ref.shape, _F32)
    q = (q_ref[...].astype(_F32) * (scale * _LOG2E)).astype(_BF16)
    cq = jnp.broadcast_to(cq_ref[0, 0] * _LOG2E, (bq, _V7X_LANES))
    ones = jnp.ones((bk, _V7X_LANES), _BF16)

    def step(j, masked, row0):
        k0 = pl.multiple_of(j * bk, bk)
        rows = slice(row0, bq)
        s = _qk(q[rows], k_ref[pl.ds(k0, bk), :])
        ck = ck_ref[0, 0, pl.ds(j, 1), :] * _LOG2E
        s = jnp.concatenate([st + (cq[rows] - ct) for st, ct in zip(_lane_tiles(s, bk), _lane_tiles(ck, bk))],
                            axis=1)
        if masked:
            s = _causal_mask(s, qi * bq + row0, k0)
        m_new, alpha, p = _softmax_block(s, m_ref[rows])
        v_ones = jnp.concatenate([v_ref[pl.ds(k0, bk), :], ones], axis=1)
        acc_ref[rows] = (_lane_repeat(alpha, d + _V7X_LANES) * acc_ref[rows]
                         + jnp.dot(p.astype(_BF16), v_ones, preferred_element_type=_F32))
        m_ref[rows] = m_new

    _kv_sweep(qi, bq // bk, bk, step)
    acc = acc_ref[...]
    o_ref[...] = (acc[:, :d] / _lane_repeat(acc[:, d:], d)).astype(o_ref.dtype)


def _fox_attention(p, cum, batch, seq_len, col_q, col_k, col_v, heads, d):
    n = p.shape[0]
    bq, bk = _flash_blocks(seq_len)
    nq, nk = seq_len // bq, seq_len // bk
    cum_rows = cum.reshape(batch, heads, nk, bk)
    cum_cols = cum.reshape(batch, heads, seq_len, 1)
    vmem = (2 * 2 * seq_len * d * 2 + 4 * bq * d * 2 + 2 * seq_len * 4 + 2 * bq * _V7X_LANES * 4
            + bq * (d + 2 * _V7X_LANES) * 4 + 10 * bq * bk * 4)
    return pl.pallas_call(
        functools.partial(_fox_kernel, bq=bq, bk=bk, d=d, scale=d ** -0.5),
        grid=(batch, heads, nq),
        in_specs=[pl.BlockSpec((bq, d), lambda b, h, i: (b * nq + i, col_q // d + h)),
                  pl.BlockSpec((seq_len, d), lambda b, h, i: (b, col_k // d + h)),
                  pl.BlockSpec((seq_len, d), lambda b, h, i: (b, col_v // d + h)),
                  pl.BlockSpec((1, 1, bq, 1), lambda b, h, i: (b, h, i, 0)),
                  pl.BlockSpec((1, 1, nk, bk), lambda b, h, i: (b, h, 0, 0))],
        out_specs=pl.BlockSpec((bq, d), lambda b, h, i: (b * nq + i, h)),
        out_shape=jax.ShapeDtypeStruct((n, heads * d), _BF16),
        scratch_shapes=[pltpu.VMEM((bq, _V7X_LANES), _F32), pltpu.VMEM((bq, d + _V7X_LANES), _F32)],
        compiler_params=_params(("parallel", "parallel", "arbitrary"), vmem),
        name="fox_attention",
    )(p, p, p, cum_cols, cum_rows)


def _lane_column(x, h):
    lane = lax.broadcasted_iota(jnp.int32, x.shape, 1)
    return jnp.sum(jnp.where(lane == h, x, 0.0), axis=-1, keepdims=True)


def _bdot(a, b):
    return jnp.dot(a.astype(_BF16), b.astype(_BF16), preferred_element_type=_F32)


_INV_BASE_LOG2 = 4


def _unit_lower_inverses(mats, ri, ci):
    c = mats[0].shape[0]
    same_block = lambda log2n: (ri >> log2n) == (ci >> log2n)
    eye = (ri == ci).astype(_F32)
    base = same_block(_INV_BASE_LOG2)
    pw = [jnp.where(base, -a, 0.0) for a in mats]
    tinv = [eye + x for x in pw]
    for _ in range(_INV_BASE_LOG2 - 1):
        pw = [_bdot(x, x) for x in pw]
        tinv = [t + _bdot(t, x) for t, x in zip(tinv, pw)]
    log2n = _INV_BASE_LOG2
    while (1 << log2n) < c:
        level = same_block(log2n + 1) & jnp.logical_not(same_block(log2n))
        off = [_bdot(t, jnp.where(level, a, 0.0)) for t, a in zip(tinv, mats)]
        tinv = [t - _bdot(o, t) for t, o in zip(tinv, off)]
        log2n += 1
    return tinv


def _deltanet_kernel(q_ref, k_ref, v_ref, z_ref, gc_ref, gr_ref, cw_ref, alog_c_ref, dtb_c_ref,
                     alog_r_ref, dtb_r_ref, ng_ref, o_ref, s_ref, tail_ref, *, heads, d, c):
    t = pl.program_id(1)
    width = heads * d

    @pl.when(t == 0)
    def _():
        s_ref[...] = jnp.zeros(s_ref.shape, _F32)
        tail_ref[...] = jnp.zeros(tail_ref.shape, _F32)

    cw = cw_ref[...]
    conv = []
    for idx, ref in enumerate((q_ref, k_ref, v_ref)):
        raw = ref[...].astype(_F32)
        cols = slice(idx * width, (idx + 1) * width)
        conv.append(_silu(_causal_conv(raw, tail_ref[:, cols], cw[:, cols])))
        tail_ref[:, cols] = raw[c - _V7X_SUBLANES:]

    gates_c = gc_ref[...]
    g_cols = -jnp.exp(alog_c_ref[...]) * _softplus(gates_c + dtb_c_ref[...])
    beta_cols = 1.0 / (1.0 + jnp.exp(-gates_c))
    gates_r = gr_ref[0]
    g_rows = -jnp.exp(alog_r_ref[...]) * _softplus(gates_r[:heads] + dtb_r_ref[...])
    ri = lax.broadcasted_iota(jnp.int32, (c, c), 0)
    ci = lax.broadcasted_iota(jnp.int32, (c, c), 1)
    incl = ri >= ci
    strict = ri > ci
    hp = lax.Precision.HIGHEST
    gcum_cols = jnp.dot(incl.astype(_F32), g_cols, precision=hp, preferred_element_type=_F32)
    gcum_rows = jnp.dot(g_rows, (ri <= ci).astype(_F32), precision=hp, preferred_element_type=_F32)

    hr = range(heads)
    cols = [slice(h * d, (h + 1) * d) for h in hr]
    unit = lambda x: x * lax.rsqrt(jnp.sum(x * x, axis=-1, keepdims=True) + _EPS)
    q = [unit(conv[0][:, s]) * (d ** -0.5) for s in cols]
    k = [unit(conv[1][:, s]) for s in cols]
    v = [conv[2][:, s] for s in cols]
    gc = [_lane_column(gcum_cols, h) for h in hr]
    beta = [_lane_column(beta_cols, heads + h) for h in hr]
    decay = [jnp.exp(jnp.where(incl, gc[h] - gcum_rows[h:h + 1], _NEG_INF)) for h in hr]
    k16 = [x.astype(_BF16) for x in k]
    kb = [k[h] * beta[h] for h in hr]
    a = [jnp.where(strict, _qk(kb[h].astype(_BF16), k16[h]) * decay[h], 0.0) for h in hr]
    tinv = _unit_lower_inverses(a, ri, ci)
    egc = [jnp.exp(x) for x in gc]
    wu = [_bdot(tinv[h], jnp.concatenate([kb[h] * egc[h], v[h] * beta[h]], axis=1)) for h in hr]
    aqk = [_qk(q[h].astype(_BF16), k16[h]) * decay[h] for h in hr]
    s = [s_ref[h] for h in hr]
    ws_qs = [_bdot(jnp.concatenate([wu[h][:, :d], q[h] * egc[h]], axis=0), s[h]) for h in hr]
    v_new = [wu[h][:, d:] - ws_qs[h][:c] for h in hr]
    o = [ws_qs[h][c:] + _bdot(aqk[h], v_new[h]) for h in hr]
    g_last = [x[c - 1:c] for x in gc]
    k_dec = [k[h] * jnp.exp(g_last[h] - gc[h]) for h in hr]
    for h in hr:
        s_ref[h] = s[h] * jnp.exp(g_last[h]) + lax.dot_general(
            k_dec[h].astype(_BF16), v_new[h].astype(_BF16), (((0,), (0,)), ((), ())), preferred_element_type=_F32)
    ng = ng_ref[...]
    for h in hr:
        on = o[h] * lax.rsqrt(jnp.mean(o[h] * o[h], axis=-1, keepdims=True) + _EPS) * ng
        o_ref[:, cols[h]] = (on * _silu(z_ref[:, cols[h]].astype(_F32))).astype(o_ref.dtype)


def _gated_deltanet(p, gates_cols, gates_rows, conv_w, a_log, dt_bias, norm_g, batch, seq_len,
                    col_q, col_k, col_v, col_z, heads, d):
    n = p.shape[0]
    c = _tile(seq_len, _DN_CHUNK, _BF16_ROWS)
    nt = seq_len // c
    width = heads * d
    lanes = gates_cols.shape[1]
    kw = conv_w.shape[0]
    col = lambda off: pl.BlockSpec((c, width), lambda b, t: (b * nt + t, off // width))
    const = lambda shape: pl.BlockSpec(shape, lambda b, t: (0,) * len(shape))
    pad_lane = lambda v: jnp.zeros((1, lanes), _F32).at[0, :heads].set(v.astype(_F32))
    vmem = 2 * 5 * c * width * 2 + 12 * c * width * 4 + heads * d * d * 4 + 2 * kw * 3 * width * 4
    return pl.pallas_call(
        functools.partial(_deltanet_kernel, heads=heads, d=d, c=c),
        grid=(batch, nt),
        in_specs=[col(col_q), col(col_k), col(col_v), col(col_z),
                  pl.BlockSpec((c, lanes), lambda b, t: (b * nt + t, 0)),
                  pl.BlockSpec((1, gates_rows.shape[1], c), lambda b, t: (b, 0, t)),
                  const((kw, 3 * width)), const((1, lanes)), const((1, lanes)),
                  const((heads, 1)), const((heads, 1)), const((1, d))],
        out_specs=pl.BlockSpec((c, width), lambda b, t: (b * nt + t, 0)),
        out_shape=jax.ShapeDtypeStruct((n, width), _BF16),
        scratch_shapes=[pltpu.VMEM((heads, d, d), _F32), pltpu.VMEM((_V7X_SUBLANES, 3 * width), _F32)],
        compiler_params=_params(("parallel", "arbitrary"), vmem),
        name="gated_deltanet",
    )(p, p, p, p, gates_cols, gates_rows, conv_w.astype(_F32), pad_lane(a_log), pad_lane(dt_bias),
      a_log.reshape(heads, 1).astype(_F32), dt_bias.reshape(heads, 1).astype(_F32),
      norm_g.reshape(1, d).astype(_F32))


def _ffn_in_kernel(h_ref, wg_ref, wu_ref, cw_ref, o_ref, tail_ref, *, blocks_per_seq, parts):
    i, j = pl.program_id(0), pl.program_id(1)
    tm, tn = o_ref.shape

    @pl.when(i % blocks_per_seq == 0)
    def _():
        tail_ref[j] = jnp.zeros(tail_ref.shape[1:], _F32)

    h = h_ref[...]
    for c in range(parts):
        cols = slice(c * (tn // parts), (c + 1) * (tn // parts))
        g = jnp.dot(h, wg_ref[:, cols], preferred_element_type=_F32)
        u = jnp.dot(h, wu_ref[:, cols], preferred_element_type=_F32)
        a = _causal_conv(g, tail_ref[j, :, cols], cw_ref[:, cols])
        tail_ref[j, :, cols] = g[tm - _V7X_SUBLANES:]
        o_ref[:, cols] = (_silu(a) * u).astype(o_ref.dtype)


def _ffn_in(h, w_gate, w_up, conv_w, layer, seq_len, tm_target=1024, tn_target=512):
    n, d = h.shape
    ff = w_gate.shape[2]
    tm = _tile(seq_len, tm_target, _BF16_ROWS)
    tn = _tile(ff, tn_target, _V7X_LANES)
    parts = 2 if tn % (2 * _V7X_LANES) == 0 else 1
    kw = conv_w.shape[1]
    wspec = pl.BlockSpec((None, d, tn), lambda i, j: (layer, 0, j))
    vmem = 2 * 2 * d * (tm + 2 * tn) + 2 * tm * tn * 2 + 6 * tm * tn * 4 + ff * _V7X_SUBLANES * 4
    return pl.pallas_call(
        functools.partial(_ffn_in_kernel, blocks_per_seq=seq_len // tm, parts=parts),
        grid=(n // tm, ff // tn),
        in_specs=[pl.BlockSpec((tm, d), lambda i, j: (i, 0)), wspec, wspec,
                  pl.BlockSpec((None, kw, tn), lambda i, j: (layer, 0, j))],
        out_specs=pl.BlockSpec((tm, tn), lambda i, j: (i, j)),
        out_shape=jax.ShapeDtypeStruct((n, ff), _BF16),
        scratch_shapes=[pltpu.VMEM((ff // tn, _V7X_SUBLANES, tn), _F32)],
        compiler_params=_params(("arbitrary", "arbitrary"), vmem),
        name="ffn_in",
    )(h, w_gate, w_up, conv_w)


def _cast_pad_kernel(x_ref, o_ref, *, rows, cols):
    i, j = pl.program_id(1), pl.program_id(2)
    tr, tc = o_ref.shape
    r = i * tr + lax.broadcasted_iota(jnp.int32, (tr, tc), 0)
    c = j * tc + lax.broadcasted_iota(jnp.int32, (tr, tc), 1)
    o_ref[...] = jnp.where(jnp.logical_and(r < rows, c < cols), x_ref[...], 0.0).astype(o_ref.dtype)


def _cast_pad(w, rows_to, cols_to, dtype):
    layers, rows, cols = w.shape
    tr = _tile(rows_to, 512, _BF16_ROWS)
    tc = _tile(cols_to, 1024, _V7X_LANES)
    assert rows_to - rows < tr and cols_to - cols < tc
    return pl.pallas_call(
        functools.partial(_cast_pad_kernel, rows=rows, cols=cols),
        grid=(layers, rows_to // tr, cols_to // tc),
        in_specs=[pl.BlockSpec((None, tr, tc), lambda l, i, j: (l, i, j))],
        out_specs=pl.BlockSpec((None, tr, tc), lambda l, i, j: (l, i, j)),
        out_shape=jax.ShapeDtypeStruct((layers, rows_to, cols_to), dtype),
        compiler_params=_params(("parallel", "parallel", "parallel"), 2 * tr * tc * 6),
        name="cast_pad",
    )(w)


def _pad_last(w, total):
    return jnp.pad(w, ((0, 0),) * (w.ndim - 1) + ((0, total - w.shape[-1]),))


def _bf16_weights(w_in, w_out, w_gate, w_up, ffn_conv, w_down, gw, dn_heads):
    o_dn_a = 10 * gw
    o_fx = o_dn_a + 2 * dn_heads
    o_fx_f = o_fx + 3 * gw
    wide = jnp.concatenate([w_in[..., :o_dn_a], w_in[..., o_fx:o_fx_f]], axis=-1).astype(_BF16)
    gates = _pad_last(jnp.concatenate([w_in[..., o_dn_a:o_fx], w_in[..., o_fx_f:]], axis=-1).astype(_BF16),
                      _V7X_LANES)
    ff = w_gate.shape[-1]
    ffp = -(-ff // _FF_ALIGN) * _FF_ALIGN if ff > _FF_ALIGN else ff
    return dict(
        wide=wide, gates=gates, out=w_out.astype(_BF16),
        gate=_cast_pad(w_gate, w_gate.shape[1], ffp, _BF16), up=_cast_pad(w_up, w_up.shape[1], ffp, _BF16),
        conv=_pad_last(ffn_conv.astype(_F32), ffp),
        down=_cast_pad(w_down, ffp, w_down.shape[2], _BF16))


def _layer(x, layer, batch, seq_len, wts, attn_norm, sc_conv, lam_vecs, diff_norm, dn_conv, dn_a_log,
           dn_dt_bias, dn_norm, fox_bias, ffn_norm):
    d = lam_vecs[0].shape[-1]
    gw = wts["out"].shape[1] // 4
    dn_heads, fox_heads = dn_a_log.shape[-1], fox_bias.shape[-1]
    diff_heads = gw // diff_norm.shape[-1]
    col = lambda idx: idx * gw

    h = _rmsnorm(x, attn_norm, _BF16)
    p = _matmul([h], wts["wide"], layer, _BF16, name="in_proj")
    gates = _matmul([h], wts["gates"], layer, _F32, name="in_proj_gates")

    y_sc = _short_conv(p, sc_conv, seq_len, 0, 1, 2, gw)
    lam_init = 0.8 - 0.6 * math.exp(-0.3 * layer)
    y_df = _diff_attention(p, lam_vecs, diff_norm, lam_init, batch, seq_len, col(3), col(4), col(5), diff_heads, d)

    gates_t = gates.reshape(batch, seq_len, _V7X_LANES).transpose(0, 2, 1)
    fox_rows = gates_t[:, 2 * dn_heads:2 * dn_heads + fox_heads].reshape(batch * fox_heads, seq_len)
    fox_bias_rows = jnp.tile(fox_bias.astype(_F32), batch).reshape(batch * fox_heads, 1)
    cum = _fox_cumsum(fox_rows, fox_bias_rows)
    y_fx = _fox_attention(p, cum, batch, seq_len, col(10), col(11), col(12), fox_heads, d)

    y_dn = _gated_deltanet(p, gates, gates_t[:, :2 * dn_heads], dn_conv, dn_a_log, dn_dt_bias, dn_norm,
                           batch, seq_len, col(6), col(7), col(8), col(9), dn_heads, d)

    x = _matmul([y_sc, y_df, y_dn, y_fx], wts["out"], layer, _F32, residual=x, tn_target=512, name="out_proj")

    h2 = _rmsnorm(x, ffn_norm, _BF16)
    act = _ffn_in(h2, wts["gate"], wts["up"], wts["conv"], layer, seq_len)
    return _matmul_acc(act, wts["down"], layer, x, name="ffn_out")


def kernel(x, attn_norm, w_in, sc_conv, lam_q1, lam_k1, lam_q2, lam_k2, diff_norm, dn_conv, dn_a_log,
           dn_dt_bias, dn_norm, fox_bias, w_out, ffn_norm, w_gate, w_up, ffn_conv, w_down, final_norm):
    batch, seq_len, d_model = x.shape
    xf = x.reshape(batch * seq_len, d_model).astype(_F32)
    wts = _bf16_weights(w_in, w_out, w_gate, w_up, ffn_conv, w_down, w_out.shape[1] // 4, dn_a_log.shape[-1])
    for l in range(w_in.shape[0]):
        xf = _layer(xf, l, batch, seq_len, wts, attn_norm[l], sc_conv[l],
                    (lam_q1[l], lam_k1[l], lam_q2[l], lam_k2[l]), diff_norm[l], dn_conv[l], dn_a_log[l],
                    dn_dt_bias[l], dn_norm[l], fox_bias[l], ffn_norm[l])
    out = _rmsnorm(xf, final_norm, x.dtype)
    return out.reshape(batch, seq_len, d_model)
```

```python
import functools
import math

import jax
import jax.numpy as jnp
from jax import lax
from jax.experimental import pallas as pl
from jax.experimental.pallas import tpu as pltpu

_F32 = jnp.float32
_BF16 = jnp.bfloat16
_EPS = 1e-6
_NEG_INF = float("-inf")

_V7X_VMEM_BYTES = 64 * 1024 * 1024
_V7X_LANES = 128
_V7X_SUBLANES = 8
_BF16_ROWS = 16
_DN_CHUNK = 128
_FF_ALIGN = 1024


def _tile(n, target, align):
    t = min(target, n)
    t -= t % align
    while t >= align:
        if n % t == 0:
            return t
        t -= align
    return n


def _params(semantics, vmem_bytes):
    limit = min(int(vmem_bytes) + (8 << 20), _V7X_VMEM_BYTES - (4 << 20))
    return pltpu.CompilerParams(dimension_semantics=semantics, vmem_limit_bytes=limit)


def _silu(x):
    return x * (1.0 / (1.0 + jnp.exp(-x)))


def _softplus(x):
    return jnp.maximum(x, 0.0) + jnp.log1p(jnp.exp(-jnp.abs(x)))


def _shift_rows(x, prev8, s):
    xs = pltpu.roll(x, s, 0)
    ps = pltpu.roll(prev8, s, 0)
    row = lax.broadcasted_iota(jnp.int32, prev8.shape, 0)
    head = jnp.where(row < s, ps, xs[:_V7X_SUBLANES])
    return jnp.concatenate([head, xs[_V7X_SUBLANES:]], axis=0)


def _causal_conv(x, prev8, w):
    k = w.shape[0]
    out = x * w[k - 1:k]
    for j in range(k - 1):
        out = out + _shift_rows(x, prev8, k - 1 - j) * w[j:j + 1]
    return out


def _rmsnorm_kernel(x_ref, g_ref, o_ref):
    x = x_ref[...]
    y = x * lax.rsqrt(jnp.mean(x * x, axis=-1, keepdims=True) + _EPS)
    o_ref[...] = (y * g_ref[...]).astype(o_ref.dtype)


def _rmsnorm(x, g, out_dtype):
    n, d = x.shape
    tm = _tile(n, 512, _BF16_ROWS)
    vmem = 2 * tm * d * (4 + jnp.dtype(out_dtype).itemsize) + 2 * d * 4
    return pl.pallas_call(
        _rmsnorm_kernel,
        grid=(n // tm,),
        in_specs=[pl.BlockSpec((tm, d), lambda i: (i, 0)), pl.BlockSpec((1, d), lambda i: (0, 0))],
        out_specs=pl.BlockSpec((tm, d), lambda i: (i, 0)),
        out_shape=jax.ShapeDtypeStruct((n, d), out_dtype),
        compiler_params=_params(("parallel",), vmem),
        name="rmsnorm",
    )(x, g.reshape(1, d).astype(_F32))


def _mm_kernel(*refs, n_pairs, has_res):
    a_refs, w_refs = refs[:n_pairs], refs[n_pairs:2 * n_pairs]
    o_ref = refs[-1]
    acc = jnp.dot(a_refs[0][...], w_refs[0][...], preferred_element_type=_F32)
    for a_ref, w_ref in zip(a_refs[1:], w_refs[1:]):
        acc = acc + jnp.dot(a_ref[...], w_ref[...], preferred_element_type=_F32)
    if has_res:
        acc = acc + refs[2 * n_pairs][...]
    o_ref[...] = acc.astype(o_ref.dtype)


def _matmul(a_list, w, layer, out_dtype, residual=None, tm_target=1024, tn_target=1024, name="matmul"):
    m, n = a_list[0].shape[0], w.shape[2]
    kb = a_list[0].shape[1]
    assert all(a.shape[1] == kb for a in a_list) and w.shape[1] == kb * len(a_list)
    tm = _tile(m, tm_target, _BF16_ROWS)
    tn = _tile(n, tn_target, _V7X_LANES)
    ktot = w.shape[1]
    in_specs = [pl.BlockSpec((tm, kb), lambda i, j: (i, 0)) for _ in a_list]
    in_specs += [pl.BlockSpec((None, kb, tn), functools.partial(lambda i, j, r: (layer, r, j), r=r))
                 for r in range(len(a_list))]
    args = list(a_list) + [w] * len(a_list)
    vmem = 2 * 2 * ktot * (tm + tn) + 2 * tm * tn * jnp.dtype(out_dtype).itemsize + tm * tn * 4
    if residual is not None:
        in_specs.append(pl.BlockSpec((tm, tn), lambda i, j: (i, j)))
        args.append(residual)
        vmem += 2 * tm * tn * 4
    return pl.pallas_call(
        functools.partial(_mm_kernel, n_pairs=len(a_list), has_res=residual is not None),
        grid=(m // tm, n // tn),
        in_specs=in_specs,
        out_specs=pl.BlockSpec((tm, tn), lambda i, j: (i, j)),
        out_shape=jax.ShapeDtypeStruct((m, n), out_dtype),
        compiler_params=_params(("parallel", "parallel"), vmem),
        name=name,
    )(*args)


def _mm_acc_kernel(a_ref, w_ref, r_ref, o_ref, acc_ref, *, nk):
    k = pl.program_id(2)
    prod = lambda: jnp.dot(a_ref[...], w_ref[...], preferred_element_type=_F32)
    if nk == 1:
        o_ref[...] = r_ref[...] + prod()
        return

    @pl.when(k == 0)
    def _():
        acc_ref[...] = r_ref[...] + prod()

    @pl.when(jnp.logical_and(k > 0, k < nk - 1))
    def _():
        acc_ref[...] += prod()

    @pl.when(k == nk - 1)
    def _():
        o_ref[...] = acc_ref[...] + prod()


def _matmul_acc(a, w, layer, residual, tm_target=1024, tn_target=1024, tk_target=2816, name="matmul_acc"):
    m, kdim = a.shape
    n = w.shape[2]
    tm = _tile(m, tm_target, _BF16_ROWS)
    tn = _tile(n, tn_target, _V7X_LANES)
    tk = _tile(kdim, tk_target, _V7X_LANES)
    vmem = 2 * 2 * tk * (tm + tn) + 5 * tm * tn * 4
    return pl.pallas_call(
        functools.partial(_mm_acc_kernel, nk=kdim // tk),
        grid=(m // tm, n // tn, kdim // tk),
        in_specs=[pl.BlockSpec((tm, tk), lambda i, j, k: (i, k)),
                  pl.BlockSpec((None, tk, tn), lambda i, j, k: (layer, k, j)),
                  pl.BlockSpec((tm, tn), lambda i, j, k: (i, j))],
        out_specs=pl.BlockSpec((tm, tn), lambda i, j, k: (i, j)),
        out_shape=jax.ShapeDtypeStruct((m, n), _F32),
        scratch_shapes=[pltpu.VMEM((tm, tn), _F32)],
        compiler_params=_params(("parallel", "parallel", "arbitrary"), vmem),
        name=name,
    )(a, w, residual)


def _sconv_kernel(h_ref, c_ref, b_ref, hh_ref, ch_ref, w_ref, o_ref, *, blocks_per_seq):
    i = pl.program_id(0)
    x = c_ref[...].astype(_F32) * h_ref[...].astype(_F32)
    halo = ch_ref[...].astype(_F32) * hh_ref[...].astype(_F32)
    prev8 = jnp.where(i % blocks_per_seq == 0, 0.0, halo[_BF16_ROWS - _V7X_SUBLANES:])
    y = b_ref[...].astype(_F32) * _causal_conv(x, prev8, w_ref[...])
    o_ref[...] = y.astype(o_ref.dtype)


def _short_conv(p, conv_w, seq_len, col_h, col_c, col_b, width):
    n = p.shape[0]
    tm = _tile(seq_len, 512, _BF16_ROWS)
    hb = tm // _BF16_ROWS
    cur = lambda c: pl.BlockSpec((tm, width), lambda i: (i, c))
    halo = lambda c: pl.BlockSpec((_BF16_ROWS, width), lambda i: (jnp.maximum(i * hb - 1, 0), c))
    k = conv_w.shape[0]
    vmem = 2 * (4 * tm + 2 * _BF16_ROWS) * width * 2 + 8 * tm * width * 4
    return pl.pallas_call(
        functools.partial(_sconv_kernel, blocks_per_seq=seq_len // tm),
        grid=(n // tm,),
        in_specs=[cur(col_h), cur(col_c), cur(col_b), halo(col_h), halo(col_c),
                  pl.BlockSpec((k, width), lambda i: (0, 0))],
        out_specs=pl.BlockSpec((tm, width), lambda i: (i, 0)),
        out_shape=jax.ShapeDtypeStruct((n, width), _BF16),
        compiler_params=_params(("parallel",), vmem),
        name="short_conv",
    )(p, p, p, p, p, conv_w.astype(_F32))


_LOG2E = math.log2(math.e)


def _lane_tiles(x, width):
    return [x[:, c:c + _V7X_LANES] for c in range(0, width, _V7X_LANES)]


def _lane_repeat(x, width):
    return jnp.concatenate([x] * (width // _V7X_LANES), axis=1)


def _softmax_block(s, m_prev):
    m_new = jnp.maximum(m_prev, jnp.max(s, axis=-1, keepdims=True))
    alpha = jnp.exp2(m_prev - m_new)
    p = jnp.concatenate([jnp.exp2(t - m_new) for t in _lane_tiles(s, s.shape[1])], axis=1)
    return m_new, alpha, p


def _causal_mask(s, q0, k0):
    row = q0 + lax.broadcasted_iota(jnp.int32, s.shape, 0)
    col = k0 + lax.broadcasted_iota(jnp.int32, s.shape, 1)
    return jnp.where(col <= row, s, _NEG_INF)


def _qk(q, k):
    return lax.dot_general(q, k, (((1,), (1,)), ((), ())), preferred_element_type=_F32)


def _flash_blocks(seq_len):
    bq = _tile(seq_len, 1024, _V7X_LANES)
    bk = bq // 2 if bq % (2 * _V7X_LANES) == 0 else bq
    return bq, bk


def _resident_spec(seq_len, width, col):
    return pl.BlockSpec((seq_len, width), lambda b, h, i: (b, col // width + h), pipeline_mode=pl.Buffered(1))


def _kv_sweep(qi, r, bk, step):
    def trip(i, carry):
        for d in range(r):
            step(i * r + d, False, 0)
        return carry

    lax.fori_loop(0, qi, trip, 0)
    for d in range(r):
        step(qi * r + d, True, d * bk)


def _diff_kernel(q_ref, k_ref, v_ref, lq1_ref, lk1_ref, lq2_ref, lk2_ref, g_ref, o_ref,
                 m_ref, l_ref, acc_ref, *, bq, bk, d, hp, scale, lam_init):
    qi = pl.program_id(2)
    hw = 2 * d
    m_ref[...] = jnp.full(m_ref.shape, _NEG_INF, _F32)
    l_ref[...] = jnp.zeros(l_ref.shape, _F32)
    acc_ref[...] = jnp.zeros(acc_ref.shape, _F32)
    q = (q_ref[...].astype(_F32) * (scale * _LOG2E)).astype(_BF16)

    def step(j, masked, row0):
        k0 = pl.multiple_of(j * bk, bk)
        k = k_ref[pl.ds(k0, bk), :]
        v = v_ref[pl.ds(k0, bk), :]
        rows = slice(row0, bq)
        for c in range(2 * hp):
            cols = slice(c * d, (c + 1) * d)
            s = _qk(q[rows, cols], k[:, cols])
            if masked:
                s = _causal_mask(s, qi * bq + row0, k0)
            m_new, alpha, p = _softmax_block(s, m_ref[c, rows])
            l_ref[c, rows] = alpha * l_ref[c, rows] + jnp.sum(p, axis=-1, keepdims=True)
            vh = v[:, (c // 2) * hw:(c // 2 + 1) * hw]
            acc_ref[c, rows] = (_lane_repeat(alpha, hw) * acc_ref[c, rows]
                                + jnp.dot(p.astype(_BF16), vh, preferred_element_type=_F32))
            m_ref[c, rows] = m_new

    _kv_sweep(qi, bq // bk, bk, step)
    lam = (jnp.exp(jnp.sum(lq1_ref[...] * lk1_ref[...], axis=-1, keepdims=True))
           - jnp.exp(jnp.sum(lq2_ref[...] * lk2_ref[...], axis=-1, keepdims=True)) + lam_init)
    for h in range(hp):
        o = (acc_ref[2 * h] / _lane_repeat(l_ref[2 * h], hw)
             - lam * (acc_ref[2 * h + 1] / _lane_repeat(l_ref[2 * h + 1], hw)))
        o = o * lax.rsqrt(jnp.mean(o * o, axis=-1, keepdims=True) + _EPS) * g_ref[...]
        o_ref[:, h * hw:(h + 1) * hw] = (o * (1.0 - lam_init)).astype(o_ref.dtype)


def _diff_attention(p, lam_vecs, norm_g, lam_init, batch, seq_len, col_q, col_k, col_v, heads, d):
    n = p.shape[0]
    hw = 2 * d
    bq, bk = _flash_blocks(seq_len)
    nq = seq_len // bq
    hp = 2 if heads % 2 == 0 else 1
    w = hp * hw
    vec = pl.BlockSpec((1, d), lambda b, h, i: (0, 0))
    vmem = 2 * seq_len * w * 2 + 4 * bq * w * 2 + 2 * hp * bq * (hw + 2 * _V7X_LANES) * 4 + 12 * bq * bk * 4
    return pl.pallas_call(
        functools.partial(_diff_kernel, bq=bq, bk=bk, d=d, hp=hp, scale=d ** -0.5, lam_init=lam_init),
        grid=(batch, heads // hp, nq),
        in_specs=[pl.BlockSpec((bq, w), lambda b, h, i: (b * nq + i, col_q // w + h)),
                  _resident_spec(seq_len, w, col_k), _resident_spec(seq_len, w, col_v),
                  vec, vec, vec, vec,
                  pl.BlockSpec((1, hw), lambda b, h, i: (0, 0))],
        out_specs=pl.BlockSpec((bq, w), lambda b, h, i: (b * nq + i, h)),
        out_shape=jax.ShapeDtypeStruct((n, heads * hw), _BF16),
        scratch_shapes=[pltpu.VMEM((2 * hp, bq, _V7X_LANES), _F32), pltpu.VMEM((2 * hp, bq, _V7X_LANES), _F32),
                        pltpu.VMEM((2 * hp, bq, hw), _F32)],
        compiler_params=_params(("parallel", "parallel", "arbitrary"), vmem),
        name="diff_attention",
    )(p, p, p, *[v.reshape(1, d).astype(_F32) for v in lam_vecs], norm_g.reshape(1, hw).astype(_F32))


def _fox_cum_kernel(f_ref, bias_ref, o_ref):
    x = f_ref[...] + bias_ref[...]
    ls = jnp.minimum(x, 0.0) - jnp.log1p(jnp.exp(-jnp.abs(x)))
    r = lax.broadcasted_iota(jnp.int32, (_V7X_LANES, _V7X_LANES), 0)
    c = lax.broadcasted_iota(jnp.int32, (_V7X_LANES, _V7X_LANES), 1)
    upper = (r <= c).astype(_F32)
    carry = jnp.zeros((x.shape[0], 1), _F32)
    for t in range(x.shape[1] // _V7X_LANES):
        sl = slice(t * _V7X_LANES, (t + 1) * _V7X_LANES)
        w = jnp.dot(ls[:, sl], upper, precision=lax.Precision.HIGHEST, preferred_element_type=_F32) + carry
        o_ref[:, sl] = w
        carry = w[:, _V7X_LANES - 1:]


def _fox_cumsum(f_rows, bias_rows):
    return pl.pallas_call(
        _fox_cum_kernel,
        out_shape=jax.ShapeDtypeStruct(f_rows.shape, _F32),
        name="fox_cumsum",
    )(f_rows, bias_rows)


def _fox_kernel(q_ref, k_ref, v_ref, cq_ref, ck_ref, o_ref, m_ref, acc_ref, *, bq, bk, d, hp, scale):
    qi = pl.program_id(2)
    m_ref[...] = jnp.full(m_ref.shape, _NEG_INF, _F32)
    acc_ref[...] = jnp.zeros(acc_ref.shape, _F32)
    q = (q_ref[...].astype(_F32) * (scale * _LOG2E)).astype(_BF16)
    cq = [jnp.broadcast_to(cq_ref[0, h] * _LOG2E, (bq, _V7X_LANES)) for h in range(hp)]
    ones = jnp.ones((bk, _V7X_LANES), _BF16)

    def step(j, masked, row0):
        k0 = pl.multiple_of(j * bk, bk)
        rows = slice(row0, bq)
        k = k_ref[pl.ds(k0, bk), :]
        v = v_ref[pl.ds(k0, bk), :]
        for h in range(hp):
            cols = slice(h * d, (h + 1) * d)
            s = _qk(q[rows, cols], k[:, cols])
            ck = ck_ref[0, h, pl.ds(j, 1), :] * _LOG2E
            s = jnp.concatenate([st + (cq[h][rows] - ct)
                                 for st, ct in zip(_lane_tiles(s, bk), _lane_tiles(ck, bk))], axis=1)
            if masked:
                s = _causal_mask(s, qi * bq + row0, k0)
            m_new, alpha, p = _softmax_block(s, m_ref[h, rows])
            v_ones = jnp.concatenate([v[:, cols], ones], axis=1)
            acc_ref[h, rows] = (_lane_repeat(alpha, d + _V7X_LANES) * acc_ref[h, rows]
                                + jnp.dot(p.astype(_BF16), v_ones, preferred_element_type=_F32))
            m_ref[h, rows] = m_new

    _kv_sweep(qi, bq // bk, bk, step)
    for h in range(hp):
        acc = acc_ref[h]
        o_ref[:, h * d:(h + 1) * d] = (acc[:, :d] / _lane_repeat(acc[:, d:], d)).astype(o_ref.dtype)


def _fox_attention(p, cum, batch, seq_len, col_q, col_k, col_v, heads, d):
    n = p.shape[0]
    bq, bk = _flash_blocks(seq_len)
    nq, nk = seq_len // bq, seq_len // bk
    hp = 4 if heads % 4 == 0 else 1
    w = hp * d
    cum_rows = cum.reshape(batch, heads, nk, bk)
    cum_cols = cum.reshape(batch, heads, seq_len, 1)
    vmem = (2 * seq_len * w * 2 + 4 * bq * w * 2 + 2 * hp * seq_len * 4 + 2 * hp * bq * _V7X_LANES * 4
            + hp * bq * (d + 2 * _V7X_LANES) * 4 + 12 * bq * bk * 4)
    return pl.pallas_call(
        functools.partial(_fox_kernel, bq=bq, bk=bk, d=d, hp=hp, scale=d ** -0.5),
        grid=(batch, heads // hp, nq),
        in_specs=[pl.BlockSpec((bq, w), lambda b, h, i: (b * nq + i, col_q // w + h)),
                  _resident_spec(seq_len, w, col_k), _resident_spec(seq_len, w, col_v),
                  pl.BlockSpec((1, hp, bq, 1), lambda b, h, i: (b, h, i, 0)),
                  pl.BlockSpec((1, hp, nk, bk), lambda b, h, i: (b, h, 0, 0))],
        out_specs=pl.BlockSpec((bq, w), lambda b, h, i: (b * nq + i, h)),
        out_shape=jax.ShapeDtypeStruct((n, heads * d), _BF16),
        scratch_shapes=[pltpu.VMEM((hp, bq, _V7X_LANES), _F32), pltpu.VMEM((hp, bq, d + _V7X_LANES), _F32)],
        compiler_params=_params(("parallel", "parallel", "arbitrary"), vmem),
        name="fox_attention",
    )(p, p, p, cum_cols, cum_rows)


def _lane_column(x, h):
    lane = lax.broadcasted_iota(jnp.int32, x.shape, 1)
    return jnp.sum(jnp.where(lane == h, x, 0.0), axis=-1, keepdims=True)


def _bdot(a, b):
    return jnp.dot(a.astype(_BF16), b.astype(_BF16), preferred_element_type=_F32)


_INV_BASE_LOG2 = 4


def _unit_lower_inverses(mats, ri, ci):
    c = mats[0].shape[0]
    same_block = lambda log2n: (ri >> log2n) == (ci >> log2n)
    eye = (ri == ci).astype(_F32)
    base = same_block(_INV_BASE_LOG2)
    pw = [jnp.where(base, -a, 0.0) for a in mats]
    tinv = [eye + x for x in pw]
    for _ in range(_INV_BASE_LOG2 - 1):
        pw = [_bdot(x, x) for x in pw]
        tinv = [t + _bdot(t, x) for t, x in zip(tinv, pw)]
    log2n = _INV_BASE_LOG2
    while (1 << log2n) < c:
        level = same_block(log2n + 1) & jnp.logical_not(same_block(log2n))
        off = [_bdot(t, jnp.where(level, a, 0.0)) for t, a in zip(tinv, mats)]
        tinv = [t - _bdot(o, t) for t, o in zip(tinv, off)]
        log2n += 1
    return tinv


def _deltanet_kernel(q_ref, k_ref, v_ref, z_ref, gc_ref, gr_ref, cw_ref, alog_c_ref, dtb_c_ref,
                     alog_r_ref, dtb_r_ref, ng_ref, o_ref, s_ref, tail_ref, *, heads, d, c):
    t = pl.program_id(1)
    width = heads * d

    @pl.when(t == 0)
    def _():
        s_ref[...] = jnp.zeros(s_ref.shape, _F32)
        tail_ref[...] = jnp.zeros(tail_ref.shape, _F32)

    cw = cw_ref[...]
    conv = []
    for idx, ref in enumerate((q_ref, k_ref, v_ref)):
        raw = ref[...].astype(_F32)
        cols = slice(idx * width, (idx + 1) * width)
        conv.append(_silu(_causal_conv(raw, tail_ref[:, cols], cw[:, cols])))
        tail_ref[:, cols] = raw[c - _V7X_SUBLANES:]

    gates_c = gc_ref[...]
    g_cols = -jnp.exp(alog_c_ref[...]) * _softplus(gates_c + dtb_c_ref[...])
    beta_cols = 1.0 / (1.0 + jnp.exp(-gates_c))
    gates_r = gr_ref[0]
    g_rows = -jnp.exp(alog_r_ref[...]) * _softplus(gates_r[:heads] + dtb_r_ref[...])
    ri = lax.broadcasted_iota(jnp.int32, (c, c), 0)
    ci = lax.broadcasted_iota(jnp.int32, (c, c), 1)
    incl = ri >= ci
    strict = ri > ci
    hp = lax.Precision.HIGHEST
    gcum_cols = jnp.dot(incl.astype(_F32), g_cols, precision=hp, preferred_element_type=_F32)
    gcum_rows = jnp.dot(g_rows, (ri <= ci).astype(_F32), precision=hp, preferred_element_type=_F32)

    hr = range(heads)
    cols = [slice(h * d, (h + 1) * d) for h in hr]
    unit = lambda x: x * lax.rsqrt(jnp.sum(x * x, axis=-1, keepdims=True) + _EPS)
    q = [unit(conv[0][:, s]) * (d ** -0.5) for s in cols]
    k = [unit(conv[1][:, s]) for s in cols]
    v = [conv[2][:, s] for s in cols]
    gc = [_lane_column(gcum_cols, h) for h in hr]
    beta = [_lane_column(beta_cols, heads + h) for h in hr]
    decay = [jnp.exp(jnp.where(incl, gc[h] - gcum_rows[h:h + 1], _NEG_INF)) for h in hr]
    k16 = [x.astype(_BF16) for x in k]
    kb = [k[h] * beta[h] for h in hr]
    a = [jnp.where(strict, _qk(kb[h].astype(_BF16), k16[h]) * decay[h], 0.0) for h in hr]
    tinv = _unit_lower_inverses(a, ri, ci)
    egc = [jnp.exp(x) for x in gc]
    wu = [_bdot(tinv[h], jnp.concatenate([kb[h] * egc[h], v[h] * beta[h]], axis=1)) for h in hr]
    aqk = [_qk(q[h].astype(_BF16), k16[h]) * decay[h] for h in hr]
    s = [s_ref[h] for h in hr]
    ws_qs = [_bdot(jnp.concatenate([wu[h][:, :d], q[h] * egc[h]], axis=0), s[h]) for h in hr]
    v_new = [wu[h][:, d:] - ws_qs[h][:c] for h in hr]
    o = [ws_qs[h][c:] + _bdot(aqk[h], v_new[h]) for h in hr]
    g_last = [x[c - 1:c] for x in gc]
    k_dec = [k[h] * jnp.exp(g_last[h] - gc[h]) for h in hr]
    for h in hr:
        s_ref[h] = s[h] * jnp.exp(g_last[h]) + lax.dot_general(
            k_dec[h].astype(_BF16), v_new[h].astype(_BF16), (((0,), (0,)), ((), ())), preferred_element_type=_F32)
    ng = ng_ref[...]
    for h in hr:
        on = o[h] * lax.rsqrt(jnp.mean(o[h] * o[h], axis=-1, keepdims=True) + _EPS) * ng
        o_ref[:, cols[h]] = (on * _silu(z_ref[:, cols[h]].astype(_F32))).astype(o_ref.dtype)


def _gated_deltanet(p, gates_cols, gates_rows, conv_w, a_log, dt_bias, norm_g, batch, seq_len,
                    col_q, col_k, col_v, col_z, heads, d):
    n = p.shape[0]
    c = _tile(seq_len, _DN_CHUNK, _BF16_ROWS)
    nt = seq_len // c
    width = heads * d
    lanes = gates_cols.shape[1]
    kw = conv_w.shape[0]
    col = lambda off: pl.BlockSpec((c, width), lambda b, t: (b * nt + t, off // width))
    const = lambda shape: pl.BlockSpec(shape, lambda b, t: (0,) * len(shape))
    pad_lane = lambda v: jnp.zeros((1, lanes), _F32).at[0, :heads].set(v.astype(_F32))
    vmem = 2 * 5 * c * width * 2 + 12 * c * width * 4 + heads * d * d * 4 + 2 * kw * 3 * width * 4
    return pl.pallas_call(
        functools.partial(_deltanet_kernel, heads=heads, d=d, c=c),
        grid=(batch, nt),
        in_specs=[col(col_q), col(col_k), col(col_v), col(col_z),
                  pl.BlockSpec((c, lanes), lambda b, t: (b * nt + t, 0)),
                  pl.BlockSpec((1, gates_rows.shape[1], c), lambda b, t: (b, 0, t)),
                  const((kw, 3 * width)), const((1, lanes)), const((1, lanes)),
                  const((heads, 1)), const((heads, 1)), const((1, d))],
        out_specs=pl.BlockSpec((c, width), lambda b, t: (b * nt + t, 0)),
        out_shape=jax.ShapeDtypeStruct((n, width), _BF16),
        scratch_shapes=[pltpu.VMEM((heads, d, d), _F32), pltpu.VMEM((_V7X_SUBLANES, 3 * width), _F32)],
        compiler_params=_params(("parallel", "arbitrary"), vmem),
        name="gated_deltanet",
    )(p, p, p, p, gates_cols, gates_rows, conv_w.astype(_F32), pad_lane(a_log), pad_lane(dt_bias),
      a_log.reshape(heads, 1).astype(_F32), dt_bias.reshape(heads, 1).astype(_F32),
      norm_g.reshape(1, d).astype(_F32))


def _ffn_in_kernel(h_ref, wg_ref, wu_ref, cw_ref, o_ref, tail_ref, *, blocks_per_seq, parts):
    i, j = pl.program_id(0), pl.program_id(1)
    tm, tn = o_ref.shape

    @pl.when(i % blocks_per_seq == 0)
    def _():
        tail_ref[j] = jnp.zeros(tail_ref.shape[1:], _F32)

    h = h_ref[...]
    for c in range(parts):
        cols = slice(c * (tn // parts), (c + 1) * (tn // parts))
        g = jnp.dot(h, wg_ref[:, cols], preferred_element_type=_F32)
        u = jnp.dot(h, wu_ref[:, cols], preferred_element_type=_F32)
        a = _causal_conv(g, tail_ref[j, :, cols], cw_ref[:, cols])
        tail_ref[j, :, cols] = g[tm - _V7X_SUBLANES:]
        o_ref[:, cols] = (_silu(a) * u).astype(o_ref.dtype)


def _ffn_in(h, w_gate, w_up, conv_w, layer, seq_len, tm_target=1024, tn_target=512):
    n, d = h.shape
    ff = w_gate.shape[2]
    tm = _tile(seq_len, tm_target, _BF16_ROWS)
    tn = _tile(ff, tn_target, _V7X_LANES)
    parts = 2 if tn % (2 * _V7X_LANES) == 0 else 1
    kw = conv_w.shape[1]
    wspec = pl.BlockSpec((None, d, tn), lambda i, j: (layer, 0, j))
    vmem = 2 * 2 * d * (tm + 2 * tn) + 2 * tm * tn * 2 + 6 * tm * tn * 4 + ff * _V7X_SUBLANES * 4
    return pl.pallas_call(
        functools.partial(_ffn_in_kernel, blocks_per_seq=seq_len // tm, parts=parts),
        grid=(n // tm, ff // tn),
        in_specs=[pl.BlockSpec((tm, d), lambda i, j: (i, 0)), wspec, wspec,
                  pl.BlockSpec((None, kw, tn), lambda i, j: (layer, 0, j))],
        out_specs=pl.BlockSpec((tm, tn), lambda i, j: (i, j)),
        out_shape=jax.ShapeDtypeStruct((n, ff), _BF16),
        scratch_shapes=[pltpu.VMEM((ff // tn, _V7X_SUBLANES, tn), _F32)],
        compiler_params=_params(("arbitrary", "arbitrary"), vmem),
        name="ffn_in",
    )(h, w_gate, w_up, conv_w)


def _cast_pad_kernel(x_ref, o_ref, *, rows, cols):
    i, j = pl.program_id(1), pl.program_id(2)
    tr, tc = o_ref.shape
    r = i * tr + lax.broadcasted_iota(jnp.int32, (tr, tc), 0)
    c = j * tc + lax.broadcasted_iota(jnp.int32, (tr, tc), 1)
    o_ref[...] = jnp.where(jnp.logical_and(r < rows, c < cols), x_ref[...], 0.0).astype(o_ref.dtype)


def _cast_pad(w, rows_to, cols_to, dtype):
    layers, rows, cols = w.shape
    tr = _tile(rows_to, 1024, _BF16_ROWS)
    tc = _tile(cols_to, 1024, _V7X_LANES)
    assert rows_to - rows < tr and cols_to - cols < tc
    return pl.pallas_call(
        functools.partial(_cast_pad_kernel, rows=rows, cols=cols),
        grid=(layers, rows_to // tr, cols_to // tc),
        in_specs=[pl.BlockSpec((None, tr, tc), lambda l, i, j: (l, i, j))],
        out_specs=pl.BlockSpec((None, tr, tc), lambda l, i, j: (l, i, j)),
        out_shape=jax.ShapeDtypeStruct((layers, rows_to, cols_to), dtype),
        compiler_params=_params(("parallel", "parallel", "parallel"), 2 * tr * tc * 6),
        name="cast_pad",
    )(w)


def _in_proj_split_kernel(w_ref, wide_ref, gates_ref, *, o_gate1, o_wide2, o_gate2):
    w = w_ref[...]
    wide_ref[:, :o_gate1] = w[:, :o_gate1].astype(wide_ref.dtype)
    wide_ref[:, o_gate1:] = w[:, o_wide2:o_gate2].astype(wide_ref.dtype)
    narrow = jnp.concatenate([w[:, o_gate1:o_wide2], w[:, o_gate2:]], axis=1)
    fill = jnp.zeros((w.shape[0], gates_ref.shape[1] - narrow.shape[1]), w.dtype)
    gates_ref[...] = jnp.concatenate([narrow, fill], axis=1).astype(gates_ref.dtype)


def _in_proj_split(w_in, o_gate1, o_wide2, o_gate2, dtype):
    layers, rows, cols = w_in.shape
    n_wide = o_gate1 + o_gate2 - o_wide2
    tr = _tile(rows, 128, _BF16_ROWS)
    return pl.pallas_call(
        functools.partial(_in_proj_split_kernel, o_gate1=o_gate1, o_wide2=o_wide2, o_gate2=o_gate2),
        grid=(layers, rows // tr),
        in_specs=[pl.BlockSpec((None, tr, cols), lambda l, i: (l, i, 0))],
        out_specs=[pl.BlockSpec((None, tr, n_wide), lambda l, i: (l, i, 0)),
                   pl.BlockSpec((None, tr, _V7X_LANES), lambda l, i: (l, i, 0))],
        out_shape=[jax.ShapeDtypeStruct((layers, rows, n_wide), dtype),
                   jax.ShapeDtypeStruct((layers, rows, _V7X_LANES), dtype)],
        compiler_params=_params(("parallel", "parallel"), 2 * tr * cols * 6 + tr * cols * 6),
        name="in_proj_split",
    )(w_in)


def _pad_last(w, total):
    return jnp.pad(w, ((0, 0),) * (w.ndim - 1) + ((0, total - w.shape[-1]),))


def _bf16_weights(w_in, w_out, w_gate, w_up, ffn_conv, w_down, gw, dn_heads):
    o_dn_a = 10 * gw
    o_fx = o_dn_a + 2 * dn_heads
    o_fx_f = o_fx + 3 * gw
    wide, gates = _in_proj_split(w_in, o_dn_a, o_fx, o_fx_f, _BF16)
    ff = w_gate.shape[-1]
    ffp = -(-ff // _FF_ALIGN) * _FF_ALIGN if ff > _FF_ALIGN else ff
    return dict(
        wide=wide, gates=gates, out=w_out.astype(_BF16),
        gate=_cast_pad(w_gate, w_gate.shape[1], ffp, _BF16), up=_cast_pad(w_up, w_up.shape[1], ffp, _BF16),
        conv=_pad_last(ffn_conv.astype(_F32), ffp),
        down=_cast_pad(w_down, ffp, w_down.shape[2], _BF16))


def _layer(x, layer, batch, seq_len, wts, attn_norm, sc_conv, lam_vecs, diff_norm, dn_conv, dn_a_log,
           dn_dt_bias, dn_norm, fox_bias, ffn_norm):
    d = lam_vecs[0].shape[-1]
    gw = wts["out"].shape[1] // 4
    dn_heads, fox_heads = dn_a_log.shape[-1], fox_bias.shape[-1]
    diff_heads = gw // diff_norm.shape[-1]
    col = lambda idx: idx * gw

    h = _rmsnorm(x, attn_norm, _BF16)
    p = _matmul([h], wts["wide"], layer, _BF16, name="in_proj")
    gates = _matmul([h], wts["gates"], layer, _F32, name="in_proj_gates")

    y_sc = _short_conv(p, sc_conv, seq_len, 0, 1, 2, gw)
    lam_init = 0.8 - 0.6 * math.exp(-0.3 * layer)
    y_df = _diff_attention(p, lam_vecs, diff_norm, lam_init, batch, seq_len, col(3), col(4), col(5), diff_heads, d)

    gates_t = gates.reshape(batch, seq_len, _V7X_LANES).transpose(0, 2, 1)
    fox_rows = gates_t[:, 2 * dn_heads:2 * dn_heads + fox_heads].reshape(batch * fox_heads, seq_len)
    fox_bias_rows = jnp.tile(fox_bias.astype(_F32), batch).reshape(batch * fox_heads, 1)
    cum = _fox_cumsum(fox_rows, fox_bias_rows)
    y_fx = _fox_attention(p, cum, batch, seq_len, col(10), col(11), col(12), fox_heads, d)

    y_dn = _gated_deltanet(p, gates, gates_t[:, :2 * dn_heads], dn_conv, dn_a_log, dn_dt_bias, dn_norm,
                           batch, seq_len, col(6), col(7), col(8), col(9), dn_heads, d)

    x = _matmul([y_sc, y_df, y_dn, y_fx], wts["out"], layer, _F32, residual=x, tn_target=512, name="out_proj")

    h2 = _rmsnorm(x, ffn_norm, _BF16)
    act = _ffn_in(h2, wts["gate"], wts["up"], wts["conv"], layer, seq_len)
    return _matmul_acc(act, wts["down"], layer, x, name="ffn_out")


def kernel(x, attn_norm, w_in, sc_conv, lam_q1, lam_k1, lam_q2, lam_k2, diff_norm, dn_conv, dn_a_log,
           dn_dt_bias, dn_norm, fox_bias, w_out, ffn_norm, w_gate, w_up, ffn_conv, w_down, final_norm):
    batch, seq_len, d_model = x.shape
    xf = x.reshape(batch * seq_len, d_model).astype(_F32)
    wts = _bf16_weights(w_in, w_out, w_gate, w_up, ffn_conv, w_down, w_out.shape[1] // 4, dn_a_log.shape[-1])
    for l in range(w_in.shape[0]):
        xf = _layer(xf, l, batch, seq_len, wts, attn_norm[l], sc_conv[l],
                    (lam_q1[l], lam_k1[l], lam_q2[l], lam_k2[l]), diff_norm[l], dn_conv[l], dn_a_log[l],
                    dn_dt_bias[l], dn_norm[l], fox_bias[l], ffn_norm[l])
    out = _rmsnorm(xf, final_norm, x.dtype)
    return out.reshape(batch, seq_len, d_model)
```

```python
import functools
import math

import jax
import jax.numpy as jnp
from jax import lax
from jax.experimental import pallas as pl
from jax.experimental.pallas import tpu as pltpu

_F32 = jnp.float32
_BF16 = jnp.bfloat16
_EPS = 1e-6
_NEG_INF = float("-inf")

_V7X_VMEM_BYTES = 64 * 1024 * 1024
_V7X_LANES = 128
_V7X_SUBLANES = 8
_BF16_ROWS = 16
_DN_CHUNK = 128
_FF_ALIGN = 1024


def _tile(n, target, align):
    t = min(target, n)
    t -= t % align
    while t >= align:
        if n % t == 0:
            return t
        t -= align
    return n


def _params(semantics, vmem_bytes):
    limit = min(int(vmem_bytes) + (8 << 20), _V7X_VMEM_BYTES - (4 << 20))
    return pltpu.CompilerParams(dimension_semantics=semantics, vmem_limit_bytes=limit)


def _silu(x):
    return x * (1.0 / (1.0 + jnp.exp(-x)))


def _softplus(x):
    return jnp.maximum(x, 0.0) + jnp.log1p(jnp.exp(-jnp.abs(x)))


def _shift_rows(x, prev8, s):
    xs = pltpu.roll(x, s, 0)
    ps = pltpu.roll(prev8, s, 0)
    row = lax.broadcasted_iota(jnp.int32, prev8.shape, 0)
    head = jnp.where(row < s, ps, xs[:_V7X_SUBLANES])
    return jnp.concatenate([head, xs[_V7X_SUBLANES:]], axis=0)


def _causal_conv(x, prev8, w):
    k = w.shape[0]
    out = x * w[k - 1:k]
    for j in range(k - 1):
        out = out + _shift_rows(x, prev8, k - 1 - j) * w[j:j + 1]
    return out


def _rmsnorm_kernel(x_ref, g_ref, o_ref):
    x = x_ref[...]
    y = x * lax.rsqrt(jnp.mean(x * x, axis=-1, keepdims=True) + _EPS)
    o_ref[...] = (y * g_ref[...]).astype(o_ref.dtype)


def _rmsnorm(x, g, out_dtype):
    n, d = x.shape
    tm = _tile(n, 512, _BF16_ROWS)
    vmem = 2 * tm * d * (4 + jnp.dtype(out_dtype).itemsize) + 2 * d * 4
    return pl.pallas_call(
        _rmsnorm_kernel,
        grid=(n // tm,),
        in_specs=[pl.BlockSpec((tm, d), lambda i: (i, 0)), pl.BlockSpec((1, d), lambda i: (0, 0))],
        out_specs=pl.BlockSpec((tm, d), lambda i: (i, 0)),
        out_shape=jax.ShapeDtypeStruct((n, d), out_dtype),
        compiler_params=_params(("parallel",), vmem),
        name="rmsnorm",
    )(x, g.reshape(1, d).astype(_F32))


def _mm_kernel(*refs, n_pairs, has_res):
    a_refs, w_refs = refs[:n_pairs], refs[n_pairs:2 * n_pairs]
    o_ref = refs[-1]
    acc = jnp.dot(a_refs[0][...], w_refs[0][...], preferred_element_type=_F32)
    for a_ref, w_ref in zip(a_refs[1:], w_refs[1:]):
        acc = acc + jnp.dot(a_ref[...], w_ref[...], preferred_element_type=_F32)
    if has_res:
        acc = acc + refs[2 * n_pairs][...]
    o_ref[...] = acc.astype(o_ref.dtype)


def _matmul(a_list, w, layer, out_dtype, residual=None, tm_target=1024, tn_target=1024, name="matmul"):
    m, n = a_list[0].shape[0], w.shape[2]
    kb = a_list[0].shape[1]
    assert all(a.shape[1] == kb for a in a_list) and w.shape[1] == kb * len(a_list)
    tm = _tile(m, tm_target, _BF16_ROWS)
    tn = _tile(n, tn_target, _V7X_LANES)
    ktot = w.shape[1]
    in_specs = [pl.BlockSpec((tm, kb), lambda i, j: (i, 0)) for _ in a_list]
    in_specs += [pl.BlockSpec((None, kb, tn), functools.partial(lambda i, j, r: (layer, r, j), r=r))
                 for r in range(len(a_list))]
    args = list(a_list) + [w] * len(a_list)
    vmem = 2 * 2 * ktot * (tm + tn) + 2 * tm * tn * jnp.dtype(out_dtype).itemsize + tm * tn * 4
    if residual is not None:
        in_specs.append(pl.BlockSpec((tm, tn), lambda i, j: (i, j)))
        args.append(residual)
        vmem += 2 * tm * tn * 4
    return pl.pallas_call(
        functools.partial(_mm_kernel, n_pairs=len(a_list), has_res=residual is not None),
        grid=(m // tm, n // tn),
        in_specs=in_specs,
        out_specs=pl.BlockSpec((tm, tn), lambda i, j: (i, j)),
        out_shape=jax.ShapeDtypeStruct((m, n), out_dtype),
        compiler_params=_params(("parallel", "parallel"), vmem),
        name=name,
    )(*args)


def _mm_acc_kernel(a_ref, w_ref, r_ref, o_ref, acc_ref, *, nk):
    k = pl.program_id(2)
    prod = lambda: jnp.dot(a_ref[...], w_ref[...], preferred_element_type=_F32)
    if nk == 1:
        o_ref[...] = r_ref[...] + prod()
        return

    @pl.when(k == 0)
    def _():
        acc_ref[...] = r_ref[...] + prod()

    @pl.when(jnp.logical_and(k > 0, k < nk - 1))
    def _():
        acc_ref[...] += prod()

    @pl.when(k == nk - 1)
    def _():
        o_ref[...] = acc_ref[...] + prod()


def _matmul_acc(a, w, layer, residual, tm_target=1024, tn_target=1024, tk_target=2816, name="matmul_acc"):
    m, kdim = a.shape
    n = w.shape[2]
    tm = _tile(m, tm_target, _BF16_ROWS)
    tn = _tile(n, tn_target, _V7X_LANES)
    tk = _tile(kdim, tk_target, _V7X_LANES)
    vmem = 2 * 2 * tk * (tm + tn) + 5 * tm * tn * 4
    return pl.pallas_call(
        functools.partial(_mm_acc_kernel, nk=kdim // tk),
        grid=(m // tm, n // tn, kdim // tk),
        in_specs=[pl.BlockSpec((tm, tk), lambda i, j, k: (i, k)),
                  pl.BlockSpec((None, tk, tn), lambda i, j, k: (layer, k, j)),
                  pl.BlockSpec((tm, tn), lambda i, j, k: (i, j))],
        out_specs=pl.BlockSpec((tm, tn), lambda i, j, k: (i, j)),
        out_shape=jax.ShapeDtypeStruct((m, n), _F32),
        scratch_shapes=[pltpu.VMEM((tm, tn), _F32)],
        compiler_params=_params(("parallel", "parallel", "arbitrary"), vmem),
        name=name,
    )(a, w, residual)


def _sconv_kernel(h_ref, c_ref, b_ref, hh_ref, ch_ref, w_ref, o_ref, *, blocks_per_seq):
    i = pl.program_id(0)
    x = c_ref[...].astype(_F32) * h_ref[...].astype(_F32)
    halo = ch_ref[...].astype(_F32) * hh_ref[...].astype(_F32)
    prev8 = jnp.where(i % blocks_per_seq == 0, 0.0, halo[_BF16_ROWS - _V7X_SUBLANES:])
    y = b_ref[...].astype(_F32) * _causal_conv(x, prev8, w_ref[...])
    o_ref[...] = y.astype(o_ref.dtype)


def _short_conv(p, conv_w, seq_len, col_h, col_c, col_b, width):
    n = p.shape[0]
    tm = _tile(seq_len, 512, _BF16_ROWS)
    hb = tm // _BF16_ROWS
    cur = lambda c: pl.BlockSpec((tm, width), lambda i: (i, c))
    halo = lambda c: pl.BlockSpec((_BF16_ROWS, width), lambda i: (jnp.maximum(i * hb - 1, 0), c))
    k = conv_w.shape[0]
    vmem = 2 * (4 * tm + 2 * _BF16_ROWS) * width * 2 + 8 * tm * width * 4
    return pl.pallas_call(
        functools.partial(_sconv_kernel, blocks_per_seq=seq_len // tm),
        grid=(n // tm,),
        in_specs=[cur(col_h), cur(col_c), cur(col_b), halo(col_h), halo(col_c),
                  pl.BlockSpec((k, width), lambda i: (0, 0))],
        out_specs=pl.BlockSpec((tm, width), lambda i: (i, 0)),
        out_shape=jax.ShapeDtypeStruct((n, width), _BF16),
        compiler_params=_params(("parallel",), vmem),
        name="short_conv",
    )(p, p, p, p, p, conv_w.astype(_F32))


_LOG2E = math.log2(math.e)


def _lane_tiles(x, width):
    return [x[:, c:c + _V7X_LANES] for c in range(0, width, _V7X_LANES)]


def _lane_repeat(x, width):
    return jnp.concatenate([x] * (width // _V7X_LANES), axis=1)


def _softmax_block(s, m_prev):
    m_new = jnp.maximum(m_prev, jnp.max(s, axis=-1, keepdims=True))
    alpha = jnp.exp2(m_prev - m_new)
    p = jnp.concatenate([jnp.exp2(t - m_new) for t in _lane_tiles(s, s.shape[1])], axis=1)
    return m_new, alpha, p


def _causal_mask(s, q0, k0):
    row = q0 + lax.broadcasted_iota(jnp.int32, s.shape, 0)
    col = k0 + lax.broadcasted_iota(jnp.int32, s.shape, 1)
    return jnp.where(col <= row, s, _NEG_INF)


def _qk(q, k):
    return lax.dot_general(q, k, (((1,), (1,)), ((), ())), preferred_element_type=_F32)


def _flash_blocks(seq_len):
    bq = _tile(seq_len, 1024, _V7X_LANES)
    bk = bq // 2 if bq % (2 * _V7X_LANES) == 0 else bq
    return bq, bk


def _resident_spec(seq_len, width, col):
    return pl.BlockSpec((seq_len, width), lambda b, h, i: (b, col // width + h), pipeline_mode=pl.Buffered(1))


def _kv_sweep(qi, r, bk, step):
    def trip(i, carry):
        for d in range(r):
            step(i * r + d, False, 0)
        return carry

    lax.fori_loop(0, qi, trip, 0)
    for d in range(r):
        step(qi * r + d, True, d * bk)


def _diff_kernel(q_ref, k_ref, v_ref, lq1_ref, lk1_ref, lq2_ref, lk2_ref, g_ref, o_ref,
                 m_ref, l_ref, acc_ref, *, bq, bk, d, hp, scale, lam_init):
    qi = pl.program_id(2)
    hw = 2 * d
    m_ref[...] = jnp.full(m_ref.shape, _NEG_INF, _F32)
    l_ref[...] = jnp.zeros(l_ref.shape, _F32)
    acc_ref[...] = jnp.zeros(acc_ref.shape, _F32)
    q = (q_ref[...].astype(_F32) * (scale * _LOG2E)).astype(_BF16)

    def step(j, masked, row0):
        k0 = pl.multiple_of(j * bk, bk)
        k = k_ref[pl.ds(k0, bk), :]
        v = v_ref[pl.ds(k0, bk), :]
        rows = slice(row0, bq)
        for c in range(2 * hp):
            cols = slice(c * d, (c + 1) * d)
            s = _qk(q[rows, cols], k[:, cols])
            if masked:
                s = _causal_mask(s, qi * bq + row0, k0)
            m_new, alpha, p = _softmax_block(s, m_ref[c, rows])
            l_ref[c, rows] = alpha * l_ref[c, rows] + jnp.sum(p, axis=-1, keepdims=True)
            vh = v[:, (c // 2) * hw:(c // 2 + 1) * hw]
            acc_ref[c, rows] = (_lane_repeat(alpha, hw) * acc_ref[c, rows]
                                + jnp.dot(p.astype(_BF16), vh, preferred_element_type=_F32))
            m_ref[c, rows] = m_new

    _kv_sweep(qi, bq // bk, bk, step)
    lam = (jnp.exp(jnp.sum(lq1_ref[...] * lk1_ref[...], axis=-1, keepdims=True))
           - jnp.exp(jnp.sum(lq2_ref[...] * lk2_ref[...], axis=-1, keepdims=True)) + lam_init)
    for h in range(hp):
        o = (acc_ref[2 * h] / _lane_repeat(l_ref[2 * h], hw)
             - lam * (acc_ref[2 * h + 1] / _lane_repeat(l_ref[2 * h + 1], hw)))
        o = o * lax.rsqrt(jnp.mean(o * o, axis=-1, keepdims=True) + _EPS) * g_ref[...]
        o_ref[:, h * hw:(h + 1) * hw] = (o * (1.0 - lam_init)).astype(o_ref.dtype)


def _diff_attention(p, lam_vecs, norm_g, lam_init, batch, seq_len, col_q, col_k, col_v, heads, d):
    n = p.shape[0]
    hw = 2 * d
    bq, bk = _flash_blocks(seq_len)
    nq = seq_len // bq
    hp = 2 if heads % 2 == 0 else 1
    w = hp * hw
    vec = pl.BlockSpec((1, d), lambda b, h, i: (0, 0))
    vmem = 2 * seq_len * w * 2 + 4 * bq * w * 2 + 2 * hp * bq * (hw + 2 * _V7X_LANES) * 4 + 12 * bq * bk * 4
    return pl.pallas_call(
        functools.partial(_diff_kernel, bq=bq, bk=bk, d=d, hp=hp, scale=d ** -0.5, lam_init=lam_init),
        grid=(batch, heads // hp, nq),
        in_specs=[pl.BlockSpec((bq, w), lambda b, h, i: (b * nq + i, col_q // w + h)),
                  _resident_spec(seq_len, w, col_k), _resident_spec(seq_len, w, col_v),
                  vec, vec, vec, vec,
                  pl.BlockSpec((1, hw), lambda b, h, i: (0, 0))],
        out_specs=pl.BlockSpec((bq, w), lambda b, h, i: (b * nq + i, h)),
        out_shape=jax.ShapeDtypeStruct((n, heads * hw), _BF16),
        scratch_shapes=[pltpu.VMEM((2 * hp, bq, _V7X_LANES), _F32), pltpu.VMEM((2 * hp, bq, _V7X_LANES), _F32),
                        pltpu.VMEM((2 * hp, bq, hw), _F32)],
        compiler_params=_params(("parallel", "parallel", "arbitrary"), vmem),
        name="diff_attention",
    )(p, p, p, *[v.reshape(1, d).astype(_F32) for v in lam_vecs], norm_g.reshape(1, hw).astype(_F32))


def _fox_cum_kernel(f_ref, bias_ref, o_ref):
    x = f_ref[...] + bias_ref[...]
    ls = jnp.minimum(x, 0.0) - jnp.log1p(jnp.exp(-jnp.abs(x)))
    r = lax.broadcasted_iota(jnp.int32, (_V7X_LANES, _V7X_LANES), 0)
    c = lax.broadcasted_iota(jnp.int32, (_V7X_LANES, _V7X_LANES), 1)
    upper = (r <= c).astype(_F32)
    carry = jnp.zeros((x.shape[0], 1), _F32)
    for t in range(x.shape[1] // _V7X_LANES):
        sl = slice(t * _V7X_LANES, (t + 1) * _V7X_LANES)
        w = jnp.dot(ls[:, sl], upper, precision=lax.Precision.HIGHEST, preferred_element_type=_F32) + carry
        o_ref[:, sl] = w
        carry = w[:, _V7X_LANES - 1:]


def _fox_cumsum(f_rows, bias_rows):
    return pl.pallas_call(
        _fox_cum_kernel,
        out_shape=jax.ShapeDtypeStruct(f_rows.shape, _F32),
        name="fox_cumsum",
    )(f_rows, bias_rows)


def _fox_kernel(q_ref, k_ref, v_ref, cq_ref, ck_ref, o_ref, m_ref, acc_ref, *, bq, bk, d, hp, scale):
    qi = pl.program_id(2)
    m_ref[...] = jnp.full(m_ref.shape, _NEG_INF, _F32)
    acc_ref[...] = jnp.zeros(acc_ref.shape, _F32)
    q = (q_ref[...].astype(_F32) * (scale * _LOG2E)).astype(_BF16)
    cq = [jnp.broadcast_to(cq_ref[0, h] * _LOG2E, (bq, _V7X_LANES)) for h in range(hp)]
    ones = jnp.ones((bk, _V7X_LANES), _BF16)

    def step(j, masked, row0):
        k0 = pl.multiple_of(j * bk, bk)
        rows = slice(row0, bq)
        k = k_ref[pl.ds(k0, bk), :]
        v = v_ref[pl.ds(k0, bk), :]
        for h in range(hp):
            cols = slice(h * d, (h + 1) * d)
            s = _qk(q[rows, cols], k[:, cols])
            ck = ck_ref[0, h, pl.ds(j, 1), :] * _LOG2E
            s = jnp.concatenate([st + (cq[h][rows] - ct)
                                 for st, ct in zip(_lane_tiles(s, bk), _lane_tiles(ck, bk))], axis=1)
            if masked:
                s = _causal_mask(s, qi * bq + row0, k0)
            m_new, alpha, p = _softmax_block(s, m_ref[h, rows])
            v_ones = jnp.concatenate([v[:, cols], ones], axis=1)
            acc_ref[h, rows] = (_lane_repeat(alpha, d + _V7X_LANES) * acc_ref[h, rows]
                                + jnp.dot(p.astype(_BF16), v_ones, preferred_element_type=_F32))
            m_ref[h, rows] = m_new

    _kv_sweep(qi, bq // bk, bk, step)
    for h in range(hp):
        acc = acc_ref[h]
        o_ref[:, h * d:(h + 1) * d] = (acc[:, :d] / _lane_repeat(acc[:, d:], d)).astype(o_ref.dtype)


def _fox_attention(p, cum, batch, seq_len, col_q, col_k, col_v, heads, d):
    n = p.shape[0]
    bq, bk = _flash_blocks(seq_len)
    nq, nk = seq_len // bq, seq_len // bk
    hp = 4 if heads % 4 == 0 else 1
    w = hp * d
    cum_rows = cum.reshape(batch, heads, nk, bk)
    cum_cols = cum.reshape(batch, heads, seq_len, 1)
    vmem = (2 * seq_len * w * 2 + 4 * bq * w * 2 + 2 * hp * seq_len * 4 + 2 * hp * bq * _V7X_LANES * 4
            + hp * bq * (d + 2 * _V7X_LANES) * 4 + 12 * bq * bk * 4)
    return pl.pallas_call(
        functools.partial(_fox_kernel, bq=bq, bk=bk, d=d, hp=hp, scale=d ** -0.5),
        grid=(batch, heads // hp, nq),
        in_specs=[pl.BlockSpec((bq, w), lambda b, h, i: (b * nq + i, col_q // w + h)),
                  _resident_spec(seq_len, w, col_k), _resident_spec(seq_len, w, col_v),
                  pl.BlockSpec((1, hp, bq, 1), lambda b, h, i: (b, h, i, 0)),
                  pl.BlockSpec((1, hp, nk, bk), lambda b, h, i: (b, h, 0, 0))],
        out_specs=pl.BlockSpec((bq, w), lambda b, h, i: (b * nq + i, h)),
        out_shape=jax.ShapeDtypeStruct((n, heads * d), _BF16),
        scratch_shapes=[pltpu.VMEM((hp, bq, _V7X_LANES), _F32), pltpu.VMEM((hp, bq, d + _V7X_LANES), _F32)],
        compiler_params=_params(("parallel", "parallel", "arbitrary"), vmem),
        name="fox_attention",
    )(p, p, p, cum_cols, cum_rows)


def _lane_column(x, h):
    lane = lax.broadcasted_iota(jnp.int32, x.shape, 1)
    return jnp.sum(jnp.where(lane == h, x, 0.0), axis=-1, keepdims=True)


def _bdot(a, b):
    return jnp.dot(a.astype(_BF16), b.astype(_BF16), preferred_element_type=_F32)


_INV_BASE_LOG2 = 4


def _unit_lower_inverses(mats, ri, ci):
    c = mats[0].shape[0]
    same_block = lambda log2n: (ri >> log2n) == (ci >> log2n)
    eye = (ri == ci).astype(_F32)
    base = same_block(_INV_BASE_LOG2)
    pw = [jnp.where(base, -a, 0.0) for a in mats]
    tinv = [eye + x for x in pw]
    for _ in range(_INV_BASE_LOG2 - 1):
        pw = [_bdot(x, x) for x in pw]
        tinv = [t + _bdot(t, x) for t, x in zip(tinv, pw)]
    log2n = _INV_BASE_LOG2
    while (1 << log2n) < c:
        level = same_block(log2n + 1) & jnp.logical_not(same_block(log2n))
        off = [_bdot(t, jnp.where(level, a, 0.0)) for t, a in zip(tinv, mats)]
        tinv = [t - _bdot(o, t) for t, o in zip(tinv, off)]
        log2n += 1
    return tinv


def _deltanet_kernel(q_ref, k_ref, v_ref, z_ref, gc_ref, gr_ref, cw_ref, alog_c_ref, dtb_c_ref,
                     alog_r_ref, dtb_r_ref, ng_ref, o_ref, s_ref, tail_ref, *, heads, d, c):
    t = pl.program_id(1)
    width = heads * d

    @pl.when(t == 0)
    def _():
        s_ref[...] = jnp.zeros(s_ref.shape, _F32)
        tail_ref[...] = jnp.zeros(tail_ref.shape, _F32)

    cw = cw_ref[...]
    conv = []
    for idx, ref in enumerate((q_ref, k_ref, v_ref)):
        raw = ref[...].astype(_F32)
        cols = slice(idx * width, (idx + 1) * width)
        conv.append(_silu(_causal_conv(raw, tail_ref[:, cols], cw[:, cols])))
        tail_ref[:, cols] = raw[c - _V7X_SUBLANES:]

    gates_c = gc_ref[...]
    g_cols = -jnp.exp(alog_c_ref[...]) * _softplus(gates_c + dtb_c_ref[...])
    beta_cols = 1.0 / (1.0 + jnp.exp(-gates_c))
    gates_r = gr_ref[0]
    g_rows = -jnp.exp(alog_r_ref[...]) * _softplus(gates_r[:heads] + dtb_r_ref[...])
    ri = lax.broadcasted_iota(jnp.int32, (c, c), 0)
    ci = lax.broadcasted_iota(jnp.int32, (c, c), 1)
    incl = ri >= ci
    strict = ri > ci
    hp = lax.Precision.HIGHEST
    gcum_cols = jnp.dot(incl.astype(_F32), g_cols, precision=hp, preferred_element_type=_F32)
    gcum_rows = jnp.dot(g_rows, (ri <= ci).astype(_F32), precision=hp, preferred_element_type=_F32)

    hr = range(heads)
    cols = [slice(h * d, (h + 1) * d) for h in hr]
    unit = lambda x: x * lax.rsqrt(jnp.sum(x * x, axis=-1, keepdims=True) + _EPS)
    q = [unit(conv[0][:, s]) * (d ** -0.5) for s in cols]
    k = [unit(conv[1][:, s]) for s in cols]
    v = [conv[2][:, s] for s in cols]
    gc = [_lane_column(gcum_cols, h) for h in hr]
    beta = [_lane_column(beta_cols, heads + h) for h in hr]
    decay = [jnp.exp(jnp.where(incl, gc[h] - gcum_rows[h:h + 1], _NEG_INF)) for h in hr]
    k16 = [x.astype(_BF16) for x in k]
    kb = [k[h] * beta[h] for h in hr]
    a = [jnp.where(strict, _qk(kb[h].astype(_BF16), k16[h]) * decay[h], 0.0) for h in hr]
    tinv = _unit_lower_inverses(a, ri, ci)
    egc = [jnp.exp(x) for x in gc]
    wu = [_bdot(tinv[h], jnp.concatenate([kb[h] * egc[h], v[h] * beta[h]], axis=1)) for h in hr]
    aqk = [_qk(q[h].astype(_BF16), k16[h]) * decay[h] for h in hr]
    s = [s_ref[h] for h in hr]
    ws_qs = [_bdot(jnp.concatenate([wu[h][:, :d], q[h] * egc[h]], axis=0), s[h]) for h in hr]
    v_new = [wu[h][:, d:] - ws_qs[h][:c] for h in hr]
    o = [ws_qs[h][c:] + _bdot(aqk[h], v_new[h]) for h in hr]
    g_last = [x[c - 1:c] for x in gc]
    k_dec = [k[h] * jnp.exp(g_last[h] - gc[h]) for h in hr]
    for h in hr:
        s_ref[h] = s[h] * jnp.exp(g_last[h]) + lax.dot_general(
            k_dec[h].astype(_BF16), v_new[h].astype(_BF16), (((0,), (0,)), ((), ())), preferred_element_type=_F32)
    ng = ng_ref[...]
    for h in hr:
        on = o[h] * lax.rsqrt(jnp.mean(o[h] * o[h], axis=-1, keepdims=True) + _EPS) * ng
        o_ref[:, cols[h]] = (on * _silu(z_ref[:, cols[h]].astype(_F32))).astype(o_ref.dtype)


def _gated_deltanet(p, gates_cols, gates_rows, conv_w, a_log, dt_bias, norm_g, batch, seq_len,
                    col_q, col_k, col_v, col_z, heads, d):
    n = p.shape[0]
    c = _tile(seq_len, _DN_CHUNK, _BF16_ROWS)
    nt = seq_len // c
    width = heads * d
    lanes = _V7X_LANES
    kw = conv_w.shape[0]
    col = lambda off: pl.BlockSpec((c, width), lambda b, t: (b * nt + t, off // width))
    const = lambda shape: pl.BlockSpec(shape, lambda b, t: (0,) * len(shape))
    pad_lane = lambda v: jnp.zeros((1, lanes), _F32).at[0, :heads].set(v.astype(_F32))
    vmem = 2 * 5 * c * width * 2 + 12 * c * width * 4 + heads * d * d * 4 + 2 * kw * 3 * width * 4
    return pl.pallas_call(
        functools.partial(_deltanet_kernel, heads=heads, d=d, c=c),
        grid=(batch, nt),
        in_specs=[col(col_q), col(col_k), col(col_v), col(col_z),
                  pl.BlockSpec((c, lanes), lambda b, t: (b * nt + t, 0)),
                  pl.BlockSpec((1, gates_rows.shape[1], c), lambda b, t: (b, 0, t)),
                  const((kw, 3 * width)), const((1, lanes)), const((1, lanes)),
                  const((heads, 1)), const((heads, 1)), const((1, d))],
        out_specs=pl.BlockSpec((c, width), lambda b, t: (b * nt + t, 0)),
        out_shape=jax.ShapeDtypeStruct((n, width), _BF16),
        scratch_shapes=[pltpu.VMEM((heads, d, d), _F32), pltpu.VMEM((_V7X_SUBLANES, 3 * width), _F32)],
        compiler_params=_params(("parallel", "arbitrary"), vmem),
        name="gated_deltanet",
    )(p, p, p, p, gates_cols, gates_rows, conv_w.astype(_F32), pad_lane(a_log), pad_lane(dt_bias),
      a_log.reshape(heads, 1).astype(_F32), dt_bias.reshape(heads, 1).astype(_F32),
      norm_g.reshape(1, d).astype(_F32))


def _ffn_in_kernel(h_ref, wg_ref, wu_ref, cw_ref, o_ref, tail_ref, *, blocks_per_seq, parts):
    i, j = pl.program_id(0), pl.program_id(1)
    tm, tn = o_ref.shape

    @pl.when(i % blocks_per_seq == 0)
    def _():
        tail_ref[j] = jnp.zeros(tail_ref.shape[1:], _F32)

    h = h_ref[...]
    for c in range(parts):
        cols = slice(c * (tn // parts), (c + 1) * (tn // parts))
        g = jnp.dot(h, wg_ref[:, cols], preferred_element_type=_F32)
        u = jnp.dot(h, wu_ref[:, cols], preferred_element_type=_F32)
        a = _causal_conv(g, tail_ref[j, :, cols], cw_ref[:, cols])
        tail_ref[j, :, cols] = g[tm - _V7X_SUBLANES:]
        o_ref[:, cols] = (_silu(a) * u).astype(o_ref.dtype)


def _ffn_in(h, w_gate, w_up, conv_w, layer, seq_len, tm_target=1024, tn_target=512):
    n, d = h.shape
    ff = w_gate.shape[2]
    tm = _tile(seq_len, tm_target, _BF16_ROWS)
    tn = _tile(ff, tn_target, _V7X_LANES)
    parts = 2 if tn % (2 * _V7X_LANES) == 0 else 1
    kw = conv_w.shape[1]
    wspec = pl.BlockSpec((None, d, tn), lambda i, j: (layer, 0, j))
    vmem = 2 * 2 * d * (tm + 2 * tn) + 2 * tm * tn * 2 + 6 * tm * tn * 4 + ff * _V7X_SUBLANES * 4
    return pl.pallas_call(
        functools.partial(_ffn_in_kernel, blocks_per_seq=seq_len // tm, parts=parts),
        grid=(n // tm, ff // tn),
        in_specs=[pl.BlockSpec((tm, d), lambda i, j: (i, 0)), wspec, wspec,
                  pl.BlockSpec((None, kw, tn), lambda i, j: (layer, 0, j))],
        out_specs=pl.BlockSpec((tm, tn), lambda i, j: (i, j)),
        out_shape=jax.ShapeDtypeStruct((n, ff), _BF16),
        scratch_shapes=[pltpu.VMEM((ff // tn, _V7X_SUBLANES, tn), _F32)],
        compiler_params=_params(("arbitrary", "arbitrary"), vmem),
        name="ffn_in",
    )(h, w_gate, w_up, conv_w)


def _cast_pad_kernel(x_ref, o_ref, *, rows, cols):
    i, j = pl.program_id(1), pl.program_id(2)
    tr, tc = o_ref.shape
    r = i * tr + lax.broadcasted_iota(jnp.int32, (tr, tc), 0)
    c = j * tc + lax.broadcasted_iota(jnp.int32, (tr, tc), 1)
    o_ref[...] = jnp.where(jnp.logical_and(r < rows, c < cols), x_ref[...], 0.0).astype(o_ref.dtype)


def _cast_pad(w, rows_to, cols_to, dtype):
    layers, rows, cols = w.shape
    tr = _tile(rows_to, 1024, _BF16_ROWS)
    tc = _tile(cols_to, 1024, _V7X_LANES)
    assert rows_to - rows < tr and cols_to - cols < tc
    return pl.pallas_call(
        functools.partial(_cast_pad_kernel, rows=rows, cols=cols),
        grid=(layers, rows_to // tr, cols_to // tc),
        in_specs=[pl.BlockSpec((None, tr, tc), lambda l, i, j: (l, i, j))],
        out_specs=pl.BlockSpec((None, tr, tc), lambda l, i, j: (l, i, j)),
        out_shape=jax.ShapeDtypeStruct((layers, rows_to, cols_to), dtype),
        compiler_params=_params(("parallel", "parallel", "parallel"), 2 * tr * tc * 6),
        name="cast_pad",
    )(w)


def _pad_last(w, total):
    return jnp.pad(w, ((0, 0),) * (w.ndim - 1) + ((0, total - w.shape[-1]),))


def _bf16_weights(w_in, w_out, w_gate, w_up, ffn_conv, w_down, gw, dn_heads):
    o_dn_a = 10 * gw
    o_fx = o_dn_a + 2 * dn_heads
    o_fx_f = o_fx + 3 * gw
    tile2 = o_fx_f // _V7X_LANES * _V7X_LANES
    narrow = jnp.concatenate([w_in[..., o_dn_a:o_dn_a + _V7X_LANES], _pad_last(w_in[..., tile2:], _V7X_LANES)],
                             axis=-1).astype(_BF16)
    ff = w_gate.shape[-1]
    ffp = -(-ff // _FF_ALIGN) * _FF_ALIGN if ff > _FF_ALIGN else ff
    return dict(
        wide_a=w_in[..., :o_dn_a].astype(_BF16), wide_b=w_in[..., o_fx:o_fx_f].astype(_BF16),
        narrow=narrow, fox_f_lane=_V7X_LANES + o_fx_f - tile2, out=w_out.astype(_BF16),
        gate=_cast_pad(w_gate, w_gate.shape[1], ffp, _BF16), up=_cast_pad(w_up, w_up.shape[1], ffp, _BF16),
        conv=_pad_last(ffn_conv.astype(_F32), ffp),
        down=_cast_pad(w_down, ffp, w_down.shape[2], _BF16))


def _layer(x, layer, batch, seq_len, wts, attn_norm, sc_conv, lam_vecs, diff_norm, dn_conv, dn_a_log,
           dn_dt_bias, dn_norm, fox_bias, ffn_norm):
    d = lam_vecs[0].shape[-1]
    gw = wts["out"].shape[1] // 4
    dn_heads, fox_heads = dn_a_log.shape[-1], fox_bias.shape[-1]
    diff_heads = gw // diff_norm.shape[-1]
    col = lambda idx: idx * gw

    h = _rmsnorm(x, attn_norm, _BF16)
    p = _matmul([h], wts["wide_a"], layer, _BF16, name="in_proj")
    p_fox = _matmul([h], wts["wide_b"], layer, _BF16, name="in_proj_fox")
    gates = _matmul([h], wts["narrow"], layer, _F32, name="in_proj_gates")

    y_sc = _short_conv(p, sc_conv, seq_len, 0, 1, 2, gw)
    lam_init = 0.8 - 0.6 * math.exp(-0.3 * layer)
    y_df = _diff_attention(p, lam_vecs, diff_norm, lam_init, batch, seq_len, col(3), col(4), col(5), diff_heads, d)

    gates_t = gates.reshape(batch, seq_len, gates.shape[1]).transpose(0, 2, 1)
    f0 = wts["fox_f_lane"]
    fox_rows = gates_t[:, f0:f0 + fox_heads].reshape(batch * fox_heads, seq_len)
    fox_bias_rows = jnp.tile(fox_bias.astype(_F32), batch).reshape(batch * fox_heads, 1)
    cum = _fox_cumsum(fox_rows, fox_bias_rows)
    y_fx = _fox_attention(p_fox, cum, batch, seq_len, col(0), col(1), col(2), fox_heads, d)

    y_dn = _gated_deltanet(p, gates, gates_t[:, :2 * dn_heads], dn_conv, dn_a_log, dn_dt_bias, dn_norm,
                           batch, seq_len, col(6), col(7), col(8), col(9), dn_heads, d)

    x = _matmul([y_sc, y_df, y_dn, y_fx], wts["out"], layer, _F32, residual=x, tn_target=512, name="out_proj")

    h2 = _rmsnorm(x, ffn_norm, _BF16)
    act = _ffn_in(h2, wts["gate"], wts["up"], wts["conv"], layer, seq_len)
    return _matmul_acc(act, wts["down"], layer, x, name="ffn_out")


def kernel(x, attn_norm, w_in, sc_conv, lam_q1, lam_k1, lam_q2, lam_k2, diff_norm, dn_conv, dn_a_log,
           dn_dt_bias, dn_norm, fox_bias, w_out, ffn_norm, w_gate, w_up, ffn_conv, w_down, final_norm):
    batch, seq_len, d_model = x.shape
    xf = x.reshape(batch * seq_len, d_model).astype(_F32)
    wts = _bf16_weights(w_in, w_out, w_gate, w_up, ffn_conv, w_down, w_out.shape[1] // 4, dn_a_log.shape[-1])
    for l in range(w_in.shape[0]):
        xf = _layer(xf, l, batch, seq_len, wts, attn_norm[l], sc_conv[l],
                    (lam_q1[l], lam_k1[l], lam_q2[l], lam_k2[l]), diff_norm[l], dn_conv[l], dn_a_log[l],
                    dn_dt_bias[l], dn_norm[l], fox_bias[l], ffn_norm[l])
    out = _rmsnorm(xf, final_norm, x.dtype)
    return out.reshape(batch, seq_len, d_model)
```

```python
import functools
import math

import jax
import jax.numpy as jnp
from jax import lax
from jax.experimental import pallas as pl
from jax.experimental.pallas import tpu as pltpu

_F32 = jnp.float32
_BF16 = jnp.bfloat16
_EPS = 1e-6
_NEG_INF = float("-inf")

_V7X_VMEM_BYTES = 64 * 1024 * 1024
_V7X_LANES = 128
_V7X_SUBLANES = 8
_BF16_ROWS = 16
_DN_CHUNK = 128
_FF_ALIGN = 1024


def _tile(n, target, align):
    t = min(target, n)
    t -= t % align
    while t >= align:
        if n % t == 0:
            return t
        t -= align
    return n


def _params(semantics, vmem_bytes):
    limit = min(int(vmem_bytes) + (8 << 20), _V7X_VMEM_BYTES - (4 << 20))
    return pltpu.CompilerParams(dimension_semantics=semantics, vmem_limit_bytes=limit)


def _silu(x):
    return x * (1.0 / (1.0 + jnp.exp(-x)))


def _softplus(x):
    return jnp.maximum(x, 0.0) + jnp.log1p(jnp.exp(-jnp.abs(x)))


def _shift_rows(x, prev8, s):
    xs = pltpu.roll(x, s, 0)
    ps = pltpu.roll(prev8, s, 0)
    row = lax.broadcasted_iota(jnp.int32, prev8.shape, 0)
    head = jnp.where(row < s, ps, xs[:_V7X_SUBLANES])
    return jnp.concatenate([head, xs[_V7X_SUBLANES:]], axis=0)


def _causal_conv(x, prev8, w):
    k = w.shape[0]
    out = x * w[k - 1:k]
    for j in range(k - 1):
        out = out + _shift_rows(x, prev8, k - 1 - j) * w[j:j + 1]
    return out


def _rmsnorm_kernel(x_ref, g_ref, o_ref):
    x = x_ref[...]
    y = x * lax.rsqrt(jnp.mean(x * x, axis=-1, keepdims=True) + _EPS)
    o_ref[...] = (y * g_ref[...]).astype(o_ref.dtype)


def _rmsnorm(x, g, out_dtype):
    n, d = x.shape
    tm = _tile(n, 512, _BF16_ROWS)
    vmem = 2 * tm * d * (4 + jnp.dtype(out_dtype).itemsize) + 2 * d * 4
    return pl.pallas_call(
        _rmsnorm_kernel,
        grid=(n // tm,),
        in_specs=[pl.BlockSpec((tm, d), lambda i: (i, 0)), pl.BlockSpec((1, d), lambda i: (0, 0))],
        out_specs=pl.BlockSpec((tm, d), lambda i: (i, 0)),
        out_shape=jax.ShapeDtypeStruct((n, d), out_dtype),
        compiler_params=_params(("parallel",), vmem),
        name="rmsnorm",
    )(x, g.reshape(1, d).astype(_F32))


def _mm_kernel(*refs, n_pairs, has_res):
    a_refs, w_refs = refs[:n_pairs], refs[n_pairs:2 * n_pairs]
    o_ref = refs[-1]
    acc = jnp.dot(a_refs[0][...], w_refs[0][...], preferred_element_type=_F32)
    for a_ref, w_ref in zip(a_refs[1:], w_refs[1:]):
        acc = acc + jnp.dot(a_ref[...], w_ref[...], preferred_element_type=_F32)
    if has_res:
        acc = acc + refs[2 * n_pairs][...]
    o_ref[...] = acc.astype(o_ref.dtype)


def _matmul(a_list, w, layer, out_dtype, residual=None, tm_target=1024, tn_target=1024, name="matmul"):
    m, n = a_list[0].shape[0], w.shape[2]
    kb = a_list[0].shape[1]
    assert all(a.shape[1] == kb for a in a_list) and w.shape[1] == kb * len(a_list)
    tm = _tile(m, tm_target, _BF16_ROWS)
    tn = _tile(n, tn_target, _V7X_LANES)
    ktot = w.shape[1]
    in_specs = [pl.BlockSpec((tm, kb), lambda i, j: (i, 0)) for _ in a_list]
    in_specs += [pl.BlockSpec((None, kb, tn), functools.partial(lambda i, j, r: (layer, r, j), r=r))
                 for r in range(len(a_list))]
    args = list(a_list) + [w] * len(a_list)
    vmem = 2 * 2 * ktot * (tm + tn) + 2 * tm * tn * jnp.dtype(out_dtype).itemsize + tm * tn * 4
    if residual is not None:
        in_specs.append(pl.BlockSpec((tm, tn), lambda i, j: (i, j)))
        args.append(residual)
        vmem += 2 * tm * tn * 4
    return pl.pallas_call(
        functools.partial(_mm_kernel, n_pairs=len(a_list), has_res=residual is not None),
        grid=(m // tm, n // tn),
        in_specs=in_specs,
        out_specs=pl.BlockSpec((tm, tn), lambda i, j: (i, j)),
        out_shape=jax.ShapeDtypeStruct((m, n), out_dtype),
        compiler_params=_params(("parallel", "parallel"), vmem),
        name=name,
    )(*args)


def _mm_acc_kernel(a_ref, w_ref, r_ref, o_ref, acc_ref, *, nk):
    k = pl.program_id(2)
    prod = lambda: jnp.dot(a_ref[...], w_ref[...], preferred_element_type=_F32)
    if nk == 1:
        o_ref[...] = r_ref[...] + prod()
        return

    @pl.when(k == 0)
    def _():
        acc_ref[...] = r_ref[...] + prod()

    @pl.when(jnp.logical_and(k > 0, k < nk - 1))
    def _():
        acc_ref[...] += prod()

    @pl.when(k == nk - 1)
    def _():
        o_ref[...] = acc_ref[...] + prod()


def _matmul_acc(a, w, layer, residual, tm_target=1024, tn_target=1024, tk_target=2816, name="matmul_acc"):
    m, kdim = a.shape
    n = w.shape[2]
    tm = _tile(m, tm_target, _BF16_ROWS)
    tn = _tile(n, tn_target, _V7X_LANES)
    tk = _tile(kdim, tk_target, _V7X_LANES)
    vmem = 2 * 2 * tk * (tm + tn) + 5 * tm * tn * 4
    return pl.pallas_call(
        functools.partial(_mm_acc_kernel, nk=kdim // tk),
        grid=(m // tm, n // tn, kdim // tk),
        in_specs=[pl.BlockSpec((tm, tk), lambda i, j, k: (i, k)),
                  pl.BlockSpec((None, tk, tn), lambda i, j, k: (layer, k, j)),
                  pl.BlockSpec((tm, tn), lambda i, j, k: (i, j))],
        out_specs=pl.BlockSpec((tm, tn), lambda i, j, k: (i, j)),
        out_shape=jax.ShapeDtypeStruct((m, n), _F32),
        scratch_shapes=[pltpu.VMEM((tm, tn), _F32)],
        compiler_params=_params(("parallel", "parallel", "arbitrary"), vmem),
        name=name,
    )(a, w, residual)


def _sconv_kernel(h_ref, c_ref, b_ref, hh_ref, ch_ref, w_ref, o_ref, *, blocks_per_seq):
    i = pl.program_id(0)
    x = c_ref[...].astype(_F32) * h_ref[...].astype(_F32)
    halo = ch_ref[...].astype(_F32) * hh_ref[...].astype(_F32)
    prev8 = jnp.where(i % blocks_per_seq == 0, 0.0, halo[_BF16_ROWS - _V7X_SUBLANES:])
    y = b_ref[...].astype(_F32) * _causal_conv(x, prev8, w_ref[...])
    o_ref[...] = y.astype(o_ref.dtype)


def _short_conv(p, conv_w, seq_len, col_h, col_c, col_b, width):
    n = p.shape[0]
    tm = _tile(seq_len, 512, _BF16_ROWS)
    hb = tm // _BF16_ROWS
    cur = lambda c: pl.BlockSpec((tm, width), lambda i: (i, c))
    halo = lambda c: pl.BlockSpec((_BF16_ROWS, width), lambda i: (jnp.maximum(i * hb - 1, 0), c))
    k = conv_w.shape[0]
    vmem = 2 * (4 * tm + 2 * _BF16_ROWS) * width * 2 + 8 * tm * width * 4
    return pl.pallas_call(
        functools.partial(_sconv_kernel, blocks_per_seq=seq_len // tm),
        grid=(n // tm,),
        in_specs=[cur(col_h), cur(col_c), cur(col_b), halo(col_h), halo(col_c),
                  pl.BlockSpec((k, width), lambda i: (0, 0))],
        out_specs=pl.BlockSpec((tm, width), lambda i: (i, 0)),
        out_shape=jax.ShapeDtypeStruct((n, width), _BF16),
        compiler_params=_params(("parallel",), vmem),
        name="short_conv",
    )(p, p, p, p, p, conv_w.astype(_F32))


_LOG2E = math.log2(math.e)


def _lane_tiles(x, width):
    return [x[:, c:c + _V7X_LANES] for c in range(0, width, _V7X_LANES)]


def _lane_repeat(x, width):
    return jnp.concatenate([x] * (width // _V7X_LANES), axis=1)


def _softmax_block(s, m_prev):
    m_new = jnp.maximum(m_prev, jnp.max(s, axis=-1, keepdims=True))
    alpha = jnp.exp2(m_prev - m_new)
    p = jnp.concatenate([jnp.exp2(t - m_new) for t in _lane_tiles(s, s.shape[1])], axis=1)
    return m_new, alpha, p


def _causal_mask(s, q0, k0):
    row = q0 + lax.broadcasted_iota(jnp.int32, s.shape, 0)
    col = k0 + lax.broadcasted_iota(jnp.int32, s.shape, 1)
    return jnp.where(col <= row, s, _NEG_INF)


def _qk(q, k):
    return lax.dot_general(q, k, (((1,), (1,)), ((), ())), preferred_element_type=_F32)


def _flash_blocks(seq_len):
    bq = _tile(seq_len, 1024, _V7X_LANES)
    bk = bq // 2 if bq % (2 * _V7X_LANES) == 0 else bq
    return bq, bk


def _resident_spec(seq_len, width, col):
    return pl.BlockSpec((seq_len, width), lambda b, h, i: (b, col // width + h), pipeline_mode=pl.Buffered(1))


def _kv_sweep(qi, r, bk, step):
    def trip(i, carry):
        for d in range(r):
            step(i * r + d, False, 0)
        return carry

    lax.fori_loop(0, qi, trip, 0)
    for d in range(r):
        step(qi * r + d, True, d * bk)


def _diff_kernel(q_ref, k_ref, v_ref, lq1_ref, lk1_ref, lq2_ref, lk2_ref, g_ref, o_ref,
                 m_ref, l_ref, acc_ref, *, bq, bk, d, hp, scale, lam_init):
    qi = pl.program_id(2)
    hw = 2 * d
    m_ref[...] = jnp.full(m_ref.shape, _NEG_INF, _F32)
    l_ref[...] = jnp.zeros(l_ref.shape, _F32)
    acc_ref[...] = jnp.zeros(acc_ref.shape, _F32)
    q = (q_ref[...].astype(_F32) * (scale * _LOG2E)).astype(_BF16)

    def step(j, masked, row0):
        k0 = pl.multiple_of(j * bk, bk)
        k = k_ref[pl.ds(k0, bk), :]
        v = v_ref[pl.ds(k0, bk), :]
        rows = slice(row0, bq)
        for c in range(2 * hp):
            cols = slice(c * d, (c + 1) * d)
            s = _qk(q[rows, cols], k[:, cols])
            if masked:
                s = _causal_mask(s, qi * bq + row0, k0)
            m_new, alpha, p = _softmax_block(s, m_ref[c, rows])
            l_ref[c, rows] = alpha * l_ref[c, rows] + jnp.sum(p, axis=-1, keepdims=True)
            vh = v[:, (c // 2) * hw:(c // 2 + 1) * hw]
            acc_ref[c, rows] = (_lane_repeat(alpha, hw) * acc_ref[c, rows]
                                + jnp.dot(p.astype(_BF16), vh, preferred_element_type=_F32))
            m_ref[c, rows] = m_new

    _kv_sweep(qi, bq // bk, bk, step)
    lam = (jnp.exp(jnp.sum(lq1_ref[...] * lk1_ref[...], axis=-1, keepdims=True))
           - jnp.exp(jnp.sum(lq2_ref[...] * lk2_ref[...], axis=-1, keepdims=True)) + lam_init)
    for h in range(hp):
        o = (acc_ref[2 * h] / _lane_repeat(l_ref[2 * h], hw)
             - lam * (acc_ref[2 * h + 1] / _lane_repeat(l_ref[2 * h + 1], hw)))
        o = o * lax.rsqrt(jnp.mean(o * o, axis=-1, keepdims=True) + _EPS) * g_ref[...]
        o_ref[:, h * hw:(h + 1) * hw] = (o * (1.0 - lam_init)).astype(o_ref.dtype)


def _diff_attention(p, lam_vecs, norm_g, lam_init, batch, seq_len, col_q, col_k, col_v, heads, d):
    n = p.shape[0]
    hw = 2 * d
    bq, bk = _flash_blocks(seq_len)
    nq = seq_len // bq
    hp = 2 if heads % 2 == 0 else 1
    w = hp * hw
    vec = pl.BlockSpec((1, d), lambda b, h, i: (0, 0))
    vmem = 2 * seq_len * w * 2 + 4 * bq * w * 2 + 2 * hp * bq * (hw + 2 * _V7X_LANES) * 4 + 12 * bq * bk * 4
    return pl.pallas_call(
        functools.partial(_diff_kernel, bq=bq, bk=bk, d=d, hp=hp, scale=d ** -0.5, lam_init=lam_init),
        grid=(batch, heads // hp, nq),
        in_specs=[pl.BlockSpec((bq, w), lambda b, h, i: (b * nq + i, col_q // w + h)),
                  _resident_spec(seq_len, w, col_k), _resident_spec(seq_len, w, col_v),
                  vec, vec, vec, vec,
                  pl.BlockSpec((1, hw), lambda b, h, i: (0, 0))],
        out_specs=pl.BlockSpec((bq, w), lambda b, h, i: (b * nq + i, h)),
        out_shape=jax.ShapeDtypeStruct((n, heads * hw), _BF16),
        scratch_shapes=[pltpu.VMEM((2 * hp, bq, _V7X_LANES), _F32), pltpu.VMEM((2 * hp, bq, _V7X_LANES), _F32),
                        pltpu.VMEM((2 * hp, bq, hw), _F32)],
        compiler_params=_params(("parallel", "parallel", "arbitrary"), vmem),
        name="diff_attention",
    )(p, p, p, *[v.reshape(1, d).astype(_F32) for v in lam_vecs], norm_g.reshape(1, hw).astype(_F32))


def _fox_cum_kernel(f_ref, bias_ref, o_ref):
    x = f_ref[...] + bias_ref[...]
    ls = jnp.minimum(x, 0.0) - jnp.log1p(jnp.exp(-jnp.abs(x)))
    r = lax.broadcasted_iota(jnp.int32, (_V7X_LANES, _V7X_LANES), 0)
    c = lax.broadcasted_iota(jnp.int32, (_V7X_LANES, _V7X_LANES), 1)
    upper = (r <= c).astype(_F32)
    carry = jnp.zeros((x.shape[0], 1), _F32)
    for t in range(x.shape[1] // _V7X_LANES):
        sl = slice(t * _V7X_LANES, (t + 1) * _V7X_LANES)
        w = jnp.dot(ls[:, sl], upper, precision=lax.Precision.HIGHEST, preferred_element_type=_F32) + carry
        o_ref[:, sl] = w
        carry = w[:, _V7X_LANES - 1:]


def _fox_cumsum(f_rows, bias_rows):
    return pl.pallas_call(
        _fox_cum_kernel,
        out_shape=jax.ShapeDtypeStruct(f_rows.shape, _F32),
        name="fox_cumsum",
    )(f_rows, bias_rows)


def _fox_kernel(q_ref, k_ref, v_ref, cq_ref, ck_ref, o_ref, m_ref, acc_ref, *, bq, bk, d, hp, scale):
    qi = pl.program_id(2)
    m_ref[...] = jnp.full(m_ref.shape, _NEG_INF, _F32)
    acc_ref[...] = jnp.zeros(acc_ref.shape, _F32)
    q = (q_ref[...].astype(_F32) * (scale * _LOG2E)).astype(_BF16)
    cq = [jnp.broadcast_to(cq_ref[0, h] * _LOG2E, (bq, _V7X_LANES)) for h in range(hp)]
    ones = jnp.ones((bk, _V7X_LANES), _BF16)

    def step(j, masked, row0):
        k0 = pl.multiple_of(j * bk, bk)
        rows = slice(row0, bq)
        k = k_ref[pl.ds(k0, bk), :]
        v = v_ref[pl.ds(k0, bk), :]
        for h in range(hp):
            cols = slice(h * d, (h + 1) * d)
            s = _qk(q[rows, cols], k[:, cols])
            ck = ck_ref[0, h, pl.ds(j, 1), :] * _LOG2E
            s = jnp.concatenate([st + (cq[h][rows] - ct)
                                 for st, ct in zip(_lane_tiles(s, bk), _lane_tiles(ck, bk))], axis=1)
            if masked:
                s = _causal_mask(s, qi * bq + row0, k0)
            m_new, alpha, p = _softmax_block(s, m_ref[h, rows])
            v_ones = jnp.concatenate([v[:, cols], ones], axis=1)
            acc_ref[h, rows] = (_lane_repeat(alpha, d + _V7X_LANES) * acc_ref[h, rows]
                                + jnp.dot(p.astype(_BF16), v_ones, preferred_element_type=_F32))
            m_ref[h, rows] = m_new

    _kv_sweep(qi, bq // bk, bk, step)
    for h in range(hp):
        acc = acc_ref[h]
        o_ref[:, h * d:(h + 1) * d] = (acc[:, :d] / _lane_repeat(acc[:, d:], d)).astype(o_ref.dtype)


def _fox_attention(p, cum, batch, seq_len, col_q, col_k, col_v, heads, d):
    n = p.shape[0]
    bq, bk = _flash_blocks(seq_len)
    nq, nk = seq_len // bq, seq_len // bk
    hp = 4 if heads % 4 == 0 else 1
    w = hp * d
    cum_rows = cum.reshape(batch, heads, nk, bk)
    cum_cols = cum.reshape(batch, heads, seq_len, 1)
    vmem = (2 * seq_len * w * 2 + 4 * bq * w * 2 + 2 * hp * seq_len * 4 + 2 * hp * bq * _V7X_LANES * 4
            + hp * bq * (d + 2 * _V7X_LANES) * 4 + 12 * bq * bk * 4)
    return pl.pallas_call(
        functools.partial(_fox_kernel, bq=bq, bk=bk, d=d, hp=hp, scale=d ** -0.5),
        grid=(batch, heads // hp, nq),
        in_specs=[pl.BlockSpec((bq, w), lambda b, h, i: (b * nq + i, col_q // w + h)),
                  _resident_spec(seq_len, w, col_k), _resident_spec(seq_len, w, col_v),
                  pl.BlockSpec((1, hp, bq, 1), lambda b, h, i: (b, h, i, 0)),
                  pl.BlockSpec((1, hp, nk, bk), lambda b, h, i: (b, h, 0, 0))],
        out_specs=pl.BlockSpec((bq, w), lambda b, h, i: (b * nq + i, h)),
        out_shape=jax.ShapeDtypeStruct((n, heads * d), _BF16),
        scratch_shapes=[pltpu.VMEM((hp, bq, _V7X_LANES), _F32), pltpu.VMEM((hp, bq, d + _V7X_LANES), _F32)],
        compiler_params=_params(("parallel", "parallel", "arbitrary"), vmem),
        name="fox_attention",
    )(p, p, p, cum_cols, cum_rows)


def _lane_column(x, h):
    lane = lax.broadcasted_iota(jnp.int32, x.shape, 1)
    return jnp.sum(jnp.where(lane == h, x, 0.0), axis=-1, keepdims=True)


def _bdot(a, b):
    return jnp.dot(a.astype(_BF16), b.astype(_BF16), preferred_element_type=_F32)


_INV_BASE_LOG2 = 4


def _unit_lower_inverses(mats, ri, ci):
    c = mats[0].shape[0]
    same_block = lambda log2n: (ri >> log2n) == (ci >> log2n)
    eye = (ri == ci).astype(_F32)
    base = same_block(_INV_BASE_LOG2)
    pw = [jnp.where(base, -a, 0.0) for a in mats]
    tinv = [eye + x for x in pw]
    for _ in range(_INV_BASE_LOG2 - 1):
        pw = [_bdot(x, x) for x in pw]
        tinv = [t + _bdot(t, x) for t, x in zip(tinv, pw)]
    log2n = _INV_BASE_LOG2
    while (1 << log2n) < c:
        level = same_block(log2n + 1) & jnp.logical_not(same_block(log2n))
        off = [_bdot(t, jnp.where(level, a, 0.0)) for t, a in zip(tinv, mats)]
        tinv = [t - _bdot(o, t) for t, o in zip(tinv, off)]
        log2n += 1
    return tinv


def _deltanet_kernel(q_ref, k_ref, v_ref, z_ref, gc_ref, gr_ref, cw_ref, alog_c_ref, dtb_c_ref,
                     alog_r_ref, dtb_r_ref, ng_ref, o_ref, s_ref, tail_ref, *, nb, heads, d, c):
    t = pl.program_id(0)
    width = heads * d

    @pl.when(t == 0)
    def _():
        s_ref[...] = jnp.zeros(s_ref.shape, _F32)
        tail_ref[...] = jnp.zeros(tail_ref.shape, _F32)

    ri = lax.broadcasted_iota(jnp.int32, (c, c), 0)
    ci = lax.broadcasted_iota(jnp.int32, (c, c), 1)
    incl = ri >= ci
    strict = ri > ci
    lower = incl.astype(_F32)
    upper = (ri <= ci).astype(_F32)
    hp = lax.Precision.HIGHEST
    cw = cw_ref[...]
    unit = lambda x: x * lax.rsqrt(jnp.sum(x * x, axis=-1, keepdims=True) + _EPS)

    q, k, v, gc, beta, gcr = [], [], [], [], [], []
    for b in range(nb):
        conv = []
        for idx, ref in enumerate((q_ref, k_ref, v_ref)):
            raw = ref[b].astype(_F32)
            cols = slice(idx * width, (idx + 1) * width)
            conv.append(_silu(_causal_conv(raw, tail_ref[b, :, cols], cw[:, cols])))
            tail_ref[b, :, cols] = raw[c - _V7X_SUBLANES:]
        gates_c = gc_ref[b]
        g_cols = -jnp.exp(alog_c_ref[...]) * _softplus(gates_c + dtb_c_ref[...])
        beta_cols = 1.0 / (1.0 + jnp.exp(-gates_c))
        g_rows = -jnp.exp(alog_r_ref[...]) * _softplus(gr_ref[b][:heads] + dtb_r_ref[...])
        gcum_cols = jnp.dot(lower, g_cols, precision=hp, preferred_element_type=_F32)
        gcum_rows = jnp.dot(g_rows, upper, precision=hp, preferred_element_type=_F32)
        for h in range(heads):
            hs = slice(h * d, (h + 1) * d)
            q.append(unit(conv[0][:, hs]) * (d ** -0.5))
            k.append(unit(conv[1][:, hs]))
            v.append(conv[2][:, hs])
            gc.append(_lane_column(gcum_cols, h))
            beta.append(_lane_column(beta_cols, heads + h))
            gcr.append(gcum_rows[h:h + 1])
    chains = range(nb * heads)
    decay = [jnp.exp(jnp.where(incl, gc[n] - gcr[n], _NEG_INF)) for n in chains]
    k16 = [x.astype(_BF16) for x in k]
    kb = [k[n] * beta[n] for n in chains]
    a = [jnp.where(strict, _qk(kb[n].astype(_BF16), k16[n]) * decay[n], 0.0) for n in chains]
    tinv = _unit_lower_inverses(a, ri, ci)
    egc = [jnp.exp(x) for x in gc]
    wu = [_bdot(tinv[n], jnp.concatenate([kb[n] * egc[n], v[n] * beta[n]], axis=1)) for n in chains]
    aqk = [_qk(q[n].astype(_BF16), k16[n]) * decay[n] for n in chains]
    s = [s_ref[n // heads, n % heads] for n in chains]
    ws_qs = [_bdot(jnp.concatenate([wu[n][:, :d], q[n] * egc[n]], axis=0), s[n]) for n in chains]
    v_new = [wu[n][:, d:] - ws_qs[n][:c] for n in chains]
    o = [ws_qs[n][c:] + _bdot(aqk[n], v_new[n]) for n in chains]
    g_last = [x[c - 1:c] for x in gc]
    k_dec = [k[n] * jnp.exp(g_last[n] - gc[n]) for n in chains]
    for n in chains:
        s_ref[n // heads, n % heads] = s[n] * jnp.exp(g_last[n]) + lax.dot_general(
            k_dec[n].astype(_BF16), v_new[n].astype(_BF16), (((0,), (0,)), ((), ())), preferred_element_type=_F32)
    ng = ng_ref[...]
    for n in chains:
        hs = slice((n % heads) * d, (n % heads + 1) * d)
        on = o[n] * lax.rsqrt(jnp.mean(o[n] * o[n], axis=-1, keepdims=True) + _EPS) * ng
        o_ref[n // heads, :, hs] = (on * _silu(z_ref[n // heads, :, hs].astype(_F32))).astype(o_ref.dtype)


def _gated_deltanet(p, gates_cols, gates_rows, conv_w, a_log, dt_bias, norm_g, batch, seq_len,
                    col_q, col_k, col_v, col_z, heads, d):
    n = p.shape[0]
    c = _tile(seq_len, _DN_CHUNK, _BF16_ROWS)
    nt = seq_len // c
    width = heads * d
    lanes = _V7X_LANES
    kw = conv_w.shape[0]
    p3 = p.reshape(batch, seq_len, p.shape[1])
    col = lambda off: pl.BlockSpec((batch, c, width), lambda t: (0, t, off // width))
    const = lambda shape: pl.BlockSpec(shape, lambda t: (0,) * len(shape))
    pad_lane = lambda v: jnp.zeros((1, lanes), _F32).at[0, :heads].set(v.astype(_F32))
    vmem = batch * (2 * 5 * c * width * 2 + 14 * c * width * 4 + heads * d * d * 4) + 2 * kw * 3 * width * 4
    out = pl.pallas_call(
        functools.partial(_deltanet_kernel, nb=batch, heads=heads, d=d, c=c),
        grid=(nt,),
        in_specs=[col(col_q), col(col_k), col(col_v), col(col_z),
                  pl.BlockSpec((batch, c, lanes), lambda t: (0, t, 0)),
                  pl.BlockSpec((batch, gates_rows.shape[1], c), lambda t: (0, 0, t)),
                  const((kw, 3 * width)), const((1, lanes)), const((1, lanes)),
                  const((heads, 1)), const((heads, 1)), const((1, d))],
        out_specs=pl.BlockSpec((batch, c, width), lambda t: (0, t, 0)),
        out_shape=jax.ShapeDtypeStruct((batch, seq_len, width), _BF16),
        scratch_shapes=[pltpu.VMEM((batch, heads, d, d), _F32),
                        pltpu.VMEM((batch, _V7X_SUBLANES, 3 * width), _F32)],
        compiler_params=_params(("arbitrary",), vmem),
        name="gated_deltanet",
    )(p3, p3, p3, p3, gates_cols.reshape(batch, seq_len, gates_cols.shape[1]), gates_rows, conv_w.astype(_F32),
      pad_lane(a_log), pad_lane(dt_bias), a_log.reshape(heads, 1).astype(_F32),
      dt_bias.reshape(heads, 1).astype(_F32), norm_g.reshape(1, d).astype(_F32))
    return out.reshape(n, width)


def _ffn_in_kernel(h_ref, wg_ref, wu_ref, cw_ref, o_ref, tail_ref, *, blocks_per_seq):
    i, j = pl.program_id(0), pl.program_id(1)
    tm = o_ref.shape[0]

    @pl.when(i % blocks_per_seq == 0)
    def _():
        tail_ref[j] = jnp.zeros(tail_ref.shape[1:], _F32)

    h = h_ref[...]
    g = jnp.dot(h, wg_ref[...], preferred_element_type=_F32)
    sg = _silu(_causal_conv(g, tail_ref[j], cw_ref[...]))
    tail_ref[j] = g[tm - _V7X_SUBLANES:]
    u = jnp.dot(h, wu_ref[...], preferred_element_type=_F32)
    o_ref[...] = (sg * u).astype(o_ref.dtype)


def _ffn_in(h, w_gate, w_up, conv_w, layer, seq_len, tm_target=1024, tn_target=512):
    n, d = h.shape
    ff = w_gate.shape[2]
    tm = _tile(seq_len, tm_target, _BF16_ROWS)
    tn = _tile(ff, tn_target, _V7X_LANES)
    kw = conv_w.shape[1]
    wspec = pl.BlockSpec((None, d, tn), lambda i, j: (layer, 0, j))
    vmem = 2 * 2 * d * (tm + 2 * tn) + 2 * tm * tn * 2 + 6 * tm * tn * 4 + ff * _V7X_SUBLANES * 4
    return pl.pallas_call(
        functools.partial(_ffn_in_kernel, blocks_per_seq=seq_len // tm),
        grid=(n // tm, ff // tn),
        in_specs=[pl.BlockSpec((tm, d), lambda i, j: (i, 0)), wspec, wspec,
                  pl.BlockSpec((None, kw, tn), lambda i, j: (layer, 0, j))],
        out_specs=pl.BlockSpec((tm, tn), lambda i, j: (i, j)),
        out_shape=jax.ShapeDtypeStruct((n, ff), _BF16),
        scratch_shapes=[pltpu.VMEM((ff // tn, _V7X_SUBLANES, tn), _F32)],
        compiler_params=_params(("arbitrary", "arbitrary"), vmem),
        name="ffn_in",
    )(h, w_gate, w_up, conv_w)


def _cast_pad_kernel(x_ref, o_ref, *, rows, cols):
    i, j = pl.program_id(1), pl.program_id(2)
    tr, tc = o_ref.shape
    r = i * tr + lax.broadcasted_iota(jnp.int32, (tr, tc), 0)
    c = j * tc + lax.broadcasted_iota(jnp.int32, (tr, tc), 1)
    o_ref[...] = jnp.where(jnp.logical_and(r < rows, c < cols), x_ref[...], 0.0).astype(o_ref.dtype)


def _cast_pad(w, rows_to, cols_to, dtype):
    layers, rows, cols = w.shape
    tr = _tile(rows_to, 1024, _BF16_ROWS)
    tc = _tile(cols_to, 1024, _V7X_LANES)
    assert rows_to - rows < tr and cols_to - cols < tc
    return pl.pallas_call(
        functools.partial(_cast_pad_kernel, rows=rows, cols=cols),
        grid=(layers, rows_to // tr, cols_to // tc),
        in_specs=[pl.BlockSpec((None, tr, tc), lambda l, i, j: (l, i, j))],
        out_specs=pl.BlockSpec((None, tr, tc), lambda l, i, j: (l, i, j)),
        out_shape=jax.ShapeDtypeStruct((layers, rows_to, cols_to), dtype),
        compiler_params=_params(("parallel", "parallel", "parallel"), 2 * tr * tc * 6),
        name="cast_pad",
    )(w)


def _pad_last(w, total):
    return jnp.pad(w, ((0, 0),) * (w.ndim - 1) + ((0, total - w.shape[-1]),))


def _bf16_weights(w_in, w_out, w_gate, w_up, ffn_conv, w_down, gw, dn_heads):
    o_dn_a = 10 * gw
    o_fx = o_dn_a + 2 * dn_heads
    o_fx_f = o_fx + 3 * gw
    tile2 = o_fx_f // _V7X_LANES * _V7X_LANES
    narrow = jnp.concatenate([w_in[..., o_dn_a:o_dn_a + _V7X_LANES], _pad_last(w_in[..., tile2:], _V7X_LANES)],
                             axis=-1).astype(_BF16)
    ff = w_gate.shape[-1]
    ffp = -(-ff // _FF_ALIGN) * _FF_ALIGN if ff > _FF_ALIGN else ff
    return dict(
        wide_a=w_in[..., :o_dn_a].astype(_BF16), wide_b=w_in[..., o_fx:o_fx_f].astype(_BF16),
        narrow=narrow, fox_f_lane=_V7X_LANES + o_fx_f - tile2, out=w_out.astype(_BF16),
        gate=_cast_pad(w_gate, w_gate.shape[1], ffp, _BF16), up=_cast_pad(w_up, w_up.shape[1], ffp, _BF16),
        conv=_pad_last(ffn_conv.astype(_F32), ffp),
        down=_cast_pad(w_down, ffp, w_down.shape[2], _BF16))


def _layer(x, layer, batch, seq_len, wts, attn_norm, sc_conv, lam_vecs, diff_norm, dn_conv, dn_a_log,
           dn_dt_bias, dn_norm, fox_bias, ffn_norm):
    d = lam_vecs[0].shape[-1]
    gw = wts["out"].shape[1] // 4
    dn_heads, fox_heads = dn_a_log.shape[-1], fox_bias.shape[-1]
    diff_heads = gw // diff_norm.shape[-1]
    col = lambda idx: idx * gw

    h = _rmsnorm(x, attn_norm, _BF16)
    p = _matmul([h], wts["wide_a"], layer, _BF16, name="in_proj")
    p_fox = _matmul([h], wts["wide_b"], layer, _BF16, name="in_proj_fox")
    gates = _matmul([h], wts["narrow"], layer, _F32, name="in_proj_gates")

    y_sc = _short_conv(p, sc_conv, seq_len, 0, 1, 2, gw)
    lam_init = 0.8 - 0.6 * math.exp(-0.3 * layer)
    y_df = _diff_attention(p, lam_vecs, diff_norm, lam_init, batch, seq_len, col(3), col(4), col(5), diff_heads, d)

    gates_t = gates.reshape(batch, seq_len, gates.shape[1]).transpose(0, 2, 1)
    f0 = wts["fox_f_lane"]
    fox_rows = gates_t[:, f0:f0 + fox_heads].reshape(batch * fox_heads, seq_len)
    fox_bias_rows = jnp.tile(fox_bias.astype(_F32), batch).reshape(batch * fox_heads, 1)
    cum = _fox_cumsum(fox_rows, fox_bias_rows)
    y_fx = _fox_attention(p_fox, cum, batch, seq_len, col(0), col(1), col(2), fox_heads, d)

    y_dn = _gated_deltanet(p, gates, gates_t[:, :2 * dn_heads], dn_conv, dn_a_log, dn_dt_bias, dn_norm,
                           batch, seq_len, col(6), col(7), col(8), col(9), dn_heads, d)

    x = _matmul([y_sc, y_df, y_dn, y_fx], wts["out"], layer, _F32, residual=x, tn_target=512, name="out_proj")

    h2 = _rmsnorm(x, ffn_norm, _BF16)
    act = _ffn_in(h2, wts["gate"], wts["up"], wts["conv"], layer, seq_len)
    return _matmul_acc(act, wts["down"], layer, x, name="ffn_out")


def kernel(x, attn_norm, w_in, sc_conv, lam_q1, lam_k1, lam_q2, lam_k2, diff_norm, dn_conv, dn_a_log,
           dn_dt_bias, dn_norm, fox_bias, w_out, ffn_norm, w_gate, w_up, ffn_conv, w_down, final_norm):
    batch, seq_len, d_model = x.shape
    xf = x.reshape(batch * seq_len, d_model).astype(_F32)
    wts = _bf16_weights(w_in, w_out, w_gate, w_up, ffn_conv, w_down, w_out.shape[1] // 4, dn_a_log.shape[-1])
    for l in range(w_in.shape[0]):
        xf = _layer(xf, l, batch, seq_len, wts, attn_norm[l], sc_conv[l],
                    (lam_q1[l], lam_k1[l], lam_q2[l], lam_k2[l]), diff_norm[l], dn_conv[l], dn_a_log[l],
                    dn_dt_bias[l], dn_norm[l], fox_bias[l], ffn_norm[l])
    out = _rmsnorm(xf, final_norm, x.dtype)
    return out.reshape(batch, seq_len, d_model)
```

```python
import functools
import math

import jax
import jax.numpy as jnp
from jax import lax
from jax.experimental import pallas as pl
from jax.experimental.pallas import tpu as pltpu

_F32 = jnp.float32
_BF16 = jnp.bfloat16
_EPS = 1e-6
_NEG_INF = float("-inf")

_V7X_VMEM_BYTES = 64 * 1024 * 1024
_V7X_LANES = 128
_V7X_SUBLANES = 8
_BF16_ROWS = 16
_DN_CHUNK = 128
_FF_ALIGN = 1024


def _tile(n, target, align):
    t = min(target, n)
    t -= t % align
    while t >= align:
        if n % t == 0:
            return t
        t -= align
    return n


def _params(semantics, vmem_bytes):
    limit = min(int(vmem_bytes) + (8 << 20), _V7X_VMEM_BYTES - (4 << 20))
    return pltpu.CompilerParams(dimension_semantics=semantics, vmem_limit_bytes=limit)


def _silu(x):
    return x * (1.0 / (1.0 + jnp.exp(-x)))


def _softplus(x):
    return jnp.maximum(x, 0.0) + jnp.log1p(jnp.exp(-jnp.abs(x)))


def _shift_rows(x, prev8, s):
    xs = pltpu.roll(x, s, 0)
    ps = pltpu.roll(prev8, s, 0)
    row = lax.broadcasted_iota(jnp.int32, prev8.shape, 0)
    head = jnp.where(row < s, ps, xs[:_V7X_SUBLANES])
    return jnp.concatenate([head, xs[_V7X_SUBLANES:]], axis=0)


def _causal_conv(x, prev8, w):
    k = w.shape[0]
    out = x * w[k - 1:k]
    for j in range(k - 1):
        out = out + _shift_rows(x, prev8, k - 1 - j) * w[j:j + 1]
    return out


def _rmsnorm_kernel(x_ref, g_ref, o_ref):
    x = x_ref[...]
    y = x * lax.rsqrt(jnp.mean(x * x, axis=-1, keepdims=True) + _EPS)
    o_ref[...] = (y * g_ref[...]).astype(o_ref.dtype)


def _rmsnorm(x, g, out_dtype):
    n, d = x.shape
    tm = _tile(n, 512, _BF16_ROWS)
    vmem = 2 * tm * d * (4 + jnp.dtype(out_dtype).itemsize) + 2 * d * 4
    return pl.pallas_call(
        _rmsnorm_kernel,
        grid=(n // tm,),
        in_specs=[pl.BlockSpec((tm, d), lambda i: (i, 0)), pl.BlockSpec((1, d), lambda i: (0, 0))],
        out_specs=pl.BlockSpec((tm, d), lambda i: (i, 0)),
        out_shape=jax.ShapeDtypeStruct((n, d), out_dtype),
        compiler_params=_params(("parallel",), vmem),
        name="rmsnorm",
    )(x, g.reshape(1, d).astype(_F32))


def _mm_kernel(*refs, n_pairs, has_res):
    a_refs, w_refs = refs[:n_pairs], refs[n_pairs:2 * n_pairs]
    o_ref = refs[-1]
    acc = jnp.dot(a_refs[0][...], w_refs[0][...], preferred_element_type=_F32)
    for a_ref, w_ref in zip(a_refs[1:], w_refs[1:]):
        acc = acc + jnp.dot(a_ref[...], w_ref[...], preferred_element_type=_F32)
    if has_res:
        acc = acc + refs[2 * n_pairs][...]
    o_ref[...] = acc.astype(o_ref.dtype)


def _matmul(a_list, w, layer, out_dtype, residual=None, n_cols=None, tm_target=1024, tn_target=1024,
            name="matmul"):
    m, n = a_list[0].shape[0], n_cols or w.shape[2]
    kb = a_list[0].shape[1]
    assert all(a.shape[1] == kb for a in a_list) and w.shape[1] == kb * len(a_list)
    tm = _tile(m, tm_target, _BF16_ROWS)
    tn = _tile(n, tn_target, _V7X_LANES)
    ktot = w.shape[1]
    in_specs = [pl.BlockSpec((tm, kb), lambda i, j: (i, 0)) for _ in a_list]
    in_specs += [pl.BlockSpec((None, kb, tn), functools.partial(lambda i, j, r: (layer, r, j), r=r))
                 for r in range(len(a_list))]
    args = list(a_list) + [w] * len(a_list)
    vmem = 2 * 2 * ktot * (tm + tn) + 2 * tm * tn * jnp.dtype(out_dtype).itemsize + tm * tn * 4
    if residual is not None:
        in_specs.append(pl.BlockSpec((tm, tn), lambda i, j: (i, j)))
        args.append(residual)
        vmem += 2 * tm * tn * 4
    return pl.pallas_call(
        functools.partial(_mm_kernel, n_pairs=len(a_list), has_res=residual is not None),
        grid=(m // tm, n // tn),
        in_specs=in_specs,
        out_specs=pl.BlockSpec((tm, tn), lambda i, j: (i, j)),
        out_shape=jax.ShapeDtypeStruct((m, n), out_dtype),
        compiler_params=_params(("parallel", "parallel"), vmem),
        name=name,
    )(*args)


def _mm_acc_kernel(a_ref, w_ref, r_ref, o_ref, acc_ref, *, nk):
    k = pl.program_id(2)
    prod = lambda: jnp.dot(a_ref[...], w_ref[...], preferred_element_type=_F32)
    if nk == 1:
        o_ref[...] = r_ref[...] + prod()
        return

    @pl.when(k == 0)
    def _():
        acc_ref[...] = r_ref[...] + prod()

    @pl.when(jnp.logical_and(k > 0, k < nk - 1))
    def _():
        acc_ref[...] += prod()

    @pl.when(k == nk - 1)
    def _():
        o_ref[...] = acc_ref[...] + prod()


def _matmul_acc(a, w, layer, residual, tm_target=1024, tn_target=1024, tk_target=2816, name="matmul_acc"):
    m, kdim = a.shape
    n = w.shape[2]
    tm = _tile(m, tm_target, _BF16_ROWS)
    tn = _tile(n, tn_target, _V7X_LANES)
    tk = _tile(kdim, tk_target, _V7X_LANES)
    vmem = 2 * 2 * tk * (tm + tn) + 5 * tm * tn * 4
    return pl.pallas_call(
        functools.partial(_mm_acc_kernel, nk=kdim // tk),
        grid=(m // tm, n // tn, kdim // tk),
        in_specs=[pl.BlockSpec((tm, tk), lambda i, j, k: (i, k)),
                  pl.BlockSpec((None, tk, tn), lambda i, j, k: (layer, k, j)),
                  pl.BlockSpec((tm, tn), lambda i, j, k: (i, j))],
        out_specs=pl.BlockSpec((tm, tn), lambda i, j, k: (i, j)),
        out_shape=jax.ShapeDtypeStruct((m, n), _F32),
        scratch_shapes=[pltpu.VMEM((tm, tn), _F32)],
        compiler_params=_params(("parallel", "parallel", "arbitrary"), vmem),
        name=name,
    )(a, w, residual)


def _sconv_kernel(h_ref, c_ref, b_ref, hh_ref, ch_ref, w_ref, o_ref, *, blocks_per_seq):
    i = pl.program_id(0)
    x = c_ref[...].astype(_F32) * h_ref[...].astype(_F32)
    halo = ch_ref[...].astype(_F32) * hh_ref[...].astype(_F32)
    prev8 = jnp.where(i % blocks_per_seq == 0, 0.0, halo[_BF16_ROWS - _V7X_SUBLANES:])
    y = b_ref[...].astype(_F32) * _causal_conv(x, prev8, w_ref[...])
    o_ref[...] = y.astype(o_ref.dtype)


def _short_conv(p, conv_w, seq_len, col_h, col_c, col_b, width):
    n = p.shape[0]
    tm = _tile(seq_len, 512, _BF16_ROWS)
    hb = tm // _BF16_ROWS
    cur = lambda c: pl.BlockSpec((tm, width), lambda i: (i, c))
    halo = lambda c: pl.BlockSpec((_BF16_ROWS, width), lambda i: (jnp.maximum(i * hb - 1, 0), c))
    k = conv_w.shape[0]
    vmem = 2 * (4 * tm + 2 * _BF16_ROWS) * width * 2 + 8 * tm * width * 4
    return pl.pallas_call(
        functools.partial(_sconv_kernel, blocks_per_seq=seq_len // tm),
        grid=(n // tm,),
        in_specs=[cur(col_h), cur(col_c), cur(col_b), halo(col_h), halo(col_c),
                  pl.BlockSpec((k, width), lambda i: (0, 0))],
        out_specs=pl.BlockSpec((tm, width), lambda i: (i, 0)),
        out_shape=jax.ShapeDtypeStruct((n, width), _BF16),
        compiler_params=_params(("parallel",), vmem),
        name="short_conv",
    )(p, p, p, p, p, conv_w.astype(_F32))


_LOG2E = math.log2(math.e)


def _lane_tiles(x, width):
    return [x[:, c:c + _V7X_LANES] for c in range(0, width, _V7X_LANES)]


def _lane_repeat(x, width):
    return jnp.concatenate([x] * (width // _V7X_LANES), axis=1)


def _softmax_block(s, m_prev):
    m_new = jnp.maximum(m_prev, jnp.max(s, axis=-1, keepdims=True))
    alpha = jnp.exp2(m_prev - m_new)
    p = jnp.concatenate([jnp.exp2(t - m_new) for t in _lane_tiles(s, s.shape[1])], axis=1)
    return m_new, alpha, p


def _causal_mask(s, q0, k0):
    row = q0 + lax.broadcasted_iota(jnp.int32, s.shape, 0)
    col = k0 + lax.broadcasted_iota(jnp.int32, s.shape, 1)
    return jnp.where(col <= row, s, _NEG_INF)


def _qk(q, k):
    return lax.dot_general(q, k, (((1,), (1,)), ((), ())), preferred_element_type=_F32)


def _flash_blocks(seq_len, kv_per_q):
    bq = _tile(seq_len, 1024, _V7X_LANES)
    bk = bq // kv_per_q if bq % (kv_per_q * _V7X_LANES) == 0 else bq
    return bq, bk


def _resident_spec(seq_len, width, col):
    return pl.BlockSpec((seq_len, width), lambda b, h, i: (b, col // width + h), pipeline_mode=pl.Buffered(1))


def _kv_sweep(qi, r, bk, step):
    def trip(i, carry):
        for d in range(r):
            step(i * r + d, False, 0)
        return carry

    lax.fori_loop(0, qi, trip, 0)
    for d in range(r):
        step(qi * r + d, True, d * bk)


def _diff_kernel(q_ref, k_ref, v_ref, lq1_ref, lk1_ref, lq2_ref, lk2_ref, g_ref, o_ref,
                 m_ref, l_ref, acc_ref, *, bq, bk, d, hp, scale, lam_init):
    qi = pl.program_id(2)
    hw = 2 * d
    m_ref[...] = jnp.full(m_ref.shape, _NEG_INF, _F32)
    l_ref[...] = jnp.zeros(l_ref.shape, _F32)
    acc_ref[...] = jnp.zeros(acc_ref.shape, _F32)
    q = (q_ref[...].astype(_F32) * (scale * _LOG2E)).astype(_BF16)

    def step(j, masked, row0):
        k0 = pl.multiple_of(j * bk, bk)
        k = k_ref[pl.ds(k0, bk), :]
        v = v_ref[pl.ds(k0, bk), :]
        rows = slice(row0, bq)
        for c in range(2 * hp):
            cols = slice(c * d, (c + 1) * d)
            s = _qk(q[rows, cols], k[:, cols])
            if masked:
                s = _causal_mask(s, qi * bq + row0, k0)
            m_new, alpha, p = _softmax_block(s, m_ref[c, rows])
            l_ref[c, rows] = alpha * l_ref[c, rows] + jnp.sum(p, axis=-1, keepdims=True)
            vh = v[:, (c // 2) * hw:(c // 2 + 1) * hw]
            acc_ref[c, rows] = (_lane_repeat(alpha, hw) * acc_ref[c, rows]
                                + jnp.dot(p.astype(_BF16), vh, preferred_element_type=_F32))
            m_ref[c, rows] = m_new

    _kv_sweep(qi, bq // bk, bk, step)
    lam = (jnp.exp(jnp.sum(lq1_ref[...] * lk1_ref[...], axis=-1, keepdims=True))
           - jnp.exp(jnp.sum(lq2_ref[...] * lk2_ref[...], axis=-1, keepdims=True)) + lam_init)
    for h in range(hp):
        o = (acc_ref[2 * h] / _lane_repeat(l_ref[2 * h], hw)
             - lam * (acc_ref[2 * h + 1] / _lane_repeat(l_ref[2 * h + 1], hw)))
        o = o * lax.rsqrt(jnp.mean(o * o, axis=-1, keepdims=True) + _EPS) * g_ref[...]
        o_ref[:, h * hw:(h + 1) * hw] = (o * (1.0 - lam_init)).astype(o_ref.dtype)


def _diff_attention(p, lam_vecs, norm_g, lam_init, batch, seq_len, col_q, col_k, col_v, heads, d):
    n = p.shape[0]
    hw = 2 * d
    bq, bk = _flash_blocks(seq_len, 2)
    nq = seq_len // bq
    hp = 2 if heads % 2 == 0 else 1
    w = hp * hw
    vec = pl.BlockSpec((1, d), lambda b, h, i: (0, 0))
    vmem = 2 * seq_len * w * 2 + 4 * bq * w * 2 + 2 * hp * bq * (hw + 2 * _V7X_LANES) * 4 + 12 * bq * bk * 4
    return pl.pallas_call(
        functools.partial(_diff_kernel, bq=bq, bk=bk, d=d, hp=hp, scale=d ** -0.5, lam_init=lam_init),
        grid=(batch, heads // hp, nq),
        in_specs=[pl.BlockSpec((bq, w), lambda b, h, i: (b * nq + i, col_q // w + h)),
                  _resident_spec(seq_len, w, col_k), _resident_spec(seq_len, w, col_v),
                  vec, vec, vec, vec,
                  pl.BlockSpec((1, hw), lambda b, h, i: (0, 0))],
        out_specs=pl.BlockSpec((bq, w), lambda b, h, i: (b * nq + i, h)),
        out_shape=jax.ShapeDtypeStruct((n, heads * hw), _BF16),
        scratch_shapes=[pltpu.VMEM((2 * hp, bq, _V7X_LANES), _F32), pltpu.VMEM((2 * hp, bq, _V7X_LANES), _F32),
                        pltpu.VMEM((2 * hp, bq, hw), _F32)],
        compiler_params=_params(("parallel", "parallel", "arbitrary"), vmem),
        name="diff_attention",
    )(p, p, p, *[v.reshape(1, d).astype(_F32) for v in lam_vecs], norm_g.reshape(1, hw).astype(_F32))


def _fox_cum_kernel(f_ref, bias_ref, o_ref):
    x = f_ref[...] + bias_ref[...]
    ls = jnp.minimum(x, 0.0) - jnp.log1p(jnp.exp(-jnp.abs(x)))
    r = lax.broadcasted_iota(jnp.int32, (_V7X_LANES, _V7X_LANES), 0)
    c = lax.broadcasted_iota(jnp.int32, (_V7X_LANES, _V7X_LANES), 1)
    upper = (r <= c).astype(_F32)
    carry = jnp.zeros((x.shape[0], 1), _F32)
    for t in range(x.shape[1] // _V7X_LANES):
        sl = slice(t * _V7X_LANES, (t + 1) * _V7X_LANES)
        w = jnp.dot(ls[:, sl], upper, precision=lax.Precision.HIGHEST, preferred_element_type=_F32) + carry
        o_ref[:, sl] = w
        carry = w[:, _V7X_LANES - 1:]


def _fox_cumsum(f_rows, bias_rows):
    return pl.pallas_call(
        _fox_cum_kernel,
        out_shape=jax.ShapeDtypeStruct(f_rows.shape, _F32),
        name="fox_cumsum",
    )(f_rows, bias_rows)


def _fox_kernel(q_ref, k_ref, v_ref, cq_ref, ck_ref, o_ref, m_ref, acc_ref, *, bq, bk, d, hp, scale):
    qi = pl.program_id(2)
    m_ref[...] = jnp.full(m_ref.shape, _NEG_INF, _F32)
    acc_ref[...] = jnp.zeros(acc_ref.shape, _F32)
    q = (q_ref[...].astype(_F32) * (scale * _LOG2E)).astype(_BF16)
    cq = [jnp.broadcast_to(cq_ref[0, h] * _LOG2E, (bq, _V7X_LANES)) for h in range(hp)]
    ones = jnp.ones((bk, _V7X_LANES), _BF16)

    def step(j, masked, row0):
        k0 = pl.multiple_of(j * bk, bk)
        rows = slice(row0, bq)
        k = k_ref[pl.ds(k0, bk), :]
        v = v_ref[pl.ds(k0, bk), :]
        for h in range(hp):
            cols = slice(h * d, (h + 1) * d)
            s = _qk(q[rows, cols], k[:, cols])
            ck = ck_ref[0, h, pl.ds(j, 1), :] * _LOG2E
            s = jnp.concatenate([st + (cq[h][rows] - ct)
                                 for st, ct in zip(_lane_tiles(s, bk), _lane_tiles(ck, bk))], axis=1)
            if masked:
                s = _causal_mask(s, qi * bq + row0, k0)
            m_new, alpha, p = _softmax_block(s, m_ref[h, rows])
            v_ones = jnp.concatenate([v[:, cols], ones], axis=1)
            acc_ref[h, rows] = (_lane_repeat(alpha, d + _V7X_LANES) * acc_ref[h, rows]
                                + jnp.dot(p.astype(_BF16), v_ones, preferred_element_type=_F32))
            m_ref[h, rows] = m_new

    _kv_sweep(qi, bq // bk, bk, step)
    for h in range(hp):
        acc = acc_ref[h]
        o_ref[:, h * d:(h + 1) * d] = (acc[:, :d] / _lane_repeat(acc[:, d:], d)).astype(o_ref.dtype)


def _fox_attention(p, cum, batch, seq_len, col_q, col_k, col_v, heads, d):
    n = p.shape[0]
    bq, bk = _flash_blocks(seq_len, 4)
    nq, nk = seq_len // bq, seq_len // bk
    hp = 4 if heads % 4 == 0 else 1
    w = hp * d
    cum_rows = cum.reshape(batch, heads, nk, bk)
    cum_cols = cum.reshape(batch, heads, seq_len, 1)
    vmem = (2 * seq_len * w * 2 + 4 * bq * w * 2 + 2 * hp * seq_len * 4 + 2 * hp * bq * _V7X_LANES * 4
            + hp * bq * (d + 2 * _V7X_LANES) * 4 + 12 * bq * bk * 4)
    return pl.pallas_call(
        functools.partial(_fox_kernel, bq=bq, bk=bk, d=d, hp=hp, scale=d ** -0.5),
        grid=(batch, heads // hp, nq),
        in_specs=[pl.BlockSpec((bq, w), lambda b, h, i: (b * nq + i, col_q // w + h)),
                  _resident_spec(seq_len, w, col_k), _resident_spec(seq_len, w, col_v),
                  pl.BlockSpec((1, hp, bq, 1), lambda b, h, i: (b, h, i, 0)),
                  pl.BlockSpec((1, hp, nk, bk), lambda b, h, i: (b, h, 0, 0))],
        out_specs=pl.BlockSpec((bq, w), lambda b, h, i: (b * nq + i, h)),
        out_shape=jax.ShapeDtypeStruct((n, heads * d), _BF16),
        scratch_shapes=[pltpu.VMEM((hp, bq, _V7X_LANES), _F32), pltpu.VMEM((hp, bq, d + _V7X_LANES), _F32)],
        compiler_params=_params(("parallel", "parallel", "arbitrary"), vmem),
        name="fox_attention",
    )(p, p, p, cum_cols, cum_rows)


def _lane_column(x, h):
    lane = lax.broadcasted_iota(jnp.int32, x.shape, 1)
    return jnp.sum(jnp.where(lane == h, x, 0.0), axis=-1, keepdims=True)


def _bdot(a, b):
    return jnp.dot(a.astype(_BF16), b.astype(_BF16), preferred_element_type=_F32)


_INV_BASE_LOG2 = 4


def _unit_lower_inverses(mats, ri, ci):
    c = mats[0].shape[0]
    same_block = lambda log2n: (ri >> log2n) == (ci >> log2n)
    eye = (ri == ci).astype(_F32)
    base = same_block(_INV_BASE_LOG2)
    pw = [jnp.where(base, -a, 0.0) for a in mats]
    tinv = [eye + x for x in pw]
    for _ in range(_INV_BASE_LOG2 - 1):
        pw = [_bdot(x, x) for x in pw]
        tinv = [t + _bdot(t, x) for t, x in zip(tinv, pw)]
    log2n = _INV_BASE_LOG2
    while (1 << log2n) < c:
        level = same_block(log2n + 1) & jnp.logical_not(same_block(log2n))
        off = [_bdot(t, jnp.where(level, a, 0.0)) for t, a in zip(tinv, mats)]
        tinv = [t - _bdot(o, t) for t, o in zip(tinv, off)]
        log2n += 1
    return tinv


def _deltanet_kernel(q_ref, k_ref, v_ref, z_ref, gc_ref, gr_ref, cw_ref, alog_c_ref, dtb_c_ref,
                     alog_r_ref, dtb_r_ref, ng_ref, o_ref, s_ref, tail_ref, *, nb, heads, d, c):
    t = pl.program_id(0)
    width = heads * d

    @pl.when(t == 0)
    def _():
        s_ref[...] = jnp.zeros(s_ref.shape, _F32)
        tail_ref[...] = jnp.zeros(tail_ref.shape, _F32)

    ri = lax.broadcasted_iota(jnp.int32, (c, c), 0)
    ci = lax.broadcasted_iota(jnp.int32, (c, c), 1)
    incl = ri >= ci
    strict = ri > ci
    lower = incl.astype(_F32)
    upper = (ri <= ci).astype(_F32)
    hp = lax.Precision.HIGHEST
    cw = cw_ref[...]
    unit = lambda x: x * lax.rsqrt(jnp.sum(x * x, axis=-1, keepdims=True) + _EPS)

    q, k, v, gc, beta, gcr = [], [], [], [], [], []
    for b in range(nb):
        conv = []
        for idx, ref in enumerate((q_ref, k_ref, v_ref)):
            raw = ref[b].astype(_F32)
            cols = slice(idx * width, (idx + 1) * width)
            conv.append(_silu(_causal_conv(raw, tail_ref[b, :, cols], cw[:, cols])))
            tail_ref[b, :, cols] = raw[c - _V7X_SUBLANES:]
        gates_c = gc_ref[b]
        g_cols = -jnp.exp(alog_c_ref[...]) * _softplus(gates_c + dtb_c_ref[...])
        beta_cols = 1.0 / (1.0 + jnp.exp(-gates_c))
        g_rows = -jnp.exp(alog_r_ref[...]) * _softplus(gr_ref[b][:heads] + dtb_r_ref[...])
        gcum_cols = jnp.dot(lower, g_cols, precision=hp, preferred_element_type=_F32)
        gcum_rows = jnp.dot(g_rows, upper, precision=hp, preferred_element_type=_F32)
        for h in range(heads):
            hs = slice(h * d, (h + 1) * d)
            q.append(unit(conv[0][:, hs]) * (d ** -0.5))
            k.append(unit(conv[1][:, hs]))
            v.append(conv[2][:, hs])
            gc.append(_lane_column(gcum_cols, h))
            beta.append(_lane_column(beta_cols, heads + h))
            gcr.append(gcum_rows[h:h + 1])
    chains = range(nb * heads)
    decay = [jnp.exp(jnp.where(incl, gc[n] - gcr[n], _NEG_INF)) for n in chains]
    k16 = [x.astype(_BF16) for x in k]
    kb = [k[n] * beta[n] for n in chains]
    a = [jnp.where(strict, _qk(kb[n].astype(_BF16), k16[n]) * decay[n], 0.0) for n in chains]
    tinv = _unit_lower_inverses(a, ri, ci)
    egc = [jnp.exp(x) for x in gc]
    wu = [_bdot(tinv[n], jnp.concatenate([kb[n] * egc[n], v[n] * beta[n]], axis=1)) for n in chains]
    aqk = [_qk(q[n].astype(_BF16), k16[n]) * decay[n] for n in chains]
    s = [s_ref[n // heads, n % heads] for n in chains]
    ws_qs = [_bdot(jnp.concatenate([wu[n][:, :d], q[n] * egc[n]], axis=0), s[n]) for n in chains]
    v_new = [wu[n][:, d:] - ws_qs[n][:c] for n in chains]
    o = [ws_qs[n][c:] + _bdot(aqk[n], v_new[n]) for n in chains]
    g_last = [x[c - 1:c] for x in gc]
    k_dec = [k[n] * jnp.exp(g_last[n] - gc[n]) for n in chains]
    for n in chains:
        s_ref[n // heads, n % heads] = s[n] * jnp.exp(g_last[n]) + lax.dot_general(
            k_dec[n].astype(_BF16), v_new[n].astype(_BF16), (((0,), (0,)), ((), ())), preferred_element_type=_F32)
    ng = ng_ref[...]
    for n in chains:
        hs = slice((n % heads) * d, (n % heads + 1) * d)
        on = o[n] * lax.rsqrt(jnp.mean(o[n] * o[n], axis=-1, keepdims=True) + _EPS) * ng
        o_ref[n // heads, :, hs] = (on * _silu(z_ref[n // heads, :, hs].astype(_F32))).astype(o_ref.dtype)


def _gated_deltanet(p, gates_cols, gates_rows, conv_w, a_log, dt_bias, norm_g, batch, seq_len,
                    col_q, col_k, col_v, col_z, heads, d):
    n = p.shape[0]
    c = _tile(seq_len, _DN_CHUNK, _BF16_ROWS)
    nt = seq_len // c
    width = heads * d
    lanes = _V7X_LANES
    kw = conv_w.shape[0]
    p3 = p.reshape(batch, seq_len, p.shape[1])
    col = lambda off: pl.BlockSpec((batch, c, width), lambda t: (0, t, off // width))
    const = lambda shape: pl.BlockSpec(shape, lambda t: (0,) * len(shape))
    pad_lane = lambda v: jnp.zeros((1, lanes), _F32).at[0, :heads].set(v.astype(_F32))
    vmem = batch * (2 * 5 * c * width * 2 + 14 * c * width * 4 + heads * d * d * 4) + 2 * kw * 3 * width * 4
    out = pl.pallas_call(
        functools.partial(_deltanet_kernel, nb=batch, heads=heads, d=d, c=c),
        grid=(nt,),
        in_specs=[col(col_q), col(col_k), col(col_v), col(col_z),
                  pl.BlockSpec((batch, c, lanes), lambda t: (0, t, 0)),
                  pl.BlockSpec((batch, gates_rows.shape[1], c), lambda t: (0, 0, t)),
                  const((kw, 3 * width)), const((1, lanes)), const((1, lanes)),
                  const((heads, 1)), const((heads, 1)), const((1, d))],
        out_specs=pl.BlockSpec((batch, c, width), lambda t: (0, t, 0)),
        out_shape=jax.ShapeDtypeStruct((batch, seq_len, width), _BF16),
        scratch_shapes=[pltpu.VMEM((batch, heads, d, d), _F32),
                        pltpu.VMEM((batch, _V7X_SUBLANES, 3 * width), _F32)],
        compiler_params=_params(("arbitrary",), vmem),
        name="gated_deltanet",
    )(p3, p3, p3, p3, gates_cols.reshape(batch, seq_len, gates_cols.shape[1]), gates_rows, conv_w.astype(_F32),
      pad_lane(a_log), pad_lane(dt_bias), a_log.reshape(heads, 1).astype(_F32),
      dt_bias.reshape(heads, 1).astype(_F32), norm_g.reshape(1, d).astype(_F32))
    return out.reshape(n, width)


def _ffn_in_kernel(h_ref, wg_ref, wu_ref, cw_ref, o_ref, tail_ref, *, blocks_per_seq):
    i, j = pl.program_id(0), pl.program_id(1)
    tm = o_ref.shape[0]

    @pl.when(i % blocks_per_seq == 0)
    def _():
        tail_ref[j] = jnp.zeros(tail_ref.shape[1:], _F32)

    h = h_ref[...]
    g = jnp.dot(h, wg_ref[...], preferred_element_type=_F32)
    sg = _silu(_causal_conv(g, tail_ref[j], cw_ref[...]))
    tail_ref[j] = g[tm - _V7X_SUBLANES:]
    u = jnp.dot(h, wu_ref[...], preferred_element_type=_F32)
    o_ref[...] = (sg * u).astype(o_ref.dtype)


def _ffn_in(h, w_gate, w_up, conv_w, layer, seq_len, tm_target=1024, tn_target=512):
    n, d = h.shape
    ff = w_gate.shape[2]
    tm = _tile(seq_len, tm_target, _BF16_ROWS)
    tn = _tile(ff, tn_target, _V7X_LANES)
    kw = conv_w.shape[1]
    wspec = pl.BlockSpec((None, d, tn), lambda i, j: (layer, 0, j))
    vmem = 2 * 2 * d * (tm + 2 * tn) + 2 * tm * tn * 2 + 6 * tm * tn * 4 + ff * _V7X_SUBLANES * 4
    return pl.pallas_call(
        functools.partial(_ffn_in_kernel, blocks_per_seq=seq_len // tm),
        grid=(n // tm, ff // tn),
        in_specs=[pl.BlockSpec((tm, d), lambda i, j: (i, 0)), wspec, wspec,
                  pl.BlockSpec((None, kw, tn), lambda i, j: (layer, 0, j))],
        out_specs=pl.BlockSpec((tm, tn), lambda i, j: (i, j)),
        out_shape=jax.ShapeDtypeStruct((n, ff), _BF16),
        scratch_shapes=[pltpu.VMEM((ff // tn, _V7X_SUBLANES, tn), _F32)],
        compiler_params=_params(("arbitrary", "arbitrary"), vmem),
        name="ffn_in",
    )(h, w_gate, w_up, conv_w)


def _cast_pad_kernel(x_ref, o_ref, *, rows, cols):
    i, j = pl.program_id(1), pl.program_id(2)
    tr, tc = o_ref.shape
    r = i * tr + lax.broadcasted_iota(jnp.int32, (tr, tc), 0)
    c = j * tc + lax.broadcasted_iota(jnp.int32, (tr, tc), 1)
    o_ref[...] = jnp.where(jnp.logical_and(r < rows, c < cols), x_ref[...], 0.0).astype(o_ref.dtype)


def _cast_pad(w, rows_to, cols_to, dtype):
    layers, rows, cols = w.shape
    tr = _tile(rows_to, 1024, _BF16_ROWS)
    tc = _tile(cols_to, 1024, _V7X_LANES)
    assert rows_to - rows < tr and cols_to - cols < tc
    return pl.pallas_call(
        functools.partial(_cast_pad_kernel, rows=rows, cols=cols),
        grid=(layers, rows_to // tr, cols_to // tc),
        in_specs=[pl.BlockSpec((None, tr, tc), lambda l, i, j: (l, i, j))],
        out_specs=pl.BlockSpec((None, tr, tc), lambda l, i, j: (l, i, j)),
        out_shape=jax.ShapeDtypeStruct((layers, rows_to, cols_to), dtype),
        compiler_params=_params(("parallel", "parallel", "parallel"), 2 * tr * tc * 6),
        name="cast_pad",
    )(w)


def _pad_last(w, total):
    return jnp.pad(w, ((0, 0),) * (w.ndim - 1) + ((0, total - w.shape[-1]),))


def _bf16_weights(w_in, w_out, w_gate, w_up, ffn_conv, w_down, gw, dn_heads):
    o_dn_a = 10 * gw
    o_fx = o_dn_a + 2 * dn_heads
    o_fx_f = o_fx + 3 * gw
    tile2 = o_fx_f // _V7X_LANES * _V7X_LANES
    narrow = jnp.concatenate([w_in[..., o_dn_a:o_dn_a + _V7X_LANES], _pad_last(w_in[..., tile2:], _V7X_LANES)],
                             axis=-1).astype(_BF16)
    ff = w_gate.shape[-1]
    ffp = -(-ff // _FF_ALIGN) * _FF_ALIGN if ff > _FF_ALIGN else ff
    return dict(
        w_in=w_in.astype(_BF16), n_wide_a=o_dn_a, wide_b=w_in[..., o_fx:o_fx_f].astype(_BF16),
        narrow=narrow, fox_f_lane=_V7X_LANES + o_fx_f - tile2, out=w_out.astype(_BF16),
        gate=_cast_pad(w_gate, w_gate.shape[1], ffp, _BF16), up=_cast_pad(w_up, w_up.shape[1], ffp, _BF16),
        conv=_pad_last(ffn_conv.astype(_F32), ffp),
        down=_cast_pad(w_down, ffp, w_down.shape[2], _BF16))


def _layer(x, layer, batch, seq_len, wts, attn_norm, sc_conv, lam_vecs, diff_norm, dn_conv, dn_a_log,
           dn_dt_bias, dn_norm, fox_bias, ffn_norm):
    d = lam_vecs[0].shape[-1]
    gw = wts["out"].shape[1] // 4
    dn_heads, fox_heads = dn_a_log.shape[-1], fox_bias.shape[-1]
    diff_heads = gw // diff_norm.shape[-1]
    col = lambda idx: idx * gw

    h = _rmsnorm(x, attn_norm, _BF16)
    p = _matmul([h], wts["w_in"], layer, _BF16, n_cols=wts["n_wide_a"], name="in_proj")
    p_fox = _matmul([h], wts["wide_b"], layer, _BF16, name="in_proj_fox")
    gates = _matmul([h], wts["narrow"], layer, _F32, name="in_proj_gates")

    y_sc = _short_conv(p, sc_conv, seq_len, 0, 1, 2, gw)
    lam_init = 0.8 - 0.6 * math.exp(-0.3 * layer)
    y_df = _diff_attention(p, lam_vecs, diff_norm, lam_init, batch, seq_len, col(3), col(4), col(5), diff_heads, d)

    gates_t = gates.reshape(batch, seq_len, gates.shape[1]).transpose(0, 2, 1)
    f0 = wts["fox_f_lane"]
    fox_rows = gates_t[:, f0:f0 + fox_heads].reshape(batch * fox_heads, seq_len)
    fox_bias_rows = jnp.tile(fox_bias.astype(_F32), batch).reshape(batch * fox_heads, 1)
    cum = _fox_cumsum(fox_rows, fox_bias_rows)
    y_fx = _fox_attention(p_fox, cum, batch, seq_len, col(0), col(1), col(2), fox_heads, d)

    y_dn = _gated_deltanet(p, gates, gates_t[:, :2 * dn_heads], dn_conv, dn_a_log, dn_dt_bias, dn_norm,
                           batch, seq_len, col(6), col(7), col(8), col(9), dn_heads, d)

    x = _matmul([y_sc, y_df, y_dn, y_fx], wts["out"], layer, _F32, residual=x, tn_target=512, name="out_proj")

    h2 = _rmsnorm(x, ffn_norm, _BF16)
    act = _ffn_in(h2, wts["gate"], wts["up"], wts["conv"], layer, seq_len)
    return _matmul_acc(act, wts["down"], layer, x, name="ffn_out")


def kernel(x, attn_norm, w_in, sc_conv, lam_q1, lam_k1, lam_q2, lam_k2, diff_norm, dn_conv, dn_a_log,
           dn_dt_bias, dn_norm, fox_bias, w_out, ffn_norm, w_gate, w_up, ffn_conv, w_down, final_norm):
    batch, seq_len, d_model = x.shape
    xf = x.reshape(batch * seq_len, d_model).astype(_F32)
    wts = _bf16_weights(w_in, w_out, w_gate, w_up, ffn_conv, w_down, w_out.shape[1] // 4, dn_a_log.shape[-1])
    for l in range(w_in.shape[0]):
        xf = _layer(xf, l, batch, seq_len, wts, attn_norm[l], sc_conv[l],
                    (lam_q1[l], lam_k1[l], lam_q2[l], lam_k2[l]), diff_norm[l], dn_conv[l], dn_a_log[l],
                    dn_dt_bias[l], dn_norm[l], fox_bias[l], ffn_norm[l])
    out = _rmsnorm(xf, final_norm, x.dtype)
    return out.reshape(batch, seq_len, d_model)
```

```python
import functools
import math

import jax
import jax.numpy as jnp
from jax import lax
from jax.experimental import pallas as pl
from jax.experimental.pallas import tpu as pltpu

_F32 = jnp.float32
_BF16 = jnp.bfloat16
_EPS = 1e-6
_NEG_INF = float("-inf")

_V7X_VMEM_BYTES = 64 * 1024 * 1024
_V7X_LANES = 128
_V7X_SUBLANES = 8
_BF16_ROWS = 16
_DN_CHUNK = 128
_FF_ALIGN = 1024


def _tile(n, target, align):
    t = min(target, n)
    t -= t % align
    while t >= align:
        if n % t == 0:
            return t
        t -= align
    return n


def _params(semantics, vmem_bytes):
    limit = min(int(vmem_bytes) + (8 << 20), _V7X_VMEM_BYTES - (4 << 20))
    return pltpu.CompilerParams(dimension_semantics=semantics, vmem_limit_bytes=limit)


def _silu(x):
    return x * (1.0 / (1.0 + jnp.exp(-x)))


def _softplus(x):
    return jnp.maximum(x, 0.0) + jnp.log1p(jnp.exp(-jnp.abs(x)))


def _shift_rows(x, prev8, s):
    xs = pltpu.roll(x, s, 0)
    ps = pltpu.roll(prev8, s, 0)
    row = lax.broadcasted_iota(jnp.int32, prev8.shape, 0)
    head = jnp.where(row < s, ps, xs[:_V7X_SUBLANES])
    return jnp.concatenate([head, xs[_V7X_SUBLANES:]], axis=0)


def _causal_conv(x, prev8, w):
    k = w.shape[0]
    out = x * w[k - 1:k]
    for j in range(k - 1):
        out = out + _shift_rows(x, prev8, k - 1 - j) * w[j:j + 1]
    return out


def _rmsnorm_kernel(x_ref, g_ref, o_ref):
    x = x_ref[...]
    y = x * lax.rsqrt(jnp.mean(x * x, axis=-1, keepdims=True) + _EPS)
    o_ref[...] = (y * g_ref[...]).astype(o_ref.dtype)


def _rmsnorm(x, g, out_dtype):
    n, d = x.shape
    tm = _tile(n, 512, _BF16_ROWS)
    vmem = 2 * tm * d * (4 + jnp.dtype(out_dtype).itemsize) + 2 * d * 4
    return pl.pallas_call(
        _rmsnorm_kernel,
        grid=(n // tm,),
        in_specs=[pl.BlockSpec((tm, d), lambda i: (i, 0)), pl.BlockSpec((1, d), lambda i: (0, 0))],
        out_specs=pl.BlockSpec((tm, d), lambda i: (i, 0)),
        out_shape=jax.ShapeDtypeStruct((n, d), out_dtype),
        compiler_params=_params(("parallel",), vmem),
        name="rmsnorm",
    )(x, g.reshape(1, d).astype(_F32))


def _stream_stats_kernel(x_ref, xb_ref, ssq_ref):
    x = x_ref[...]
    xb_ref[...] = x.astype(xb_ref.dtype)
    ssq_ref[...] = jnp.broadcast_to(jnp.sum(x * x, axis=-1, keepdims=True), ssq_ref.shape)


def _stream_stats(x):
    n, d = x.shape
    tm = _tile(n, 512, _BF16_ROWS)
    return pl.pallas_call(
        _stream_stats_kernel,
        grid=(n // tm,),
        in_specs=[pl.BlockSpec((tm, d), lambda i: (i, 0))],
        out_specs=[pl.BlockSpec((tm, d), lambda i: (i, 0)), pl.BlockSpec((tm, _V7X_LANES), lambda i: (i, 0))],
        out_shape=[jax.ShapeDtypeStruct((n, d), _BF16), jax.ShapeDtypeStruct((n, _V7X_LANES), _F32)],
        compiler_params=_params(("parallel",), 2 * tm * d * 6),
        name="stream_stats",
    )(x)


def _inv_rms(ssq, width):
    return lax.rsqrt(ssq * (1.0 / width) + _EPS)


def _emit_stream_stats(x_new, first, xb_ref, ssq_ref):
    xb_ref[...] = x_new.astype(xb_ref.dtype)
    part = jnp.sum(x_new * x_new, axis=-1, keepdims=True)

    @pl.when(first)
    def _():
        ssq_ref[...] = jnp.broadcast_to(part, ssq_ref.shape)

    @pl.when(jnp.logical_not(first))
    def _():
        ssq_ref[...] += part


def _mm_kernel(*refs, n_pairs, has_res, has_ssq, emit_stats):
    a_refs, w_refs = refs[:n_pairs], refs[n_pairs:2 * n_pairs]
    extra = list(refs[2 * n_pairs:])
    res_ref = extra.pop(0) if has_res else None
    ssq_in_ref = extra.pop(0) if has_ssq else None
    o_ref = extra.pop(0)
    acc = jnp.dot(a_refs[0][...], w_refs[0][...], preferred_element_type=_F32)
    for a_ref, w_ref in zip(a_refs[1:], w_refs[1:]):
        acc = acc + jnp.dot(a_ref[...], w_ref[...], preferred_element_type=_F32)
    if has_ssq:
        acc = acc * _lane_repeat(_inv_rms(ssq_in_ref[...], a_refs[0].shape[1]), acc.shape[1])
    if has_res:
        acc = acc + res_ref[...]
    o_ref[...] = acc.astype(o_ref.dtype)
    if emit_stats:
        _emit_stream_stats(acc, pl.program_id(1) == 0, *extra)


def _matmul(a_list, w, layer, out_dtype, residual=None, row_ssq=None, emit_stats=False, n_cols=None,
            tm_target=1024, tn_target=1024, name="matmul"):
    m, n = a_list[0].shape[0], n_cols or w.shape[2]
    kb = a_list[0].shape[1]
    assert all(a.shape[1] == kb for a in a_list) and w.shape[1] == kb * len(a_list)
    assert row_ssq is None or len(a_list) == 1
    tm = _tile(m, tm_target, _BF16_ROWS)
    tn = _tile(n, tn_target, _V7X_LANES)
    ktot = w.shape[1]
    in_specs = [pl.BlockSpec((tm, kb), lambda i, j: (i, 0)) for _ in a_list]
    in_specs += [pl.BlockSpec((None, kb, tn), functools.partial(lambda i, j, r: (layer, r, j), r=r))
                 for r in range(len(a_list))]
    args = list(a_list) + [w] * len(a_list)
    vmem = 2 * 2 * ktot * (tm + tn) + 2 * tm * tn * jnp.dtype(out_dtype).itemsize + tm * tn * 4
    if residual is not None:
        in_specs.append(pl.BlockSpec((tm, tn), lambda i, j: (i, j)))
        args.append(residual)
        vmem += 2 * tm * tn * 4
    stats_spec = pl.BlockSpec((tm, _V7X_LANES), lambda i, j: (i, 0))
    if row_ssq is not None:
        in_specs.append(stats_spec)
        args.append(row_ssq)
    out_specs = pl.BlockSpec((tm, tn), lambda i, j: (i, j))
    out_shape = jax.ShapeDtypeStruct((m, n), out_dtype)
    if emit_stats:
        out_specs = [out_specs, pl.BlockSpec((tm, tn), lambda i, j: (i, j)), stats_spec]
        out_shape = [out_shape, jax.ShapeDtypeStruct((m, n), _BF16), jax.ShapeDtypeStruct((m, _V7X_LANES), _F32)]
        vmem += 2 * tm * tn * 2
    return pl.pallas_call(
        functools.partial(_mm_kernel, n_pairs=len(a_list), has_res=residual is not None,
                          has_ssq=row_ssq is not None, emit_stats=emit_stats),
        grid=(m // tm, n // tn),
        in_specs=in_specs,
        out_specs=out_specs,
        out_shape=out_shape,
        compiler_params=_params(("parallel", "arbitrary" if emit_stats else "parallel"), vmem),
        name=name,
    )(*args)


def _mm_acc_kernel(a_ref, w_ref, r_ref, o_ref, *rest, nk, emit_stats):
    acc_ref = rest[-1]
    k = pl.program_id(2)
    prod = lambda: jnp.dot(a_ref[...], w_ref[...], preferred_element_type=_F32)

    def finish(x_new):
        o_ref[...] = x_new
        if emit_stats:
            _emit_stream_stats(x_new, pl.program_id(1) == 0, rest[0], rest[1])

    if nk == 1:
        finish(r_ref[...] + prod())
        return

    @pl.when(k == 0)
    def _():
        acc_ref[...] = r_ref[...] + prod()

    @pl.when(jnp.logical_and(k > 0, k < nk - 1))
    def _():
        acc_ref[...] += prod()

    @pl.when(k == nk - 1)
    def _():
        finish(acc_ref[...] + prod())


def _matmul_acc(a, w, layer, residual, emit_stats=False, tm_target=1024, tn_target=1024, tk_target=2816,
                name="matmul_acc"):
    m, kdim = a.shape
    n = w.shape[2]
    tm = _tile(m, tm_target, _BF16_ROWS)
    tn = _tile(n, tn_target, _V7X_LANES)
    tk = _tile(kdim, tk_target, _V7X_LANES)
    vmem = 2 * 2 * tk * (tm + tn) + 5 * tm * tn * 4
    tile_spec = pl.BlockSpec((tm, tn), lambda i, j, k: (i, j))
    out_specs, out_shape = tile_spec, jax.ShapeDtypeStruct((m, n), _F32)
    if emit_stats:
        out_specs = [tile_spec, tile_spec, pl.BlockSpec((tm, _V7X_LANES), lambda i, j, k: (i, 0))]
        out_shape = [out_shape, jax.ShapeDtypeStruct((m, n), _BF16), jax.ShapeDtypeStruct((m, _V7X_LANES), _F32)]
        vmem += 2 * tm * tn * 2
    return pl.pallas_call(
        functools.partial(_mm_acc_kernel, nk=kdim // tk, emit_stats=emit_stats),
        grid=(m // tm, n // tn, kdim // tk),
        in_specs=[pl.BlockSpec((tm, tk), lambda i, j, k: (i, k)),
                  pl.BlockSpec((None, tk, tn), lambda i, j, k: (layer, k, j)),
                  tile_spec],
        out_specs=out_specs,
        out_shape=out_shape,
        scratch_shapes=[pltpu.VMEM((tm, tn), _F32)],
        compiler_params=_params(("parallel", "arbitrary" if emit_stats else "parallel", "arbitrary"), vmem),
        name=name,
    )(a, w, residual)


def _sconv_kernel(h_ref, c_ref, b_ref, hh_ref, ch_ref, w_ref, o_ref, *, blocks_per_seq):
    i = pl.program_id(0)
    x = c_ref[...].astype(_F32) * h_ref[...].astype(_F32)
    halo = ch_ref[...].astype(_F32) * hh_ref[...].astype(_F32)
    prev8 = jnp.where(i % blocks_per_seq == 0, 0.0, halo[_BF16_ROWS - _V7X_SUBLANES:])
    y = b_ref[...].astype(_F32) * _causal_conv(x, prev8, w_ref[...])
    o_ref[...] = y.astype(o_ref.dtype)


def _short_conv(p, conv_w, seq_len, col_h, col_c, col_b, width):
    n = p.shape[0]
    tm = _tile(seq_len, 512, _BF16_ROWS)
    hb = tm // _BF16_ROWS
    cur = lambda c: pl.BlockSpec((tm, width), lambda i: (i, c))
    halo = lambda c: pl.BlockSpec((_BF16_ROWS, width), lambda i: (jnp.maximum(i * hb - 1, 0), c))
    k = conv_w.shape[0]
    vmem = 2 * (4 * tm + 2 * _BF16_ROWS) * width * 2 + 8 * tm * width * 4
    return pl.pallas_call(
        functools.partial(_sconv_kernel, blocks_per_seq=seq_len // tm),
        grid=(n // tm,),
        in_specs=[cur(col_h), cur(col_c), cur(col_b), halo(col_h), halo(col_c),
                  pl.BlockSpec((k, width), lambda i: (0, 0))],
        out_specs=pl.BlockSpec((tm, width), lambda i: (i, 0)),
        out_shape=jax.ShapeDtypeStruct((n, width), _BF16),
        compiler_params=_params(("parallel",), vmem),
        name="short_conv",
    )(p, p, p, p, p, conv_w.astype(_F32))


_LOG2E = math.log2(math.e)


def _lane_tiles(x, width):
    return [x[:, c:c + _V7X_LANES] for c in range(0, width, _V7X_LANES)]


def _lane_repeat(x, width):
    return jnp.concatenate([x] * (width // _V7X_LANES), axis=1)


def _softmax_block(s, m_prev):
    m_new = jnp.maximum(m_prev, jnp.max(s, axis=-1, keepdims=True))
    alpha = jnp.exp2(m_prev - m_new)
    p = jnp.concatenate([jnp.exp2(t - m_new) for t in _lane_tiles(s, s.shape[1])], axis=1)
    return m_new, alpha, p


def _causal_mask(s, q0, k0):
    row = q0 + lax.broadcasted_iota(jnp.int32, s.shape, 0)
    col = k0 + lax.broadcasted_iota(jnp.int32, s.shape, 1)
    return jnp.where(col <= row, s, _NEG_INF)


def _qk(q, k):
    return lax.dot_general(q, k, (((1,), (1,)), ((), ())), preferred_element_type=_F32)


def _flash_blocks(seq_len, kv_per_q):
    bq = _tile(seq_len, 1024, _V7X_LANES)
    bk = bq // kv_per_q if bq % (kv_per_q * _V7X_LANES) == 0 else bq
    return bq, bk


def _resident_spec(seq_len, width, col):
    return pl.BlockSpec((seq_len, width), lambda b, h, i: (b, col // width + h), pipeline_mode=pl.Buffered(1))


def _kv_sweep(qi, r, bk, step):
    def trip(i, carry):
        for d in range(r):
            step(i * r + d, False, 0)
        return carry

    lax.fori_loop(0, qi, trip, 0)
    for d in range(r):
        step(qi * r + d, True, d * bk)


def _diff_kernel(q_ref, k_ref, v_ref, lq1_ref, lk1_ref, lq2_ref, lk2_ref, g_ref, o_ref,
                 m_ref, l_ref, acc_ref, *, bq, bk, d, hp, scale, lam_init):
    qi = pl.program_id(2)
    hw = 2 * d
    m_ref[...] = jnp.full(m_ref.shape, _NEG_INF, _F32)
    l_ref[...] = jnp.zeros(l_ref.shape, _F32)
    acc_ref[...] = jnp.zeros(acc_ref.shape, _F32)
    q = (q_ref[...].astype(_F32) * (scale * _LOG2E)).astype(_BF16)

    def step(j, masked, row0):
        k0 = pl.multiple_of(j * bk, bk)
        k = k_ref[pl.ds(k0, bk), :]
        v = v_ref[pl.ds(k0, bk), :]
        rows = slice(row0, bq)
        for c in range(2 * hp):
            cols = slice(c * d, (c + 1) * d)
            s = _qk(q[rows, cols], k[:, cols])
            if masked:
                s = _causal_mask(s, qi * bq + row0, k0)
            m_new, alpha, p = _softmax_block(s, m_ref[c, rows])
            l_ref[c, rows] = alpha * l_ref[c, rows] + jnp.sum(p, axis=-1, keepdims=True)
            vh = v[:, (c // 2) * hw:(c // 2 + 1) * hw]
            acc_ref[c, rows] = (_lane_repeat(alpha, hw) * acc_ref[c, rows]
                                + jnp.dot(p.astype(_BF16), vh, preferred_element_type=_F32))
            m_ref[c, rows] = m_new

    _kv_sweep(qi, bq // bk, bk, step)
    lam = (jnp.exp(jnp.sum(lq1_ref[...] * lk1_ref[...], axis=-1, keepdims=True))
           - jnp.exp(jnp.sum(lq2_ref[...] * lk2_ref[...], axis=-1, keepdims=True)) + lam_init)
    for h in range(hp):
        o = (acc_ref[2 * h] / _lane_repeat(l_ref[2 * h], hw)
             - lam * (acc_ref[2 * h + 1] / _lane_repeat(l_ref[2 * h + 1], hw)))
        o = o * lax.rsqrt(jnp.mean(o * o, axis=-1, keepdims=True) + _EPS) * g_ref[...]
        o_ref[:, h * hw:(h + 1) * hw] = (o * (1.0 - lam_init)).astype(o_ref.dtype)


def _diff_attention(p, lam_vecs, norm_g, lam_init, batch, seq_len, col_q, col_k, col_v, heads, d):
    n = p.shape[0]
    hw = 2 * d
    bq, bk = _flash_blocks(seq_len, 2)
    nq = seq_len // bq
    hp = 2 if heads % 2 == 0 else 1
    w = hp * hw
    vec = pl.BlockSpec((1, d), lambda b, h, i: (0, 0))
    vmem = 2 * seq_len * w * 2 + 4 * bq * w * 2 + 2 * hp * bq * (hw + 2 * _V7X_LANES) * 4 + 12 * bq * bk * 4
    return pl.pallas_call(
        functools.partial(_diff_kernel, bq=bq, bk=bk, d=d, hp=hp, scale=d ** -0.5, lam_init=lam_init),
        grid=(batch, heads // hp, nq),
        in_specs=[pl.BlockSpec((bq, w), lambda b, h, i: (b * nq + i, col_q // w + h)),
                  _resident_spec(seq_len, w, col_k), _resident_spec(seq_len, w, col_v),
                  vec, vec, vec, vec,
                  pl.BlockSpec((1, hw), lambda b, h, i: (0, 0))],
        out_specs=pl.BlockSpec((bq, w), lambda b, h, i: (b * nq + i, h)),
        out_shape=jax.ShapeDtypeStruct((n, heads * hw), _BF16),
        scratch_shapes=[pltpu.VMEM((2 * hp, bq, _V7X_LANES), _F32), pltpu.VMEM((2 * hp, bq, _V7X_LANES), _F32),
                        pltpu.VMEM((2 * hp, bq, hw), _F32)],
        compiler_params=_params(("parallel", "parallel", "arbitrary"), vmem),
        name="diff_attention",
    )(p, p, p, *[v.reshape(1, d).astype(_F32) for v in lam_vecs], norm_g.reshape(1, hw).astype(_F32))


def _fox_cum_kernel(f_ref, bias_ref, o_ref):
    x = f_ref[...] + bias_ref[...]
    ls = jnp.minimum(x, 0.0) - jnp.log1p(jnp.exp(-jnp.abs(x)))
    r = lax.broadcasted_iota(jnp.int32, (_V7X_LANES, _V7X_LANES), 0)
    c = lax.broadcasted_iota(jnp.int32, (_V7X_LANES, _V7X_LANES), 1)
    upper = (r <= c).astype(_F32)
    carry = jnp.zeros((x.shape[0], 1), _F32)
    for t in range(x.shape[1] // _V7X_LANES):
        sl = slice(t * _V7X_LANES, (t + 1) * _V7X_LANES)
        w = jnp.dot(ls[:, sl], upper, precision=lax.Precision.HIGHEST, preferred_element_type=_F32) + carry
        o_ref[:, sl] = w
        carry = w[:, _V7X_LANES - 1:]


def _fox_cumsum(f_rows, bias_rows):
    return pl.pallas_call(
        _fox_cum_kernel,
        out_shape=jax.ShapeDtypeStruct(f_rows.shape, _F32),
        name="fox_cumsum",
    )(f_rows, bias_rows)


def _fox_kernel(q_ref, k_ref, v_ref, cq_ref, ck_ref, o_ref, m_ref, acc_ref, *, bq, bk, d, hp, scale):
    qi = pl.program_id(2)
    m_ref[...] = jnp.full(m_ref.shape, _NEG_INF, _F32)
    acc_ref[...] = jnp.zeros(acc_ref.shape, _F32)
    q = (q_ref[...].astype(_F32) * (scale * _LOG2E)).astype(_BF16)
    cq = [jnp.broadcast_to(cq_ref[0, h] * _LOG2E, (bq, _V7X_LANES)) for h in range(hp)]
    ones = jnp.ones((bk, _V7X_LANES), _BF16)

    def step(j, masked, row0):
        k0 = pl.multiple_of(j * bk, bk)
        rows = slice(row0, bq)
        k = k_ref[pl.ds(k0, bk), :]
        v = v_ref[pl.ds(k0, bk), :]
        for h in range(hp):
            cols = slice(h * d, (h + 1) * d)
            s = _qk(q[rows, cols], k[:, cols])
            ck = ck_ref[0, h, pl.ds(j, 1), :] * _LOG2E
            s = jnp.concatenate([st + (cq[h][rows] - ct)
                                 for st, ct in zip(_lane_tiles(s, bk), _lane_tiles(ck, bk))], axis=1)
            if masked:
                s = _causal_mask(s, qi * bq + row0, k0)
            m_new, alpha, p = _softmax_block(s, m_ref[h, rows])
            v_ones = jnp.concatenate([v[:, cols], ones], axis=1)
            acc_ref[h, rows] = (_lane_repeat(alpha, d + _V7X_LANES) * acc_ref[h, rows]
                                + jnp.dot(p.astype(_BF16), v_ones, preferred_element_type=_F32))
            m_ref[h, rows] = m_new

    _kv_sweep(qi, bq // bk, bk, step)
    for h in range(hp):
        acc = acc_ref[h]
        o_ref[:, h * d:(h + 1) * d] = (acc[:, :d] / _lane_repeat(acc[:, d:], d)).astype(o_ref.dtype)


def _fox_attention(p, cum, batch, seq_len, col_q, col_k, col_v, heads, d):
    n = p.shape[0]
    bq, bk = _flash_blocks(seq_len, 4)
    nq, nk = seq_len // bq, seq_len // bk
    hp = 4 if heads % 4 == 0 else 1
    w = hp * d
    cum_rows = cum.reshape(batch, heads, nk, bk)
    cum_cols = cum.reshape(batch, heads, seq_len, 1)
    vmem = (2 * seq_len * w * 2 + 4 * bq * w * 2 + 2 * hp * seq_len * 4 + 2 * hp * bq * _V7X_LANES * 4
            + hp * bq * (d + 2 * _V7X_LANES) * 4 + 12 * bq * bk * 4)
    return pl.pallas_call(
        functools.partial(_fox_kernel, bq=bq, bk=bk, d=d, hp=hp, scale=d ** -0.5),
        grid=(batch, heads // hp, nq),
        in_specs=[pl.BlockSpec((bq, w), lambda b, h, i: (b * nq + i, col_q // w + h)),
                  _resident_spec(seq_len, w, col_k), _resident_spec(seq_len, w, col_v),
                  pl.BlockSpec((1, hp, bq, 1), lambda b, h, i: (b, h, i, 0)),
                  pl.BlockSpec((1, hp, nk, bk), lambda b, h, i: (b, h, 0, 0))],
        out_specs=pl.BlockSpec((bq, w), lambda b, h, i: (b * nq + i, h)),
        out_shape=jax.ShapeDtypeStruct((n, heads * d), _BF16),
        scratch_shapes=[pltpu.VMEM((hp, bq, _V7X_LANES), _F32), pltpu.VMEM((hp, bq, d + _V7X_LANES), _F32)],
        compiler_params=_params(("parallel", "parallel", "arbitrary"), vmem),
        name="fox_attention",
    )(p, p, p, cum_cols, cum_rows)


def _lane_column(x, h):
    lane = lax.broadcasted_iota(jnp.int32, x.shape, 1)
    return jnp.sum(jnp.where(lane == h, x, 0.0), axis=-1, keepdims=True)


def _bdot(a, b):
    return jnp.dot(a.astype(_BF16), b.astype(_BF16), preferred_element_type=_F32)


_INV_BASE_LOG2 = 4


def _unit_lower_inverses(mats, ri, ci):
    c = mats[0].shape[0]
    same_block = lambda log2n: (ri >> log2n) == (ci >> log2n)
    eye = (ri == ci).astype(_F32)
    base = same_block(_INV_BASE_LOG2)
    pw = [jnp.where(base, -a, 0.0) for a in mats]
    tinv = [eye + x for x in pw]
    for _ in range(_INV_BASE_LOG2 - 1):
        pw = [_bdot(x, x) for x in pw]
        tinv = [t + _bdot(t, x) for t, x in zip(tinv, pw)]
    log2n = _INV_BASE_LOG2
    while (1 << log2n) < c:
        level = same_block(log2n + 1) & jnp.logical_not(same_block(log2n))
        off = [_bdot(t, jnp.where(level, a, 0.0)) for t, a in zip(tinv, mats)]
        tinv = [t - _bdot(o, t) for t, o in zip(tinv, off)]
        log2n += 1
    return tinv


def _deltanet_kernel(q_ref, k_ref, v_ref, z_ref, gc_ref, gr_ref, cw_ref, alog_c_ref, dtb_c_ref,
                     alog_r_ref, dtb_r_ref, ng_ref, o_ref, s_ref, tail_ref, *, nb, heads, d, c):
    t = pl.program_id(0)
    width = heads * d

    @pl.when(t == 0)
    def _():
        s_ref[...] = jnp.zeros(s_ref.shape, _F32)
        tail_ref[...] = jnp.zeros(tail_ref.shape, _F32)

    ri = lax.broadcasted_iota(jnp.int32, (c, c), 0)
    ci = lax.broadcasted_iota(jnp.int32, (c, c), 1)
    incl = ri >= ci
    strict = ri > ci
    lower = incl.astype(_F32)
    upper = (ri <= ci).astype(_F32)
    hp = lax.Precision.HIGHEST
    cw = cw_ref[...]
    unit = lambda x: x * lax.rsqrt(jnp.sum(x * x, axis=-1, keepdims=True) + _EPS)

    q, k, v, gc, beta, gcr = [], [], [], [], [], []
    for b in range(nb):
        conv = []
        for idx, ref in enumerate((q_ref, k_ref, v_ref)):
            raw = ref[b].astype(_F32)
            cols = slice(idx * width, (idx + 1) * width)
            conv.append(_silu(_causal_conv(raw, tail_ref[b, :, cols], cw[:, cols])))
            tail_ref[b, :, cols] = raw[c - _V7X_SUBLANES:]
        gates_c = gc_ref[b]
        g_cols = -jnp.exp(alog_c_ref[...]) * _softplus(gates_c + dtb_c_ref[...])
        beta_cols = 1.0 / (1.0 + jnp.exp(-gates_c))
        g_rows = -jnp.exp(alog_r_ref[...]) * _softplus(gr_ref[b][:heads] + dtb_r_ref[...])
        gcum_cols = jnp.dot(lower, g_cols, precision=hp, preferred_element_type=_F32)
        gcum_rows = jnp.dot(g_rows, upper, precision=hp, preferred_element_type=_F32)
        for h in range(heads):
            hs = slice(h * d, (h + 1) * d)
            q.append(unit(conv[0][:, hs]) * (d ** -0.5))
            k.append(unit(conv[1][:, hs]))
            v.append(conv[2][:, hs])
            gc.append(_lane_column(gcum_cols, h))
            beta.append(_lane_column(beta_cols, heads + h))
            gcr.append(gcum_rows[h:h + 1])
    chains = range(nb * heads)
    decay = [jnp.exp(jnp.where(incl, gc[n] - gcr[n], _NEG_INF)) for n in chains]
    k16 = [x.astype(_BF16) for x in k]
    kb = [k[n] * beta[n] for n in chains]
    a = [jnp.where(strict, _qk(kb[n].astype(_BF16), k16[n]) * decay[n], 0.0) for n in chains]
    tinv = _unit_lower_inverses(a, ri, ci)
    egc = [jnp.exp(x) for x in gc]
    wu = [_bdot(tinv[n], jnp.concatenate([kb[n] * egc[n], v[n] * beta[n]], axis=1)) for n in chains]
    aqk = [_qk(q[n].astype(_BF16), k16[n]) * decay[n] for n in chains]
    s = [s_ref[n // heads, n % heads] for n in chains]
    ws_qs = [_bdot(jnp.concatenate([wu[n][:, :d], q[n] * egc[n]], axis=0), s[n]) for n in chains]
    v_new = [wu[n][:, d:] - ws_qs[n][:c] for n in chains]
    o = [ws_qs[n][c:] + _bdot(aqk[n], v_new[n]) for n in chains]
    g_last = [x[c - 1:c] for x in gc]
    k_dec = [k[n] * jnp.exp(g_last[n] - gc[n]) for n in chains]
    for n in chains:
        s_ref[n // heads, n % heads] = s[n] * jnp.exp(g_last[n]) + lax.dot_general(
            k_dec[n].astype(_BF16), v_new[n].astype(_BF16), (((0,), (0,)), ((), ())), preferred_element_type=_F32)
    ng = ng_ref[...]
    for n in chains:
        hs = slice((n % heads) * d, (n % heads + 1) * d)
        on = o[n] * lax.rsqrt(jnp.mean(o[n] * o[n], axis=-1, keepdims=True) + _EPS) * ng
        o_ref[n // heads, :, hs] = (on * _silu(z_ref[n // heads, :, hs].astype(_F32))).astype(o_ref.dtype)


def _gated_deltanet(p, gates_cols, gates_rows, conv_w, a_log, dt_bias, norm_g, batch, seq_len,
                    col_q, col_k, col_v, col_z, heads, d):
    n = p.shape[0]
    c = _tile(seq_len, _DN_CHUNK, _BF16_ROWS)
    nt = seq_len // c
    width = heads * d
    lanes = _V7X_LANES
    kw = conv_w.shape[0]
    p3 = p.reshape(batch, seq_len, p.shape[1])
    col = lambda off: pl.BlockSpec((batch, c, width), lambda t: (0, t, off // width))
    const = lambda shape: pl.BlockSpec(shape, lambda t: (0,) * len(shape))
    pad_lane = lambda v: jnp.zeros((1, lanes), _F32).at[0, :heads].set(v.astype(_F32))
    vmem = batch * (2 * 5 * c * width * 2 + 14 * c * width * 4 + heads * d * d * 4) + 2 * kw * 3 * width * 4
    out = pl.pallas_call(
        functools.partial(_deltanet_kernel, nb=batch, heads=heads, d=d, c=c),
        grid=(nt,),
        in_specs=[col(col_q), col(col_k), col(col_v), col(col_z),
                  pl.BlockSpec((batch, c, lanes), lambda t: (0, t, 0)),
                  pl.BlockSpec((batch, gates_rows.shape[1], c), lambda t: (0, 0, t)),
                  const((kw, 3 * width)), const((1, lanes)), const((1, lanes)),
                  const((heads, 1)), const((heads, 1)), const((1, d))],
        out_specs=pl.BlockSpec((batch, c, width), lambda t: (0, t, 0)),
        out_shape=jax.ShapeDtypeStruct((batch, seq_len, width), _BF16),
        scratch_shapes=[pltpu.VMEM((batch, heads, d, d), _F32),
                        pltpu.VMEM((batch, _V7X_SUBLANES, 3 * width), _F32)],
        compiler_params=_params(("arbitrary",), vmem),
        name="gated_deltanet",
    )(p3, p3, p3, p3, gates_cols.reshape(batch, seq_len, gates_cols.shape[1]), gates_rows, conv_w.astype(_F32),
      pad_lane(a_log), pad_lane(dt_bias), a_log.reshape(heads, 1).astype(_F32),
      dt_bias.reshape(heads, 1).astype(_F32), norm_g.reshape(1, d).astype(_F32))
    return out.reshape(n, width)


def _ffn_in_kernel(h_ref, ssq_ref, wg_ref, wu_ref, cw_ref, o_ref, tail_ref, *, blocks_per_seq):
    i, j = pl.program_id(0), pl.program_id(1)
    tm, tn = o_ref.shape

    @pl.when(i % blocks_per_seq == 0)
    def _():
        tail_ref[j] = jnp.zeros(tail_ref.shape[1:], _F32)

    h = h_ref[...]
    inv_rms = _lane_repeat(_inv_rms(ssq_ref[...], h.shape[1]), tn)
    g = jnp.dot(h, wg_ref[...], preferred_element_type=_F32) * inv_rms
    sg = _silu(_causal_conv(g, tail_ref[j], cw_ref[...])) * inv_rms
    tail_ref[j] = g[tm - _V7X_SUBLANES:]
    u = jnp.dot(h, wu_ref[...], preferred_element_type=_F32)
    o_ref[...] = (sg * u).astype(o_ref.dtype)


def _ffn_in(h, row_ssq, w_gate, w_up, conv_w, layer, seq_len, tm_target=1024, tn_target=512):
    n, d = h.shape
    ff = w_gate.shape[2]
    tm = _tile(seq_len, tm_target, _BF16_ROWS)
    tn = _tile(ff, tn_target, _V7X_LANES)
    kw = conv_w.shape[1]
    wspec = pl.BlockSpec((None, d, tn), lambda i, j: (layer, 0, j))
    vmem = 2 * 2 * d * (tm + 2 * tn) + 2 * tm * tn * 2 + 6 * tm * tn * 4 + ff * _V7X_SUBLANES * 4
    return pl.pallas_call(
        functools.partial(_ffn_in_kernel, blocks_per_seq=seq_len // tm),
        grid=(n // tm, ff // tn),
        in_specs=[pl.BlockSpec((tm, d), lambda i, j: (i, 0)),
                  pl.BlockSpec((tm, _V7X_LANES), lambda i, j: (i, 0)), wspec, wspec,
                  pl.BlockSpec((None, kw, tn), lambda i, j: (layer, 0, j))],
        out_specs=pl.BlockSpec((tm, tn), lambda i, j: (i, j)),
        out_shape=jax.ShapeDtypeStruct((n, ff), _BF16),
        scratch_shapes=[pltpu.VMEM((ff // tn, _V7X_SUBLANES, tn), _F32)],
        compiler_params=_params(("arbitrary", "arbitrary"), vmem),
        name="ffn_in",
    )(h, row_ssq, w_gate, w_up, conv_w)


def _cast_pad_kernel(x_ref, *refs, rows, cols):
    o_ref = refs[-1]
    i, j = pl.program_id(1), pl.program_id(2)
    tr, tc = o_ref.shape
    r = i * tr + lax.broadcasted_iota(jnp.int32, (tr, tc), 0)
    c = j * tc + lax.broadcasted_iota(jnp.int32, (tr, tc), 1)
    x = x_ref[...] if len(refs) == 1 else x_ref[...] * refs[0][...]
    o_ref[...] = jnp.where(jnp.logical_and(r < rows, c < cols), x, 0.0).astype(o_ref.dtype)


def _cast_pad(w, rows_to, cols_to, dtype, row_gain=None):
    layers, rows, cols = w.shape
    tr = _tile(rows_to, 1024, _BF16_ROWS)
    tc = _tile(cols_to, 1024, _V7X_LANES)
    assert rows_to - rows < tr and cols_to - cols < tc
    in_specs, args = [pl.BlockSpec((None, tr, tc), lambda l, i, j: (l, i, j))], [w]
    if row_gain is not None:
        assert rows_to == rows
        in_specs.append(pl.BlockSpec((None, tr, 1), lambda l, i, j: (l, i, 0)))
        args.append(row_gain.astype(_F32).reshape(layers, rows, 1))
    return pl.pallas_call(
        functools.partial(_cast_pad_kernel, rows=rows, cols=cols),
        grid=(layers, rows_to // tr, cols_to // tc),
        in_specs=in_specs,
        out_specs=pl.BlockSpec((None, tr, tc), lambda l, i, j: (l, i, j)),
        out_shape=jax.ShapeDtypeStruct((layers, rows_to, cols_to), dtype),
        compiler_params=_params(("parallel", "parallel", "parallel"), 2 * tr * tc * 6 + 2 * tr * _V7X_LANES * 4),
        name="cast_pad",
    )(*args)


def _pad_last(w, total):
    return jnp.pad(w, ((0, 0),) * (w.ndim - 1) + ((0, total - w.shape[-1]),))


def _bf16_weights(w_in, w_out, w_gate, w_up, ffn_conv, w_down, attn_norm, ffn_norm, gw, dn_heads):
    w_in = w_in * attn_norm.astype(_F32)[:, :, None]
    o_dn_a = 10 * gw
    o_fx = o_dn_a + 2 * dn_heads
    o_fx_f = o_fx + 3 * gw
    tile2 = o_fx_f // _V7X_LANES * _V7X_LANES
    narrow = jnp.concatenate([w_in[..., o_dn_a:o_dn_a + _V7X_LANES], _pad_last(w_in[..., tile2:], _V7X_LANES)],
                             axis=-1).astype(_BF16)
    ff = w_gate.shape[-1]
    ffp = -(-ff // _FF_ALIGN) * _FF_ALIGN if ff > _FF_ALIGN else ff
    return dict(
        w_in=w_in.astype(_BF16), n_wide_a=o_dn_a, wide_b=w_in[..., o_fx:o_fx_f].astype(_BF16),
        narrow=narrow, fox_f_lane=_V7X_LANES + o_fx_f - tile2, out=w_out.astype(_BF16),
        gate=_cast_pad(w_gate, w_gate.shape[1], ffp, _BF16, ffn_norm),
        up=_cast_pad(w_up, w_up.shape[1], ffp, _BF16, ffn_norm),
        conv=_pad_last(ffn_conv.astype(_F32), ffp),
        down=_cast_pad(w_down, ffp, w_down.shape[2], _BF16))


def _layer(stream, layer, last, batch, seq_len, wts, sc_conv, lam_vecs, diff_norm, dn_conv, dn_a_log,
           dn_dt_bias, dn_norm, fox_bias):
    x, h, ssq = stream
    d = lam_vecs[0].shape[-1]
    gw = wts["out"].shape[1] // 4
    dn_heads, fox_heads = dn_a_log.shape[-1], fox_bias.shape[-1]
    diff_heads = gw // diff_norm.shape[-1]
    col = lambda idx: idx * gw

    p = _matmul([h], wts["w_in"], layer, _BF16, row_ssq=ssq, n_cols=wts["n_wide_a"], name="in_proj")
    p_fox = _matmul([h], wts["wide_b"], layer, _BF16, row_ssq=ssq, name="in_proj_fox")
    gates = _matmul([h], wts["narrow"], layer, _F32, row_ssq=ssq, name="in_proj_gates")

    y_sc = _short_conv(p, sc_conv, seq_len, 0, 1, 2, gw)
    lam_init = 0.8 - 0.6 * math.exp(-0.3 * layer)
    y_df = _diff_attention(p, lam_vecs, diff_norm, lam_init, batch, seq_len, col(3), col(4), col(5), diff_heads, d)

    gates_t = gates.reshape(batch, seq_len, gates.shape[1]).transpose(0, 2, 1)
    f0 = wts["fox_f_lane"]
    fox_rows = gates_t[:, f0:f0 + fox_heads].reshape(batch * fox_heads, seq_len)
    fox_bias_rows = jnp.tile(fox_bias.astype(_F32), batch).reshape(batch * fox_heads, 1)
    cum = _fox_cumsum(fox_rows, fox_bias_rows)
    y_fx = _fox_attention(p_fox, cum, batch, seq_len, col(0), col(1), col(2), fox_heads, d)

    y_dn = _gated_deltanet(p, gates, gates_t[:, :2 * dn_heads], dn_conv, dn_a_log, dn_dt_bias, dn_norm,
                           batch, seq_len, col(6), col(7), col(8), col(9), dn_heads, d)

    x, h, ssq = _matmul([y_sc, y_df, y_dn, y_fx], wts["out"], layer, _F32, residual=x, emit_stats=True,
                        tn_target=512, name="out_proj")
    act = _ffn_in(h, ssq, wts["gate"], wts["up"], wts["conv"], layer, seq_len)
    if last:
        return _matmul_acc(act, wts["down"], layer, x, name="ffn_out"), None, None
    return _matmul_acc(act, wts["down"], layer, x, emit_stats=True, name="ffn_out")


def kernel(x, attn_norm, w_in, sc_conv, lam_q1, lam_k1, lam_q2, lam_k2, diff_norm, dn_conv, dn_a_log,
           dn_dt_bias, dn_norm, fox_bias, w_out, ffn_norm, w_gate, w_up, ffn_conv, w_down, final_norm):
    batch, seq_len, d_model = x.shape
    xf = x.reshape(batch * seq_len, d_model).astype(_F32)
    wts = _bf16_weights(w_in, w_out, w_gate, w_up, ffn_conv, w_down, attn_norm, ffn_norm,
                        w_out.shape[1] // 4, dn_a_log.shape[-1])
    layers = w_in.shape[0]
    stream = (xf,) + tuple(_stream_stats(xf))
    for l in range(layers):
        stream = _layer(stream, l, l == layers - 1, batch, seq_len, wts, sc_conv[l],
                        (lam_q1[l], lam_k1[l], lam_q2[l], lam_k2[l]), diff_norm[l], dn_conv[l], dn_a_log[l],
                        dn_dt_bias[l], dn_norm[l], fox_bias[l])
    out = _rmsnorm(stream[0], final_norm, x.dtype)
    return out.reshape(batch, seq_len, d_model)
```

```python
import functools
import math

import jax
import jax.numpy as jnp
from jax import lax
from jax.experimental import pallas as pl
from jax.experimental.pallas import tpu as pltpu

_F32 = jnp.float32
_BF16 = jnp.bfloat16
_EPS = 1e-6
_NEG_INF = float("-inf")

_V7X_VMEM_BYTES = 64 * 1024 * 1024
_V7X_LANES = 128
_V7X_SUBLANES = 8
_BF16_ROWS = 16
_DN_CHUNK = 128
_FF_ALIGN = 1024


def _tile(n, target, align):
    t = min(target, n)
    t -= t % align
    while t >= align:
        if n % t == 0:
            return t
        t -= align
    return n


def _params(semantics, vmem_bytes):
    limit = min(int(vmem_bytes) + (8 << 20), _V7X_VMEM_BYTES - (4 << 20))
    return pltpu.CompilerParams(dimension_semantics=semantics, vmem_limit_bytes=limit)


def _silu(x):
    return x * (1.0 / (1.0 + jnp.exp(-x)))


def _softplus(x):
    return jnp.maximum(x, 0.0) + jnp.log1p(jnp.exp(-jnp.abs(x)))


def _shift_rows(x, prev8, s):
    xs = pltpu.roll(x, s, 0)
    ps = pltpu.roll(prev8, s, 0)
    row = lax.broadcasted_iota(jnp.int32, prev8.shape, 0)
    head = jnp.where(row < s, ps, xs[:_V7X_SUBLANES])
    return jnp.concatenate([head, xs[_V7X_SUBLANES:]], axis=0)


def _causal_conv(x, prev8, w):
    k = w.shape[0]
    out = x * w[k - 1:k]
    for j in range(k - 1):
        out = out + _shift_rows(x, prev8, k - 1 - j) * w[j:j + 1]
    return out


def _rmsnorm_kernel(x_ref, g_ref, o_ref):
    x = x_ref[...]
    y = x * lax.rsqrt(jnp.mean(x * x, axis=-1, keepdims=True) + _EPS)
    o_ref[...] = (y * g_ref[...]).astype(o_ref.dtype)


def _rmsnorm(x, g, out_dtype):
    n, d = x.shape
    tm = _tile(n, 512, _BF16_ROWS)
    vmem = 2 * tm * d * (4 + jnp.dtype(out_dtype).itemsize) + 2 * d * 4
    return pl.pallas_call(
        _rmsnorm_kernel,
        grid=(n // tm,),
        in_specs=[pl.BlockSpec((tm, d), lambda i: (i, 0)), pl.BlockSpec((1, d), lambda i: (0, 0))],
        out_specs=pl.BlockSpec((tm, d), lambda i: (i, 0)),
        out_shape=jax.ShapeDtypeStruct((n, d), out_dtype),
        compiler_params=_params(("parallel",), vmem),
        name="rmsnorm",
    )(x, g.reshape(1, d).astype(_F32))


def _stream_stats_kernel(x_ref, g_ref, xb_ref, ssq_ref):
    x = x_ref[...]
    xb_ref[...] = (x * g_ref[...]).astype(xb_ref.dtype)
    ssq_ref[...] = jnp.broadcast_to(jnp.sum(x * x, axis=-1, keepdims=True), ssq_ref.shape)


def _stream_stats(x, gain):
    n, d = x.shape
    tm = _tile(n, 512, _BF16_ROWS)
    return pl.pallas_call(
        _stream_stats_kernel,
        grid=(n // tm,),
        in_specs=[pl.BlockSpec((tm, d), lambda i: (i, 0)), pl.BlockSpec((1, d), lambda i: (0, 0))],
        out_specs=[pl.BlockSpec((tm, d), lambda i: (i, 0)), pl.BlockSpec((tm, _V7X_LANES), lambda i: (i, 0))],
        out_shape=[jax.ShapeDtypeStruct((n, d), _BF16), jax.ShapeDtypeStruct((n, _V7X_LANES), _F32)],
        compiler_params=_params(("parallel",), 2 * tm * d * 6),
        name="stream_stats",
    )(x, gain.reshape(1, d).astype(_F32))


def _inv_rms(ssq, width):
    return lax.rsqrt(ssq * (1.0 / width) + _EPS)


def _emit_stream_stats(x_new, first, gain_ref, xb_ref, ssq_ref):
    xb_ref[...] = (x_new * gain_ref[...]).astype(xb_ref.dtype)
    part = jnp.sum(x_new * x_new, axis=-1, keepdims=True)

    @pl.when(first)
    def _():
        ssq_ref[...] = jnp.broadcast_to(part, ssq_ref.shape)

    @pl.when(jnp.logical_not(first))
    def _():
        ssq_ref[...] += part


def _mm_kernel(*refs, n_pairs, has_res, has_ssq, emit_stats):
    a_refs, w_refs = refs[:n_pairs], refs[n_pairs:2 * n_pairs]
    extra = list(refs[2 * n_pairs:])
    res_ref = extra.pop(0) if has_res else None
    ssq_in_ref = extra.pop(0) if has_ssq else None
    gain_ref = extra.pop(0) if emit_stats else None
    o_ref = extra.pop(0)
    acc = jnp.dot(a_refs[0][...], w_refs[0][...], preferred_element_type=_F32)
    for a_ref, w_ref in zip(a_refs[1:], w_refs[1:]):
        acc = acc + jnp.dot(a_ref[...], w_ref[...], preferred_element_type=_F32)
    if has_ssq:
        acc = acc * _lane_repeat(_inv_rms(ssq_in_ref[...], a_refs[0].shape[1]), acc.shape[1])
    if has_res:
        acc = acc + res_ref[...]
    o_ref[...] = acc.astype(o_ref.dtype)
    if emit_stats:
        _emit_stream_stats(acc, pl.program_id(1) == 0, gain_ref, *extra)


def _matmul(a_list, w, layer, out_dtype, residual=None, row_ssq=None, next_gain=None, n_cols=None,
            tm_target=1024, tn_target=1024, name="matmul"):
    emit_stats = next_gain is not None
    m, n = a_list[0].shape[0], n_cols or w.shape[2]
    kb = a_list[0].shape[1]
    assert all(a.shape[1] == kb for a in a_list) and w.shape[1] == kb * len(a_list)
    assert row_ssq is None or len(a_list) == 1
    tm = _tile(m, tm_target, _BF16_ROWS)
    tn = _tile(n, tn_target, _V7X_LANES)
    ktot = w.shape[1]
    in_specs = [pl.BlockSpec((tm, kb), lambda i, j: (i, 0)) for _ in a_list]
    in_specs += [pl.BlockSpec((None, kb, tn), functools.partial(lambda i, j, r: (layer, r, j), r=r))
                 for r in range(len(a_list))]
    args = list(a_list) + [w] * len(a_list)
    vmem = 2 * 2 * ktot * (tm + tn) + 2 * tm * tn * jnp.dtype(out_dtype).itemsize + tm * tn * 4
    if residual is not None:
        in_specs.append(pl.BlockSpec((tm, tn), lambda i, j: (i, j)))
        args.append(residual)
        vmem += 2 * tm * tn * 4
    stats_spec = pl.BlockSpec((tm, _V7X_LANES), lambda i, j: (i, 0))
    if row_ssq is not None:
        in_specs.append(stats_spec)
        args.append(row_ssq)
    out_specs = pl.BlockSpec((tm, tn), lambda i, j: (i, j))
    out_shape = jax.ShapeDtypeStruct((m, n), out_dtype)
    if emit_stats:
        in_specs.append(pl.BlockSpec((1, tn), lambda i, j: (0, j)))
        args.append(next_gain.reshape(1, n).astype(_F32))
        out_specs = [out_specs, pl.BlockSpec((tm, tn), lambda i, j: (i, j)), stats_spec]
        out_shape = [out_shape, jax.ShapeDtypeStruct((m, n), _BF16), jax.ShapeDtypeStruct((m, _V7X_LANES), _F32)]
        vmem += 2 * tm * tn * 2
    return pl.pallas_call(
        functools.partial(_mm_kernel, n_pairs=len(a_list), has_res=residual is not None,
                          has_ssq=row_ssq is not None, emit_stats=emit_stats),
        grid=(m // tm, n // tn),
        in_specs=in_specs,
        out_specs=out_specs,
        out_shape=out_shape,
        compiler_params=_params(("parallel", "arbitrary" if emit_stats else "parallel"), vmem),
        name=name,
    )(*args)


def _mm_acc_kernel(a_ref, w_ref, r_ref, *rest, nk, emit_stats):
    gain_ref = rest[0] if emit_stats else None
    o_ref, acc_ref = rest[1 if emit_stats else 0], rest[-1]
    k = pl.program_id(2)
    prod = lambda: jnp.dot(a_ref[...], w_ref[...], preferred_element_type=_F32)

    def finish(x_new):
        o_ref[...] = x_new
        if emit_stats:
            _emit_stream_stats(x_new, pl.program_id(1) == 0, gain_ref, rest[2], rest[3])

    if nk == 1:
        finish(r_ref[...] + prod())
        return

    @pl.when(k == 0)
    def _():
        acc_ref[...] = r_ref[...] + prod()

    @pl.when(jnp.logical_and(k > 0, k < nk - 1))
    def _():
        acc_ref[...] += prod()

    @pl.when(k == nk - 1)
    def _():
        finish(acc_ref[...] + prod())


def _matmul_acc(a, w, layer, residual, next_gain=None, tm_target=1024, tn_target=1024, tk_target=2816,
                name="matmul_acc"):
    emit_stats = next_gain is not None
    m, kdim = a.shape
    n = w.shape[2]
    tm = _tile(m, tm_target, _BF16_ROWS)
    tn = _tile(n, tn_target, _V7X_LANES)
    tk = _tile(kdim, tk_target, _V7X_LANES)
    vmem = 2 * 2 * tk * (tm + tn) + 5 * tm * tn * 4
    tile_spec = pl.BlockSpec((tm, tn), lambda i, j, k: (i, j))
    out_specs, out_shape = tile_spec, jax.ShapeDtypeStruct((m, n), _F32)
    in_specs = [pl.BlockSpec((tm, tk), lambda i, j, k: (i, k)),
                pl.BlockSpec((None, tk, tn), lambda i, j, k: (layer, k, j)), tile_spec]
    args = [a, w, residual]
    if emit_stats:
        in_specs.append(pl.BlockSpec((1, tn), lambda i, j, k: (0, j)))
        args.append(next_gain.reshape(1, n).astype(_F32))
        out_specs = [tile_spec, tile_spec, pl.BlockSpec((tm, _V7X_LANES), lambda i, j, k: (i, 0))]
        out_shape = [out_shape, jax.ShapeDtypeStruct((m, n), _BF16), jax.ShapeDtypeStruct((m, _V7X_LANES), _F32)]
        vmem += 2 * tm * tn * 2
    return pl.pallas_call(
        functools.partial(_mm_acc_kernel, nk=kdim // tk, emit_stats=emit_stats),
        grid=(m // tm, n // tn, kdim // tk),
        in_specs=in_specs,
        out_specs=out_specs,
        out_shape=out_shape,
        scratch_shapes=[pltpu.VMEM((tm, tn), _F32)],
        compiler_params=_params(("parallel", "arbitrary" if emit_stats else "parallel", "arbitrary"), vmem),
        name=name,
    )(*args)


def _sconv_kernel(h_ref, c_ref, b_ref, hh_ref, ch_ref, w_ref, o_ref, *, blocks_per_seq):
    i = pl.program_id(0)
    x = c_ref[...].astype(_F32) * h_ref[...].astype(_F32)
    halo = ch_ref[...].astype(_F32) * hh_ref[...].astype(_F32)
    prev8 = jnp.where(i % blocks_per_seq == 0, 0.0, halo[_BF16_ROWS - _V7X_SUBLANES:])
    y = b_ref[...].astype(_F32) * _causal_conv(x, prev8, w_ref[...])
    o_ref[...] = y.astype(o_ref.dtype)


def _short_conv(p, conv_w, seq_len, col_h, col_c, col_b, width):
    n = p.shape[0]
    tm = _tile(seq_len, 512, _BF16_ROWS)
    hb = tm // _BF16_ROWS
    cur = lambda c: pl.BlockSpec((tm, width), lambda i: (i, c))
    halo = lambda c: pl.BlockSpec((_BF16_ROWS, width), lambda i: (jnp.maximum(i * hb - 1, 0), c))
    k = conv_w.shape[0]
    vmem = 2 * (4 * tm + 2 * _BF16_ROWS) * width * 2 + 8 * tm * width * 4
    return pl.pallas_call(
        functools.partial(_sconv_kernel, blocks_per_seq=seq_len // tm),
        grid=(n // tm,),
        in_specs=[cur(col_h), cur(col_c), cur(col_b), halo(col_h), halo(col_c),
                  pl.BlockSpec((k, width), lambda i: (0, 0))],
        out_specs=pl.BlockSpec((tm, width), lambda i: (i, 0)),
        out_shape=jax.ShapeDtypeStruct((n, width), _BF16),
        compiler_params=_params(("parallel",), vmem),
        name="short_conv",
    )(p, p, p, p, p, conv_w.astype(_F32))


_LOG2E = math.log2(math.e)


def _lane_tiles(x, width):
    return [x[:, c:c + _V7X_LANES] for c in range(0, width, _V7X_LANES)]


def _lane_repeat(x, width):
    return jnp.concatenate([x] * (width // _V7X_LANES), axis=1)


def _softmax_block(s, m_prev):
    m_new = jnp.maximum(m_prev, jnp.max(s, axis=-1, keepdims=True))
    alpha = jnp.exp2(m_prev - m_new)
    p = jnp.concatenate([jnp.exp2(t - m_new) for t in _lane_tiles(s, s.shape[1])], axis=1)
    return m_new, alpha, p


def _causal_mask(s, q0, k0):
    row = q0 + lax.broadcasted_iota(jnp.int32, s.shape, 0)
    col = k0 + lax.broadcasted_iota(jnp.int32, s.shape, 1)
    return jnp.where(col <= row, s, _NEG_INF)


def _qk(q, k):
    return lax.dot_general(q, k, (((1,), (1,)), ((), ())), preferred_element_type=_F32)


def _flash_blocks(seq_len, kv_per_q):
    bq = _tile(seq_len, 1024, _V7X_LANES)
    bk = bq // kv_per_q if bq % (kv_per_q * _V7X_LANES) == 0 else bq
    return bq, bk


def _resident_spec(seq_len, width, col):
    return pl.BlockSpec((seq_len, width), lambda b, h, i: (b, col // width + h), pipeline_mode=pl.Buffered(1))


def _kv_sweep(qi, r, bk, step):
    def trip(i, carry):
        for d in range(r):
            step(i * r + d, False, 0)
        return carry

    lax.fori_loop(0, qi, trip, 0)
    for d in range(r):
        step(qi * r + d, True, d * bk)


def _diff_kernel(q_ref, k_ref, v_ref, lq1_ref, lk1_ref, lq2_ref, lk2_ref, g_ref, o_ref,
                 m_ref, l_ref, acc_ref, *, bq, bk, d, hp, scale, lam_init):
    qi = pl.program_id(2)
    hw = 2 * d
    m_ref[...] = jnp.full(m_ref.shape, _NEG_INF, _F32)
    l_ref[...] = jnp.zeros(l_ref.shape, _F32)
    acc_ref[...] = jnp.zeros(acc_ref.shape, _F32)
    q = (q_ref[...].astype(_F32) * (scale * _LOG2E)).astype(_BF16)

    def step(j, masked, row0):
        k0 = pl.multiple_of(j * bk, bk)
        k = k_ref[pl.ds(k0, bk), :]
        v = v_ref[pl.ds(k0, bk), :]
        rows = slice(row0, bq)
        for c in range(2 * hp):
            cols = slice(c * d, (c + 1) * d)
            s = _qk(q[rows, cols], k[:, cols])
            if masked:
                s = _causal_mask(s, qi * bq + row0, k0)
            m_new, alpha, p = _softmax_block(s, m_ref[c, rows])
            l_ref[c, rows] = alpha * l_ref[c, rows] + jnp.sum(p, axis=-1, keepdims=True)
            vh = v[:, (c // 2) * hw:(c // 2 + 1) * hw]
            acc_ref[c, rows] = (_lane_repeat(alpha, hw) * acc_ref[c, rows]
                                + jnp.dot(p.astype(_BF16), vh, preferred_element_type=_F32))
            m_ref[c, rows] = m_new

    _kv_sweep(qi, bq // bk, bk, step)
    lam = (jnp.exp(jnp.sum(lq1_ref[...] * lk1_ref[...], axis=-1, keepdims=True))
           - jnp.exp(jnp.sum(lq2_ref[...] * lk2_ref[...], axis=-1, keepdims=True)) + lam_init)
    for h in range(hp):
        o = (acc_ref[2 * h] / _lane_repeat(l_ref[2 * h], hw)
             - lam * (acc_ref[2 * h + 1] / _lane_repeat(l_ref[2 * h + 1], hw)))
        o = o * lax.rsqrt(jnp.mean(o * o, axis=-1, keepdims=True) + _EPS) * g_ref[...]
        o_ref[:, h * hw:(h + 1) * hw] = (o * (1.0 - lam_init)).astype(o_ref.dtype)


def _diff_attention(p, lam_vecs, norm_g, lam_init, batch, seq_len, col_q, col_k, col_v, heads, d):
    n = p.shape[0]
    hw = 2 * d
    bq, bk = _flash_blocks(seq_len, 2)
    nq = seq_len // bq
    hp = 2 if heads % 2 == 0 else 1
    w = hp * hw
    vec = pl.BlockSpec((1, d), lambda b, h, i: (0, 0))
    vmem = 2 * seq_len * w * 2 + 4 * bq * w * 2 + 2 * hp * bq * (hw + 2 * _V7X_LANES) * 4 + 12 * bq * bk * 4
    return pl.pallas_call(
        functools.partial(_diff_kernel, bq=bq, bk=bk, d=d, hp=hp, scale=d ** -0.5, lam_init=lam_init),
        grid=(batch, heads // hp, nq),
        in_specs=[pl.BlockSpec((bq, w), lambda b, h, i: (b * nq + i, col_q // w + h)),
                  _resident_spec(seq_len, w, col_k), _resident_spec(seq_len, w, col_v),
                  vec, vec, vec, vec,
                  pl.BlockSpec((1, hw), lambda b, h, i: (0, 0))],
        out_specs=pl.BlockSpec((bq, w), lambda b, h, i: (b * nq + i, h)),
        out_shape=jax.ShapeDtypeStruct((n, heads * hw), _BF16),
        scratch_shapes=[pltpu.VMEM((2 * hp, bq, _V7X_LANES), _F32), pltpu.VMEM((2 * hp, bq, _V7X_LANES), _F32),
                        pltpu.VMEM((2 * hp, bq, hw), _F32)],
        compiler_params=_params(("parallel", "parallel", "arbitrary"), vmem),
        name="diff_attention",
    )(p, p, p, *[v.reshape(1, d).astype(_F32) for v in lam_vecs], norm_g.reshape(1, hw).astype(_F32))


def _fox_cum_kernel(f_ref, bias_ref, o_ref):
    x = f_ref[...] + bias_ref[...]
    ls = jnp.minimum(x, 0.0) - jnp.log1p(jnp.exp(-jnp.abs(x)))
    r = lax.broadcasted_iota(jnp.int32, (_V7X_LANES, _V7X_LANES), 0)
    c = lax.broadcasted_iota(jnp.int32, (_V7X_LANES, _V7X_LANES), 1)
    upper = (r <= c).astype(_F32)
    carry = jnp.zeros((x.shape[0], 1), _F32)
    for t in range(x.shape[1] // _V7X_LANES):
        sl = slice(t * _V7X_LANES, (t + 1) * _V7X_LANES)
        w = jnp.dot(ls[:, sl], upper, precision=lax.Precision.HIGHEST, preferred_element_type=_F32) + carry
        o_ref[:, sl] = w
        carry = w[:, _V7X_LANES - 1:]


def _fox_cumsum(f_rows, bias_rows):
    return pl.pallas_call(
        _fox_cum_kernel,
        out_shape=jax.ShapeDtypeStruct(f_rows.shape, _F32),
        name="fox_cumsum",
    )(f_rows, bias_rows)


def _fox_kernel(q_ref, k_ref, v_ref, cq_ref, ck_ref, o_ref, m_ref, acc_ref, *, bq, bk, d, hp, scale):
    qi = pl.program_id(2)
    m_ref[...] = jnp.full(m_ref.shape, _NEG_INF, _F32)
    acc_ref[...] = jnp.zeros(acc_ref.shape, _F32)
    q = (q_ref[...].astype(_F32) * (scale * _LOG2E)).astype(_BF16)
    cq = [jnp.broadcast_to(cq_ref[0, h] * _LOG2E, (bq, _V7X_LANES)) for h in range(hp)]
    ones = jnp.ones((bk, _V7X_LANES), _BF16)

    def step(j, masked, row0):
        k0 = pl.multiple_of(j * bk, bk)
        rows = slice(row0, bq)
        k = k_ref[pl.ds(k0, bk), :]
        v = v_ref[pl.ds(k0, bk), :]
        for h in range(hp):
            cols = slice(h * d, (h + 1) * d)
            s = _qk(q[rows, cols], k[:, cols])
            ck = ck_ref[0, h, pl.ds(j, 1), :] * _LOG2E
            s = jnp.concatenate([st + (cq[h][rows] - ct)
                                 for st, ct in zip(_lane_tiles(s, bk), _lane_tiles(ck, bk))], axis=1)
            if masked:
                s = _causal_mask(s, qi * bq + row0, k0)
            m_new, alpha, p = _softmax_block(s, m_ref[h, rows])
            v_ones = jnp.concatenate([v[:, cols], ones], axis=1)
            acc_ref[h, rows] = (_lane_repeat(alpha, d + _V7X_LANES) * acc_ref[h, rows]
                                + jnp.dot(p.astype(_BF16), v_ones, preferred_element_type=_F32))
            m_ref[h, rows] = m_new

    _kv_sweep(qi, bq // bk, bk, step)
    for h in range(hp):
        acc = acc_ref[h]
        o_ref[:, h * d:(h + 1) * d] = (acc[:, :d] / _lane_repeat(acc[:, d:], d)).astype(o_ref.dtype)


def _fox_attention(p, cum, batch, seq_len, col_q, col_k, col_v, heads, d):
    n = p.shape[0]
    bq, bk = _flash_blocks(seq_len, 4)
    nq, nk = seq_len // bq, seq_len // bk
    hp = 4 if heads % 4 == 0 else 1
    w = hp * d
    cum_rows = cum.reshape(batch, heads, nk, bk)
    cum_cols = cum.reshape(batch, heads, seq_len, 1)
    vmem = (2 * seq_len * w * 2 + 4 * bq * w * 2 + 2 * hp * seq_len * 4 + 2 * hp * bq * _V7X_LANES * 4
            + hp * bq * (d + 2 * _V7X_LANES) * 4 + 12 * bq * bk * 4)
    return pl.pallas_call(
        functools.partial(_fox_kernel, bq=bq, bk=bk, d=d, hp=hp, scale=d ** -0.5),
        grid=(batch, heads // hp, nq),
        in_specs=[pl.BlockSpec((bq, w), lambda b, h, i: (b * nq + i, col_q // w + h)),
                  _resident_spec(seq_len, w, col_k), _resident_spec(seq_len, w, col_v),
                  pl.BlockSpec((1, hp, bq, 1), lambda b, h, i: (b, h, i, 0)),
                  pl.BlockSpec((1, hp, nk, bk), lambda b, h, i: (b, h, 0, 0))],
        out_specs=pl.BlockSpec((bq, w), lambda b, h, i: (b * nq + i, h)),
        out_shape=jax.ShapeDtypeStruct((n, heads * d), _BF16),
        scratch_shapes=[pltpu.VMEM((hp, bq, _V7X_LANES), _F32), pltpu.VMEM((hp, bq, d + _V7X_LANES), _F32)],
        compiler_params=_params(("parallel", "parallel", "arbitrary"), vmem),
        name="fox_attention",
    )(p, p, p, cum_cols, cum_rows)


def _lane_column(x, h):
    lane = lax.broadcasted_iota(jnp.int32, x.shape, 1)
    return jnp.sum(jnp.where(lane == h, x, 0.0), axis=-1, keepdims=True)


def _bdot(a, b):
    return jnp.dot(a.astype(_BF16), b.astype(_BF16), preferred_element_type=_F32)


_INV_BASE_LOG2 = 4


def _unit_lower_inverses(mats, ri, ci):
    c = mats[0].shape[0]
    same_block = lambda log2n: (ri >> log2n) == (ci >> log2n)
    eye = (ri == ci).astype(_F32)
    base = same_block(_INV_BASE_LOG2)
    pw = [jnp.where(base, -a, 0.0) for a in mats]
    tinv = [eye + x for x in pw]
    for _ in range(_INV_BASE_LOG2 - 1):
        pw = [_bdot(x, x) for x in pw]
        tinv = [t + _bdot(t, x) for t, x in zip(tinv, pw)]
    log2n = _INV_BASE_LOG2
    while (1 << log2n) < c:
        level = same_block(log2n + 1) & jnp.logical_not(same_block(log2n))
        off = [_bdot(t, jnp.where(level, a, 0.0)) for t, a in zip(tinv, mats)]
        tinv = [t - _bdot(o, t) for t, o in zip(tinv, off)]
        log2n += 1
    return tinv


def _deltanet_kernel(q_ref, k_ref, v_ref, z_ref, gc_ref, gr_ref, cw_ref, alog_c_ref, dtb_c_ref,
                     alog_r_ref, dtb_r_ref, ng_ref, o_ref, s_ref, tail_ref, *, nb, heads, d, c):
    t = pl.program_id(0)
    width = heads * d

    @pl.when(t == 0)
    def _():
        s_ref[...] = jnp.zeros(s_ref.shape, _F32)
        tail_ref[...] = jnp.zeros(tail_ref.shape, _F32)

    ri = lax.broadcasted_iota(jnp.int32, (c, c), 0)
    ci = lax.broadcasted_iota(jnp.int32, (c, c), 1)
    incl = ri >= ci
    strict = ri > ci
    lower = incl.astype(_F32)
    upper = (ri <= ci).astype(_F32)
    hp = lax.Precision.HIGHEST
    cw = cw_ref[...]
    unit = lambda x: x * lax.rsqrt(jnp.sum(x * x, axis=-1, keepdims=True) + _EPS)

    q, k, v, gc, beta, gcr = [], [], [], [], [], []
    for b in range(nb):
        conv = []
        for idx, ref in enumerate((q_ref, k_ref, v_ref)):
            raw = ref[b].astype(_F32)
            cols = slice(idx * width, (idx + 1) * width)
            conv.append(_silu(_causal_conv(raw, tail_ref[b, :, cols], cw[:, cols])))
            tail_ref[b, :, cols] = raw[c - _V7X_SUBLANES:]
        gates_c = gc_ref[b]
        g_cols = -jnp.exp(alog_c_ref[...]) * _softplus(gates_c + dtb_c_ref[...])
        beta_cols = 1.0 / (1.0 + jnp.exp(-gates_c))
        g_rows = -jnp.exp(alog_r_ref[...]) * _softplus(gr_ref[b][:heads] + dtb_r_ref[...])
        gcum_cols = jnp.dot(lower, g_cols, precision=hp, preferred_element_type=_F32)
        gcum_rows = jnp.dot(g_rows, upper, precision=hp, preferred_element_type=_F32)
        for h in range(heads):
            hs = slice(h * d, (h + 1) * d)
            q.append(unit(conv[0][:, hs]) * (d ** -0.5))
            k.append(unit(conv[1][:, hs]))
            v.append(conv[2][:, hs])
            gc.append(_lane_column(gcum_cols, h))
            beta.append(_lane_column(beta_cols, heads + h))
            gcr.append(gcum_rows[h:h + 1])
    chains = range(nb * heads)
    decay = [jnp.exp(jnp.where(incl, gc[n] - gcr[n], _NEG_INF)) for n in chains]
    k16 = [x.astype(_BF16) for x in k]
    kb = [k[n] * beta[n] for n in chains]
    a = [jnp.where(strict, _qk(kb[n].astype(_BF16), k16[n]) * decay[n], 0.0) for n in chains]
    tinv = _unit_lower_inverses(a, ri, ci)
    egc = [jnp.exp(x) for x in gc]
    wu = [_bdot(tinv[n], jnp.concatenate([kb[n] * egc[n], v[n] * beta[n]], axis=1)) for n in chains]
    aqk = [_qk(q[n].astype(_BF16), k16[n]) * decay[n] for n in chains]
    s = [s_ref[n // heads, n % heads] for n in chains]
    ws_qs = [_bdot(jnp.concatenate([wu[n][:, :d], q[n] * egc[n]], axis=0), s[n]) for n in chains]
    v_new = [wu[n][:, d:] - ws_qs[n][:c] for n in chains]
    o = [ws_qs[n][c:] + _bdot(aqk[n], v_new[n]) for n in chains]
    g_last = [x[c - 1:c] for x in gc]
    k_dec = [k[n] * jnp.exp(g_last[n] - gc[n]) for n in chains]
    for n in chains:
        s_ref[n // heads, n % heads] = s[n] * jnp.exp(g_last[n]) + lax.dot_general(
            k_dec[n].astype(_BF16), v_new[n].astype(_BF16), (((0,), (0,)), ((), ())), preferred_element_type=_F32)
    ng = ng_ref[...]
    for n in chains:
        hs = slice((n % heads) * d, (n % heads + 1) * d)
        on = o[n] * lax.rsqrt(jnp.mean(o[n] * o[n], axis=-1, keepdims=True) + _EPS) * ng
        o_ref[n // heads, :, hs] = (on * _silu(z_ref[n // heads, :, hs].astype(_F32))).astype(o_ref.dtype)


def _gated_deltanet(p, gates_cols, gates_rows, conv_w, a_log, dt_bias, norm_g, batch, seq_len,
                    col_q, col_k, col_v, col_z, heads, d):
    n = p.shape[0]
    c = _tile(seq_len, _DN_CHUNK, _BF16_ROWS)
    nt = seq_len // c
    width = heads * d
    lanes = _V7X_LANES
    kw = conv_w.shape[0]
    p3 = p.reshape(batch, seq_len, p.shape[1])
    col = lambda off: pl.BlockSpec((batch, c, width), lambda t: (0, t, off // width))
    const = lambda shape: pl.BlockSpec(shape, lambda t: (0,) * len(shape))
    pad_lane = lambda v: jnp.zeros((1, lanes), _F32).at[0, :heads].set(v.astype(_F32))
    vmem = batch * (2 * 5 * c * width * 2 + 14 * c * width * 4 + heads * d * d * 4) + 2 * kw * 3 * width * 4
    out = pl.pallas_call(
        functools.partial(_deltanet_kernel, nb=batch, heads=heads, d=d, c=c),
        grid=(nt,),
        in_specs=[col(col_q), col(col_k), col(col_v), col(col_z),
                  pl.BlockSpec((batch, c, lanes), lambda t: (0, t, 0)),
                  pl.BlockSpec((batch, gates_rows.shape[1], c), lambda t: (0, 0, t)),
                  const((kw, 3 * width)), const((1, lanes)), const((1, lanes)),
                  const((heads, 1)), const((heads, 1)), const((1, d))],
        out_specs=pl.BlockSpec((batch, c, width), lambda t: (0, t, 0)),
        out_shape=jax.ShapeDtypeStruct((batch, seq_len, width), _BF16),
        scratch_shapes=[pltpu.VMEM((batch, heads, d, d), _F32),
                        pltpu.VMEM((batch, _V7X_SUBLANES, 3 * width), _F32)],
        compiler_params=_params(("arbitrary",), vmem),
        name="gated_deltanet",
    )(p3, p3, p3, p3, gates_cols.reshape(batch, seq_len, gates_cols.shape[1]), gates_rows, conv_w.astype(_F32),
      pad_lane(a_log), pad_lane(dt_bias), a_log.reshape(heads, 1).astype(_F32),
      dt_bias.reshape(heads, 1).astype(_F32), norm_g.reshape(1, d).astype(_F32))
    return out.reshape(n, width)


def _ffn_in_kernel(h_ref, ssq_ref, wg_ref, wu_ref, cw_ref, o_ref, tail_ref, *, blocks_per_seq):
    i, j = pl.program_id(0), pl.program_id(1)
    tm, tn = o_ref.shape

    @pl.when(i % blocks_per_seq == 0)
    def _():
        tail_ref[j] = jnp.zeros(tail_ref.shape[1:], _F32)

    h = h_ref[...]
    inv_rms = _lane_repeat(_inv_rms(ssq_ref[...], h.shape[1]), tn)
    g = jnp.dot(h, wg_ref[...], preferred_element_type=_F32) * inv_rms
    sg = _silu(_causal_conv(g, tail_ref[j], cw_ref[...])) * inv_rms
    tail_ref[j] = g[tm - _V7X_SUBLANES:]
    u = jnp.dot(h, wu_ref[...], preferred_element_type=_F32)
    o_ref[...] = (sg * u).astype(o_ref.dtype)


def _ffn_in(h, row_ssq, w_gate, w_up, conv_w, layer, seq_len, tm_target=1024, tn_target=512):
    n, d = h.shape
    ff = w_gate.shape[2]
    tm = _tile(seq_len, tm_target, _BF16_ROWS)
    tn = _tile(ff, tn_target, _V7X_LANES)
    kw = conv_w.shape[1]
    wspec = pl.BlockSpec((None, d, tn), lambda i, j: (layer, 0, j))
    vmem = 2 * 2 * d * (tm + 2 * tn) + 2 * tm * tn * 2 + 6 * tm * tn * 4 + ff * _V7X_SUBLANES * 4
    return pl.pallas_call(
        functools.partial(_ffn_in_kernel, blocks_per_seq=seq_len // tm),
        grid=(n // tm, ff // tn),
        in_specs=[pl.BlockSpec((tm, d), lambda i, j: (i, 0)),
                  pl.BlockSpec((tm, _V7X_LANES), lambda i, j: (i, 0)), wspec, wspec,
                  pl.BlockSpec((None, kw, tn), lambda i, j: (layer, 0, j))],
        out_specs=pl.BlockSpec((tm, tn), lambda i, j: (i, j)),
        out_shape=jax.ShapeDtypeStruct((n, ff), _BF16),
        scratch_shapes=[pltpu.VMEM((ff // tn, _V7X_SUBLANES, tn), _F32)],
        compiler_params=_params(("arbitrary", "arbitrary"), vmem),
        name="ffn_in",
    )(h, row_ssq, w_gate, w_up, conv_w)


def _cast_pad_kernel(x_ref, o_ref, *, rows, cols):
    i, j = pl.program_id(1), pl.program_id(2)
    tr, tc = o_ref.shape
    r = i * tr + lax.broadcasted_iota(jnp.int32, (tr, tc), 0)
    c = j * tc + lax.broadcasted_iota(jnp.int32, (tr, tc), 1)
    o_ref[...] = jnp.where(jnp.logical_and(r < rows, c < cols), x_ref[...], 0.0).astype(o_ref.dtype)


def _cast_pad(w, rows_to, cols_to, dtype):
    layers, rows, cols = w.shape
    tr = _tile(rows_to, 1024, _BF16_ROWS)
    tc = _tile(cols_to, 1024, _V7X_LANES)
    assert rows_to - rows < tr and cols_to - cols < tc
    return pl.pallas_call(
        functools.partial(_cast_pad_kernel, rows=rows, cols=cols),
        grid=(layers, rows_to // tr, cols_to // tc),
        in_specs=[pl.BlockSpec((None, tr, tc), lambda l, i, j: (l, i, j))],
        out_specs=pl.BlockSpec((None, tr, tc), lambda l, i, j: (l, i, j)),
        out_shape=jax.ShapeDtypeStruct((layers, rows_to, cols_to), dtype),
        compiler_params=_params(("parallel", "parallel", "parallel"), 2 * tr * tc * 6),
        name="cast_pad",
    )(w)


def _pad_last(w, total):
    return jnp.pad(w, ((0, 0),) * (w.ndim - 1) + ((0, total - w.shape[-1]),))


def _bf16_weights(w_in, w_out, w_gate, w_up, ffn_conv, w_down, gw, dn_heads):
    o_dn_a = 10 * gw
    o_fx = o_dn_a + 2 * dn_heads
    o_fx_f = o_fx + 3 * gw
    tile2 = o_fx_f // _V7X_LANES * _V7X_LANES
    narrow = jnp.concatenate([w_in[..., o_dn_a:o_dn_a + _V7X_LANES], _pad_last(w_in[..., tile2:], _V7X_LANES)],
                             axis=-1).astype(_BF16)
    ff = w_gate.shape[-1]
    ffp = -(-ff // _FF_ALIGN) * _FF_ALIGN if ff > _FF_ALIGN else ff
    return dict(
        w_in=w_in.astype(_BF16), n_wide_a=o_dn_a, wide_b=w_in[..., o_fx:o_fx_f].astype(_BF16),
        narrow=narrow, fox_f_lane=_V7X_LANES + o_fx_f - tile2, out=w_out.astype(_BF16),
        gate=_cast_pad(w_gate, w_gate.shape[1], ffp, _BF16), up=_cast_pad(w_up, w_up.shape[1], ffp, _BF16),
        conv=_pad_last(ffn_conv.astype(_F32), ffp),
        down=_cast_pad(w_down, ffp, w_down.shape[2], _BF16))


def _layer(stream, layer, batch, seq_len, wts, sc_conv, lam_vecs, diff_norm, dn_conv, dn_a_log,
           dn_dt_bias, dn_norm, fox_bias, ffn_norm, next_attn_norm):
    x, h, ssq = stream
    d = lam_vecs[0].shape[-1]
    gw = wts["out"].shape[1] // 4
    dn_heads, fox_heads = dn_a_log.shape[-1], fox_bias.shape[-1]
    diff_heads = gw // diff_norm.shape[-1]
    col = lambda idx: idx * gw

    p = _matmul([h], wts["w_in"], layer, _BF16, row_ssq=ssq, n_cols=wts["n_wide_a"], name="in_proj")
    p_fox = _matmul([h], wts["wide_b"], layer, _BF16, row_ssq=ssq, name="in_proj_fox")
    gates = _matmul([h], wts["narrow"], layer, _F32, row_ssq=ssq, name="in_proj_gates")

    y_sc = _short_conv(p, sc_conv, seq_len, 0, 1, 2, gw)
    lam_init = 0.8 - 0.6 * math.exp(-0.3 * layer)
    y_df = _diff_attention(p, lam_vecs, diff_norm, lam_init, batch, seq_len, col(3), col(4), col(5), diff_heads, d)

    gates_t = gates.reshape(batch, seq_len, gates.shape[1]).transpose(0, 2, 1)
    f0 = wts["fox_f_lane"]
    fox_rows = gates_t[:, f0:f0 + fox_heads].reshape(batch * fox_heads, seq_len)
    fox_bias_rows = jnp.tile(fox_bias.astype(_F32), batch).reshape(batch * fox_heads, 1)
    cum = _fox_cumsum(fox_rows, fox_bias_rows)
    y_fx = _fox_attention(p_fox, cum, batch, seq_len, col(0), col(1), col(2), fox_heads, d)

    y_dn = _gated_deltanet(p, gates, gates_t[:, :2 * dn_heads], dn_conv, dn_a_log, dn_dt_bias, dn_norm,
                           batch, seq_len, col(6), col(7), col(8), col(9), dn_heads, d)

    x, h, ssq = _matmul([y_sc, y_df, y_dn, y_fx], wts["out"], layer, _F32, residual=x, next_gain=ffn_norm,
                        tn_target=512, name="out_proj")
    act = _ffn_in(h, ssq, wts["gate"], wts["up"], wts["conv"], layer, seq_len)
    if next_attn_norm is None:
        return _matmul_acc(act, wts["down"], layer, x, name="ffn_out"), None, None
    return _matmul_acc(act, wts["down"], layer, x, next_gain=next_attn_norm, name="ffn_out")


def kernel(x, attn_norm, w_in, sc_conv, lam_q1, lam_k1, lam_q2, lam_k2, diff_norm, dn_conv, dn_a_log,
           dn_dt_bias, dn_norm, fox_bias, w_out, ffn_norm, w_gate, w_up, ffn_conv, w_down, final_norm):
    batch, seq_len, d_model = x.shape
    xf = x.reshape(batch * seq_len, d_model).astype(_F32)
    wts = _bf16_weights(w_in, w_out, w_gate, w_up, ffn_conv, w_down, w_out.shape[1] // 4, dn_a_log.shape[-1])
    layers = w_in.shape[0]
    stream = (xf,) + tuple(_stream_stats(xf, attn_norm[0]))
    for l in range(layers):
        stream = _layer(stream, l, batch, seq_len, wts, sc_conv[l],
                        (lam_q1[l], lam_k1[l], lam_q2[l], lam_k2[l]), diff_norm[l], dn_conv[l], dn_a_log[l],
                        dn_dt_bias[l], dn_norm[l], fox_bias[l], ffn_norm[l],
                        attn_norm[l + 1] if l + 1 < layers else None)
    out = _rmsnorm(stream[0], final_norm, x.dtype)
    return out.reshape(batch, seq_len, d_model)
```

```python
import functools
import math

import jax
import jax.numpy as jnp
from jax import lax
from jax.experimental import pallas as pl
from jax.experimental.pallas import tpu as pltpu

_F32 = jnp.float32
_BF16 = jnp.bfloat16
_EPS = 1e-6
_NEG_INF = float("-inf")

_V7X_VMEM_BYTES = 64 * 1024 * 1024
_V7X_LANES = 128
_V7X_SUBLANES = 8
_BF16_ROWS = 16
_DN_CHUNK = 128
_FF_ALIGN = 1024


def _tile(n, target, align):
    t = min(target, n)
    t -= t % align
    while t >= align:
        if n % t == 0:
            return t
        t -= align
    return n


def _params(semantics, vmem_bytes):
    limit = min(int(vmem_bytes) + (8 << 20), _V7X_VMEM_BYTES - (4 << 20))
    return pltpu.CompilerParams(dimension_semantics=semantics, vmem_limit_bytes=limit)


def _silu(x):
    return x * (1.0 / (1.0 + jnp.exp(-x)))


def _softplus(x):
    return jnp.maximum(x, 0.0) + jnp.log1p(jnp.exp(-jnp.abs(x)))


def _shift_rows(x, prev8, s):
    xs = pltpu.roll(x, s, 0)
    ps = pltpu.roll(prev8, s, 0)
    row = lax.broadcasted_iota(jnp.int32, prev8.shape, 0)
    head = jnp.where(row < s, ps, xs[:_V7X_SUBLANES])
    return jnp.concatenate([head, xs[_V7X_SUBLANES:]], axis=0)


def _causal_conv(x, prev8, w):
    k = w.shape[0]
    out = x * w[k - 1:k]
    for j in range(k - 1):
        out = out + _shift_rows(x, prev8, k - 1 - j) * w[j:j + 1]
    return out


def _rmsnorm_kernel(x_ref, g_ref, o_ref):
    x = x_ref[...]
    y = x * lax.rsqrt(jnp.mean(x * x, axis=-1, keepdims=True) + _EPS)
    o_ref[...] = (y * g_ref[...]).astype(o_ref.dtype)


def _rmsnorm(x, g, out_dtype):
    n, d = x.shape
    tm = _tile(n, 512, _BF16_ROWS)
    vmem = 2 * tm * d * (4 + jnp.dtype(out_dtype).itemsize) + 2 * d * 4
    return pl.pallas_call(
        _rmsnorm_kernel,
        grid=(n // tm,),
        in_specs=[pl.BlockSpec((tm, d), lambda i: (i, 0)), pl.BlockSpec((1, d), lambda i: (0, 0))],
        out_specs=pl.BlockSpec((tm, d), lambda i: (i, 0)),
        out_shape=jax.ShapeDtypeStruct((n, d), out_dtype),
        compiler_params=_params(("parallel",), vmem),
        name="rmsnorm",
    )(x, g.reshape(1, d).astype(_F32))


def _stream_stats_kernel(x_ref, g_ref, xb_ref, ssq_ref):
    x = x_ref[...]
    xb_ref[...] = (x * g_ref[...]).astype(xb_ref.dtype)
    ssq_ref[...] = jnp.broadcast_to(jnp.sum(x * x, axis=-1, keepdims=True), ssq_ref.shape)


def _stream_stats(x, gain):
    n, d = x.shape
    tm = _tile(n, 512, _BF16_ROWS)
    return pl.pallas_call(
        _stream_stats_kernel,
        grid=(n // tm,),
        in_specs=[pl.BlockSpec((tm, d), lambda i: (i, 0)), pl.BlockSpec((1, d), lambda i: (0, 0))],
        out_specs=[pl.BlockSpec((tm, d), lambda i: (i, 0)), pl.BlockSpec((tm, _V7X_LANES), lambda i: (i, 0))],
        out_shape=[jax.ShapeDtypeStruct((n, d), _BF16), jax.ShapeDtypeStruct((n, _V7X_LANES), _F32)],
        compiler_params=_params(("parallel",), 2 * tm * d * 6),
        name="stream_stats",
    )(x, gain.reshape(1, d).astype(_F32))


def _inv_rms(ssq, width):
    return lax.rsqrt(ssq * (1.0 / width) + _EPS)


def _emit_stream_stats(x_new, first, gain_ref, xb_ref, ssq_ref):
    xb_ref[...] = (x_new * gain_ref[...]).astype(xb_ref.dtype)
    part = jnp.sum(x_new * x_new, axis=-1, keepdims=True)

    @pl.when(first)
    def _():
        ssq_ref[...] = jnp.broadcast_to(part, ssq_ref.shape)

    @pl.when(jnp.logical_not(first))
    def _():
        ssq_ref[...] += part


def _mm_kernel(*refs, n_pairs, has_res, has_ssq, emit_stats):
    a_refs, w_refs = refs[:n_pairs], refs[n_pairs:2 * n_pairs]
    extra = list(refs[2 * n_pairs:])
    res_ref = extra.pop(0) if has_res else None
    ssq_in_ref = extra.pop(0) if has_ssq else None
    gain_ref = extra.pop(0) if emit_stats else None
    o_ref = extra.pop(0)
    acc = jnp.dot(a_refs[0][...], w_refs[0][...], preferred_element_type=_F32)
    for a_ref, w_ref in zip(a_refs[1:], w_refs[1:]):
        acc = acc + jnp.dot(a_ref[...], w_ref[...], preferred_element_type=_F32)
    if has_ssq:
        acc = acc * _lane_repeat(_inv_rms(ssq_in_ref[...], a_refs[0].shape[1]), acc.shape[1])
    if has_res:
        acc = acc + res_ref[...]
    o_ref[...] = acc.astype(o_ref.dtype)
    if emit_stats:
        _emit_stream_stats(acc, pl.program_id(1) == 0, gain_ref, *extra)


def _matmul(a_list, w, layer, out_dtype, residual=None, row_ssq=None, next_gain=None, n_cols=None,
            tm_target=1024, tn_target=1024, name="matmul"):
    emit_stats = next_gain is not None
    m, n = a_list[0].shape[0], n_cols or w.shape[2]
    kb = a_list[0].shape[1]
    assert all(a.shape[1] == kb for a in a_list) and w.shape[1] == kb * len(a_list)
    assert row_ssq is None or len(a_list) == 1
    tm = _tile(m, tm_target, _BF16_ROWS)
    tn = _tile(n, tn_target, _V7X_LANES)
    ktot = w.shape[1]
    in_specs = [pl.BlockSpec((tm, kb), lambda i, j: (i, 0)) for _ in a_list]
    in_specs += [pl.BlockSpec((None, kb, tn), functools.partial(lambda i, j, r: (layer, r, j), r=r))
                 for r in range(len(a_list))]
    args = list(a_list) + [w] * len(a_list)
    vmem = 2 * 2 * ktot * (tm + tn) + 2 * tm * tn * jnp.dtype(out_dtype).itemsize + tm * tn * 4
    if residual is not None:
        in_specs.append(pl.BlockSpec((tm, tn), lambda i, j: (i, j)))
        args.append(residual)
        vmem += 2 * tm * tn * 4
    stats_spec = pl.BlockSpec((tm, _V7X_LANES), lambda i, j: (i, 0))
    if row_ssq is not None:
        in_specs.append(stats_spec)
        args.append(row_ssq)
    out_specs = pl.BlockSpec((tm, tn), lambda i, j: (i, j))
    out_shape = jax.ShapeDtypeStruct((m, n), out_dtype)
    if emit_stats:
        in_specs.append(pl.BlockSpec((1, tn), lambda i, j: (0, j)))
        args.append(next_gain.reshape(1, n).astype(_F32))
        out_specs = [out_specs, pl.BlockSpec((tm, tn), lambda i, j: (i, j)), stats_spec]
        out_shape = [out_shape, jax.ShapeDtypeStruct((m, n), _BF16), jax.ShapeDtypeStruct((m, _V7X_LANES), _F32)]
        vmem += 2 * tm * tn * 2
    return pl.pallas_call(
        functools.partial(_mm_kernel, n_pairs=len(a_list), has_res=residual is not None,
                          has_ssq=row_ssq is not None, emit_stats=emit_stats),
        grid=(m // tm, n // tn),
        in_specs=in_specs,
        out_specs=out_specs,
        out_shape=out_shape,
        compiler_params=_params(("parallel", "arbitrary" if emit_stats else "parallel"), vmem),
        name=name,
    )(*args)


def _mm_acc_kernel(a_ref, w_ref, r_ref, *rest, nk, emit_stats):
    gain_ref = rest[0] if emit_stats else None
    o_ref, acc_ref = rest[1 if emit_stats else 0], rest[-1]
    k = pl.program_id(2)
    prod = lambda: jnp.dot(a_ref[...], w_ref[...], preferred_element_type=_F32)

    def finish(x_new):
        o_ref[...] = x_new
        if emit_stats:
            _emit_stream_stats(x_new, pl.program_id(1) == 0, gain_ref, rest[2], rest[3])

    if nk == 1:
        finish(r_ref[...] + prod())
        return

    @pl.when(k == 0)
    def _():
        acc_ref[...] = r_ref[...] + prod()

    @pl.when(jnp.logical_and(k > 0, k < nk - 1))
    def _():
        acc_ref[...] += prod()

    @pl.when(k == nk - 1)
    def _():
        finish(acc_ref[...] + prod())


def _matmul_acc(a, w, layer, residual, next_gain=None, tm_target=1024, tn_target=1024, tk_target=2816,
                name="matmul_acc"):
    emit_stats = next_gain is not None
    m, kdim = a.shape
    n = w.shape[2]
    tm = _tile(m, tm_target, _BF16_ROWS)
    tn = _tile(n, tn_target, _V7X_LANES)
    tk = _tile(kdim, tk_target, _V7X_LANES)
    vmem = 2 * 2 * tk * (tm + tn) + 5 * tm * tn * 4
    tile_spec = pl.BlockSpec((tm, tn), lambda i, j, k: (i, j))
    out_specs, out_shape = tile_spec, jax.ShapeDtypeStruct((m, n), _F32)
    in_specs = [pl.BlockSpec((tm, tk), lambda i, j, k: (i, k)),
                pl.BlockSpec((None, tk, tn), lambda i, j, k: (layer, k, j)), tile_spec]
    args = [a, w, residual]
    if emit_stats:
        in_specs.append(pl.BlockSpec((1, tn), lambda i, j, k: (0, j)))
        args.append(next_gain.reshape(1, n).astype(_F32))
        out_specs = [tile_spec, tile_spec, pl.BlockSpec((tm, _V7X_LANES), lambda i, j, k: (i, 0))]
        out_shape = [out_shape, jax.ShapeDtypeStruct((m, n), _BF16), jax.ShapeDtypeStruct((m, _V7X_LANES), _F32)]
        vmem += 2 * tm * tn * 2
    return pl.pallas_call(
        functools.partial(_mm_acc_kernel, nk=kdim // tk, emit_stats=emit_stats),
        grid=(m // tm, n // tn, kdim // tk),
        in_specs=in_specs,
        out_specs=out_specs,
        out_shape=out_shape,
        scratch_shapes=[pltpu.VMEM((tm, tn), _F32)],
        compiler_params=_params(("parallel", "arbitrary" if emit_stats else "parallel", "arbitrary"), vmem),
        name=name,
    )(*args)


def _sconv_kernel(h_ref, c_ref, b_ref, hh_ref, ch_ref, w_ref, o_ref, *, blocks_per_seq):
    i = pl.program_id(0)
    x = c_ref[...].astype(_F32) * h_ref[...].astype(_F32)
    halo = ch_ref[...].astype(_F32) * hh_ref[...].astype(_F32)
    prev8 = jnp.where(i % blocks_per_seq == 0, 0.0, halo[_BF16_ROWS - _V7X_SUBLANES:])
    y = b_ref[...].astype(_F32) * _causal_conv(x, prev8, w_ref[...])
    o_ref[...] = y.astype(o_ref.dtype)


def _short_conv(p, conv_w, seq_len, col_h, col_c, col_b, width):
    n = p.shape[0]
    tm = _tile(seq_len, 512, _BF16_ROWS)
    hb = tm // _BF16_ROWS
    cur = lambda c: pl.BlockSpec((tm, width), lambda i: (i, c))
    halo = lambda c: pl.BlockSpec((_BF16_ROWS, width), lambda i: (jnp.maximum(i * hb - 1, 0), c))
    k = conv_w.shape[0]
    vmem = 2 * (4 * tm + 2 * _BF16_ROWS) * width * 2 + 8 * tm * width * 4
    return pl.pallas_call(
        functools.partial(_sconv_kernel, blocks_per_seq=seq_len // tm),
        grid=(n // tm,),
        in_specs=[cur(col_h), cur(col_c), cur(col_b), halo(col_h), halo(col_c),
                  pl.BlockSpec((k, width), lambda i: (0, 0))],
        out_specs=pl.BlockSpec((tm, width), lambda i: (i, 0)),
        out_shape=jax.ShapeDtypeStruct((n, width), _BF16),
        compiler_params=_params(("parallel",), vmem),
        name="short_conv",
    )(p, p, p, p, p, conv_w.astype(_F32))


_LOG2E = math.log2(math.e)


def _lane_tiles(x, width):
    return [x[:, c:c + _V7X_LANES] for c in range(0, width, _V7X_LANES)]


def _lane_repeat(x, width):
    return jnp.concatenate([x] * (width // _V7X_LANES), axis=1)


def _softmax_block(s, m_prev):
    m_new = jnp.maximum(m_prev, jnp.max(s, axis=-1, keepdims=True))
    alpha = jnp.exp2(m_prev - m_new)
    p = jnp.concatenate([jnp.exp2(t - m_new) for t in _lane_tiles(s, s.shape[1])], axis=1)
    return m_new, alpha, p


def _causal_mask(s, q0, k0):
    row = q0 + lax.broadcasted_iota(jnp.int32, s.shape, 0)
    col = k0 + lax.broadcasted_iota(jnp.int32, s.shape, 1)
    return jnp.where(col <= row, s, _NEG_INF)


def _qk(q, k):
    return lax.dot_general(q, k, (((1,), (1,)), ((), ())), preferred_element_type=_F32)


def _flash_blocks(seq_len, kv_per_q):
    bq = _tile(seq_len, 1024, _V7X_LANES)
    bk = bq // kv_per_q if bq % (kv_per_q * _V7X_LANES) == 0 else bq
    return bq, bk


def _resident_spec(seq_len, width, col):
    return pl.BlockSpec((seq_len, width), lambda b, h, i: (b, col // width + h), pipeline_mode=pl.Buffered(1))


def _kv_sweep(qi, r, bk, step):
    def trip(i, carry):
        for d in range(r):
            step(i * r + d, False, 0)
        return carry

    lax.fori_loop(0, qi, trip, 0)
    for d in range(r):
        step(qi * r + d, True, d * bk)


def _diff_kernel(q_ref, k_ref, v_ref, lq1_ref, lk1_ref, lq2_ref, lk2_ref, g_ref, o_ref,
                 m_ref, l_ref, acc_ref, *, bq, bk, d, hp, scale, lam_init):
    qi = pl.program_id(2)
    hw = 2 * d
    m_ref[...] = jnp.full(m_ref.shape, _NEG_INF, _F32)
    l_ref[...] = jnp.zeros(l_ref.shape, _F32)
    acc_ref[...] = jnp.zeros(acc_ref.shape, _F32)
    q = (q_ref[...].astype(_F32) * (scale * _LOG2E)).astype(_BF16)

    def step(j, masked, row0):
        k0 = pl.multiple_of(j * bk, bk)
        k = k_ref[pl.ds(k0, bk), :]
        v = v_ref[pl.ds(k0, bk), :]
        rows = slice(row0, bq)
        for c in range(2 * hp):
            cols = slice(c * d, (c + 1) * d)
            s = _qk(q[rows, cols], k[:, cols])
            if masked:
                s = _causal_mask(s, qi * bq + row0, k0)
            m_new, alpha, p = _softmax_block(s, m_ref[c, rows])
            l_ref[c, rows] = alpha * l_ref[c, rows] + jnp.sum(p, axis=-1, keepdims=True)
            vh = v[:, (c // 2) * hw:(c // 2 + 1) * hw]
            acc_ref[c, rows] = (_lane_repeat(alpha, hw) * acc_ref[c, rows]
                                + jnp.dot(p.astype(_BF16), vh, preferred_element_type=_F32))
            m_ref[c, rows] = m_new

    _kv_sweep(qi, bq // bk, bk, step)
    lam = (jnp.exp(jnp.sum(lq1_ref[...] * lk1_ref[...], axis=-1, keepdims=True))
           - jnp.exp(jnp.sum(lq2_ref[...] * lk2_ref[...], axis=-1, keepdims=True)) + lam_init)
    for h in range(hp):
        o = (acc_ref[2 * h] / _lane_repeat(l_ref[2 * h], hw)
             - lam * (acc_ref[2 * h + 1] / _lane_repeat(l_ref[2 * h + 1], hw)))
        o = o * lax.rsqrt(jnp.mean(o * o, axis=-1, keepdims=True) + _EPS) * g_ref[...]
        o_ref[:, h * hw:(h + 1) * hw] = (o * (1.0 - lam_init)).astype(o_ref.dtype)


def _diff_attention(p, lam_vecs, norm_g, lam_init, batch, seq_len, col_q, col_k, col_v, heads, d):
    n = p.shape[0]
    hw = 2 * d
    bq, bk = _flash_blocks(seq_len, 4)
    nq = seq_len // bq
    hp = 2 if heads % 2 == 0 else 1
    w = hp * hw
    vec = pl.BlockSpec((1, d), lambda b, h, i: (0, 0))
    vmem = 2 * seq_len * w * 2 + 4 * bq * w * 2 + 2 * hp * bq * (hw + 2 * _V7X_LANES) * 4 + 12 * bq * bk * 4
    return pl.pallas_call(
        functools.partial(_diff_kernel, bq=bq, bk=bk, d=d, hp=hp, scale=d ** -0.5, lam_init=lam_init),
        grid=(batch, heads // hp, nq),
        in_specs=[pl.BlockSpec((bq, w), lambda b, h, i: (b * nq + i, col_q // w + h)),
                  _resident_spec(seq_len, w, col_k), _resident_spec(seq_len, w, col_v),
                  vec, vec, vec, vec,
                  pl.BlockSpec((1, hw), lambda b, h, i: (0, 0))],
        out_specs=pl.BlockSpec((bq, w), lambda b, h, i: (b * nq + i, h)),
        out_shape=jax.ShapeDtypeStruct((n, heads * hw), _BF16),
        scratch_shapes=[pltpu.VMEM((2 * hp, bq, _V7X_LANES), _F32), pltpu.VMEM((2 * hp, bq, _V7X_LANES), _F32),
                        pltpu.VMEM((2 * hp, bq, hw), _F32)],
        compiler_params=_params(("parallel", "parallel", "arbitrary"), vmem),
        name="diff_attention",
    )(p, p, p, *[v.reshape(1, d).astype(_F32) for v in lam_vecs], norm_g.reshape(1, hw).astype(_F32))


def _fox_cum_kernel(f_ref, bias_ref, o_ref):
    x = f_ref[...] + bias_ref[...]
    ls = jnp.minimum(x, 0.0) - jnp.log1p(jnp.exp(-jnp.abs(x)))
    r = lax.broadcasted_iota(jnp.int32, (_V7X_LANES, _V7X_LANES), 0)
    c = lax.broadcasted_iota(jnp.int32, (_V7X_LANES, _V7X_LANES), 1)
    upper = (r <= c).astype(_F32)
    carry = jnp.zeros((x.shape[0], 1), _F32)
    for t in range(x.shape[1] // _V7X_LANES):
        sl = slice(t * _V7X_LANES, (t + 1) * _V7X_LANES)
        w = jnp.dot(ls[:, sl], upper, precision=lax.Precision.HIGHEST, preferred_element_type=_F32) + carry
        o_ref[:, sl] = w
        carry = w[:, _V7X_LANES - 1:]


def _fox_cumsum(f_rows, bias_rows):
    return pl.pallas_call(
        _fox_cum_kernel,
        out_shape=jax.ShapeDtypeStruct(f_rows.shape, _F32),
        name="fox_cumsum",
    )(f_rows, bias_rows)


def _fox_kernel(q_ref, k_ref, v_ref, cq_ref, ck_ref, o_ref, m_ref, acc_ref, *, bq, bk, d, hp, scale):
    qi = pl.program_id(2)
    m_ref[...] = jnp.full(m_ref.shape, _NEG_INF, _F32)
    acc_ref[...] = jnp.zeros(acc_ref.shape, _F32)
    q = (q_ref[...].astype(_F32) * (scale * _LOG2E)).astype(_BF16)
    cq = [jnp.broadcast_to(cq_ref[0, h] * _LOG2E, (bq, _V7X_LANES)) for h in range(hp)]
    ones = jnp.ones((bk, _V7X_LANES), _BF16)

    def step(j, masked, row0):
        k0 = pl.multiple_of(j * bk, bk)
        rows = slice(row0, bq)
        k = k_ref[pl.ds(k0, bk), :]
        v = v_ref[pl.ds(k0, bk), :]
        for h in range(hp):
            cols = slice(h * d, (h + 1) * d)
            s = _qk(q[rows, cols], k[:, cols])
            ck = ck_ref[0, h, pl.ds(j, 1), :] * _LOG2E
            s = jnp.concatenate([st + (cq[h][rows] - ct)
                                 for st, ct in zip(_lane_tiles(s, bk), _lane_tiles(ck, bk))], axis=1)
            if masked:
                s = _causal_mask(s, qi * bq + row0, k0)
            m_new, alpha, p = _softmax_block(s, m_ref[h, rows])
            v_ones = jnp.concatenate([v[:, cols], ones], axis=1)
            acc_ref[h, rows] = (_lane_repeat(alpha, d + _V7X_LANES) * acc_ref[h, rows]
                                + jnp.dot(p.astype(_BF16), v_ones, preferred_element_type=_F32))
            m_ref[h, rows] = m_new

    _kv_sweep(qi, bq // bk, bk, step)
    for h in range(hp):
        acc = acc_ref[h]
        o_ref[:, h * d:(h + 1) * d] = (acc[:, :d] / _lane_repeat(acc[:, d:], d)).astype(o_ref.dtype)


def _fox_attention(p, cum, batch, seq_len, col_q, col_k, col_v, heads, d):
    n = p.shape[0]
    bq, bk = _flash_blocks(seq_len, 4)
    nq, nk = seq_len // bq, seq_len // bk
    hp = 4 if heads % 4 == 0 else 1
    w = hp * d
    cum_rows = cum.reshape(batch, heads, nk, bk)
    cum_cols = cum.reshape(batch, heads, seq_len, 1)
    vmem = (2 * seq_len * w * 2 + 4 * bq * w * 2 + 2 * hp * seq_len * 4 + 2 * hp * bq * _V7X_LANES * 4
            + hp * bq * (d + 2 * _V7X_LANES) * 4 + 12 * bq * bk * 4)
    return pl.pallas_call(
        functools.partial(_fox_kernel, bq=bq, bk=bk, d=d, hp=hp, scale=d ** -0.5),
        grid=(batch, heads // hp, nq),
        in_specs=[pl.BlockSpec((bq, w), lambda b, h, i: (b * nq + i, col_q // w + h)),
                  _resident_spec(seq_len, w, col_k), _resident_spec(seq_len, w, col_v),
                  pl.BlockSpec((1, hp, bq, 1), lambda b, h, i: (b, h, i, 0)),
                  pl.BlockSpec((1, hp, nk, bk), lambda b, h, i: (b, h, 0, 0))],
        out_specs=pl.BlockSpec((bq, w), lambda b, h, i: (b * nq + i, h)),
        out_shape=jax.ShapeDtypeStruct((n, heads * d), _BF16),
        scratch_shapes=[pltpu.VMEM((hp, bq, _V7X_LANES), _F32), pltpu.VMEM((hp, bq, d + _V7X_LANES), _F32)],
        compiler_params=_params(("parallel", "parallel", "arbitrary"), vmem),
        name="fox_attention",
    )(p, p, p, cum_cols, cum_rows)


def _lane_column(x, h):
    lane = lax.broadcasted_iota(jnp.int32, x.shape, 1)
    return jnp.sum(jnp.where(lane == h, x, 0.0), axis=-1, keepdims=True)


def _bdot(a, b):
    return jnp.dot(a.astype(_BF16), b.astype(_BF16), preferred_element_type=_F32)


_INV_BASE_LOG2 = 4


def _unit_lower_inverses(mats, ri, ci):
    c = mats[0].shape[0]
    same_block = lambda log2n: (ri >> log2n) == (ci >> log2n)
    eye = (ri == ci).astype(_F32)
    base = same_block(_INV_BASE_LOG2)
    pw = [jnp.where(base, -a, 0.0) for a in mats]
    tinv = [eye + x for x in pw]
    for _ in range(_INV_BASE_LOG2 - 1):
        pw = [_bdot(x, x) for x in pw]
        tinv = [t + _bdot(t, x) for t, x in zip(tinv, pw)]
    log2n = _INV_BASE_LOG2
    while (1 << log2n) < c:
        level = same_block(log2n + 1) & jnp.logical_not(same_block(log2n))
        off = [_bdot(t, jnp.where(level, a, 0.0)) for t, a in zip(tinv, mats)]
        tinv = [t - _bdot(o, t) for t, o in zip(tinv, off)]
        log2n += 1
    return tinv


def _deltanet_kernel(q_ref, k_ref, v_ref, z_ref, gc_ref, gr_ref, cw_ref, alog_c_ref, dtb_c_ref,
                     alog_r_ref, dtb_r_ref, ng_ref, o_ref, s_ref, tail_ref, *, nb, heads, d, c):
    t = pl.program_id(0)
    width = heads * d

    @pl.when(t == 0)
    def _():
        s_ref[...] = jnp.zeros(s_ref.shape, _F32)
        tail_ref[...] = jnp.zeros(tail_ref.shape, _F32)

    ri = lax.broadcasted_iota(jnp.int32, (c, c), 0)
    ci = lax.broadcasted_iota(jnp.int32, (c, c), 1)
    incl = ri >= ci
    strict = ri > ci
    lower = incl.astype(_F32)
    upper = (ri <= ci).astype(_F32)
    hp = lax.Precision.HIGHEST
    cw = cw_ref[...]
    unit = lambda x: x * lax.rsqrt(jnp.sum(x * x, axis=-1, keepdims=True) + _EPS)

    q, k, v, gc, beta, gcr = [], [], [], [], [], []
    for b in range(nb):
        conv = []
        for idx, ref in enumerate((q_ref, k_ref, v_ref)):
            raw = ref[b].astype(_F32)
            cols = slice(idx * width, (idx + 1) * width)
            conv.append(_silu(_causal_conv(raw, tail_ref[b, :, cols], cw[:, cols])))
            tail_ref[b, :, cols] = raw[c - _V7X_SUBLANES:]
        gates_c = gc_ref[b]
        g_cols = -jnp.exp(alog_c_ref[...]) * _softplus(gates_c + dtb_c_ref[...])
        beta_cols = 1.0 / (1.0 + jnp.exp(-gates_c))
        g_rows = -jnp.exp(alog_r_ref[...]) * _softplus(gr_ref[b][:heads] + dtb_r_ref[...])
        gcum_cols = jnp.dot(lower, g_cols, precision=hp, preferred_element_type=_F32)
        gcum_rows = jnp.dot(g_rows, upper, precision=hp, preferred_element_type=_F32)
        for h in range(heads):
            hs = slice(h * d, (h + 1) * d)
            q.append(unit(conv[0][:, hs]) * (d ** -0.5))
            k.append(unit(conv[1][:, hs]))
            v.append(conv[2][:, hs])
            gc.append(_lane_column(gcum_cols, h))
            beta.append(_lane_column(beta_cols, heads + h))
            gcr.append(gcum_rows[h:h + 1])
    chains = range(nb * heads)
    decay = [jnp.exp(jnp.where(incl, gc[n] - gcr[n], _NEG_INF)) for n in chains]
    k16 = [x.astype(_BF16) for x in k]
    kb = [k[n] * beta[n] for n in chains]
    a = [jnp.where(strict, _qk(kb[n].astype(_BF16), k16[n]) * decay[n], 0.0) for n in chains]
    tinv = _unit_lower_inverses(a, ri, ci)
    egc = [jnp.exp(x) for x in gc]
    wu = [_bdot(tinv[n], jnp.concatenate([kb[n] * egc[n], v[n] * beta[n]], axis=1)) for n in chains]
    aqk = [_qk(q[n].astype(_BF16), k16[n]) * decay[n] for n in chains]
    s = [s_ref[n // heads, n % heads] for n in chains]
    ws_qs = [_bdot(jnp.concatenate([wu[n][:, :d], q[n] * egc[n]], axis=0), s[n]) for n in chains]
    v_new = [wu[n][:, d:] - ws_qs[n][:c] for n in chains]
    o = [ws_qs[n][c:] + _bdot(aqk[n], v_new[n]) for n in chains]
    g_last = [x[c - 1:c] for x in gc]
    k_dec = [k[n] * jnp.exp(g_last[n] - gc[n]) for n in chains]
    for n in chains:
        s_ref[n // heads, n % heads] = s[n] * jnp.exp(g_last[n]) + lax.dot_general(
            k_dec[n].astype(_BF16), v_new[n].astype(_BF16), (((0,), (0,)), ((), ())), preferred_element_type=_F32)
    ng = ng_ref[...]
    for n in chains:
        hs = slice((n % heads) * d, (n % heads + 1) * d)
        on = o[n] * lax.rsqrt(jnp.mean(o[n] * o[n], axis=-1, keepdims=True) + _EPS) * ng
        o_ref[n // heads, :, hs] = (on * _silu(z_ref[n // heads, :, hs].astype(_F32))).astype(o_ref.dtype)


def _gated_deltanet(p, gates_cols, gates_rows, conv_w, a_log, dt_bias, norm_g, batch, seq_len,
                    col_q, col_k, col_v, col_z, heads, d):
    n = p.shape[0]
    c = _tile(seq_len, _DN_CHUNK, _BF16_ROWS)
    nt = seq_len // c
    width = heads * d
    lanes = _V7X_LANES
    kw = conv_w.shape[0]
    p3 = p.reshape(batch, seq_len, p.shape[1])
    col = lambda off: pl.BlockSpec((batch, c, width), lambda t: (0, t, off // width))
    const = lambda shape: pl.BlockSpec(shape, lambda t: (0,) * len(shape))
    pad_lane = lambda v: jnp.zeros((1, lanes), _F32).at[0, :heads].set(v.astype(_F32))
    vmem = batch * (2 * 5 * c * width * 2 + 14 * c * width * 4 + heads * d * d * 4) + 2 * kw * 3 * width * 4
    out = pl.pallas_call(
        functools.partial(_deltanet_kernel, nb=batch, heads=heads, d=d, c=c),
        grid=(nt,),
        in_specs=[col(col_q), col(col_k), col(col_v), col(col_z),
                  pl.BlockSpec((batch, c, lanes), lambda t: (0, t, 0)),
                  pl.BlockSpec((batch, gates_rows.shape[1], c), lambda t: (0, 0, t)),
                  const((kw, 3 * width)), const((1, lanes)), const((1, lanes)),
                  const((heads, 1)), const((heads, 1)), const((1, d))],
        out_specs=pl.BlockSpec((batch, c, width), lambda t: (0, t, 0)),
        out_shape=jax.ShapeDtypeStruct((batch, seq_len, width), _BF16),
        scratch_shapes=[pltpu.VMEM((batch, heads, d, d), _F32),
                        pltpu.VMEM((batch, _V7X_SUBLANES, 3 * width), _F32)],
        compiler_params=_params(("arbitrary",), vmem),
        name="gated_deltanet",
    )(p3, p3, p3, p3, gates_cols.reshape(batch, seq_len, gates_cols.shape[1]), gates_rows, conv_w.astype(_F32),
      pad_lane(a_log), pad_lane(dt_bias), a_log.reshape(heads, 1).astype(_F32),
      dt_bias.reshape(heads, 1).astype(_F32), norm_g.reshape(1, d).astype(_F32))
    return out.reshape(n, width)


def _ffn_in_kernel(h_ref, ssq_ref, wg_ref, wu_ref, cw_ref, o_ref, tail_ref, *, blocks_per_seq):
    i, j = pl.program_id(0), pl.program_id(1)
    tm, tn = o_ref.shape

    @pl.when(i % blocks_per_seq == 0)
    def _():
        tail_ref[j] = jnp.zeros(tail_ref.shape[1:], _F32)

    h = h_ref[...]
    inv_rms = _lane_repeat(_inv_rms(ssq_ref[...], h.shape[1]), tn)
    g = jnp.dot(h, wg_ref[...], preferred_element_type=_F32) * inv_rms
    sg = _silu(_causal_conv(g, tail_ref[j], cw_ref[...])) * inv_rms
    tail_ref[j] = g[tm - _V7X_SUBLANES:]
    u = jnp.dot(h, wu_ref[...], preferred_element_type=_F32)
    o_ref[...] = (sg * u).astype(o_ref.dtype)


def _ffn_in(h, row_ssq, w_gate, w_up, conv_w, layer, seq_len, tm_target=1024, tn_target=512):
    n, d = h.shape
    ff = w_gate.shape[2]
    tm = _tile(seq_len, tm_target, _BF16_ROWS)
    tn = _tile(ff, tn_target, _V7X_LANES)
    kw = conv_w.shape[1]
    wspec = pl.BlockSpec((None, d, tn), lambda i, j: (layer, 0, j))
    vmem = 2 * 2 * d * (tm + 2 * tn) + 2 * tm * tn * 2 + 6 * tm * tn * 4 + ff * _V7X_SUBLANES * 4
    return pl.pallas_call(
        functools.partial(_ffn_in_kernel, blocks_per_seq=seq_len // tm),
        grid=(n // tm, ff // tn),
        in_specs=[pl.BlockSpec((tm, d), lambda i, j: (i, 0)),
                  pl.BlockSpec((tm, _V7X_LANES), lambda i, j: (i, 0)), wspec, wspec,
                  pl.BlockSpec((None, kw, tn), lambda i, j: (layer, 0, j))],
        out_specs=pl.BlockSpec((tm, tn), lambda i, j: (i, j)),
        out_shape=jax.ShapeDtypeStruct((n, ff), _BF16),
        scratch_shapes=[pltpu.VMEM((ff // tn, _V7X_SUBLANES, tn), _F32)],
        compiler_params=_params(("arbitrary", "arbitrary"), vmem),
        name="ffn_in",
    )(h, row_ssq, w_gate, w_up, conv_w)


def _cast_pad_kernel(x_ref, o_ref, *, rows, cols):
    i, j = pl.program_id(1), pl.program_id(2)
    tr, tc = o_ref.shape
    r = i * tr + lax.broadcasted_iota(jnp.int32, (tr, tc), 0)
    c = j * tc + lax.broadcasted_iota(jnp.int32, (tr, tc), 1)
    o_ref[...] = jnp.where(jnp.logical_and(r < rows, c < cols), x_ref[...], 0.0).astype(o_ref.dtype)


def _cast_pad(w, rows_to, cols_to, dtype):
    layers, rows, cols = w.shape
    tr = _tile(rows_to, 1024, _BF16_ROWS)
    tc = _tile(cols_to, 1024, _V7X_LANES)
    assert rows_to - rows < tr and cols_to - cols < tc
    return pl.pallas_call(
        functools.partial(_cast_pad_kernel, rows=rows, cols=cols),
        grid=(layers, rows_to // tr, cols_to // tc),
        in_specs=[pl.BlockSpec((None, tr, tc), lambda l, i, j: (l, i, j))],
        out_specs=pl.BlockSpec((None, tr, tc), lambda l, i, j: (l, i, j)),
        out_shape=jax.ShapeDtypeStruct((layers, rows_to, cols_to), dtype),
        compiler_params=_params(("parallel", "parallel", "parallel"), 2 * tr * tc * 6),
        name="cast_pad",
    )(w)


def _pad_last(w, total):
    return jnp.pad(w, ((0, 0),) * (w.ndim - 1) + ((0, total - w.shape[-1]),))


def _bf16_weights(w_in, w_out, w_gate, w_up, ffn_conv, w_down, gw, dn_heads):
    o_dn_a = 10 * gw
    o_fx = o_dn_a + 2 * dn_heads
    o_fx_f = o_fx + 3 * gw
    tile2 = o_fx_f // _V7X_LANES * _V7X_LANES
    narrow = jnp.concatenate([w_in[..., o_dn_a:o_dn_a + _V7X_LANES], _pad_last(w_in[..., tile2:], _V7X_LANES)],
                             axis=-1).astype(_BF16)
    ff = w_gate.shape[-1]
    ffp = -(-ff // _FF_ALIGN) * _FF_ALIGN if ff > _FF_ALIGN else ff
    return dict(
        w_in=w_in.astype(_BF16), n_wide_a=o_dn_a, wide_b=w_in[..., o_fx:o_fx_f].astype(_BF16),
        narrow=narrow, fox_f_lane=_V7X_LANES + o_fx_f - tile2, out=w_out.astype(_BF16),
        gate=_cast_pad(w_gate, w_gate.shape[1], ffp, _BF16), up=_cast_pad(w_up, w_up.shape[1], ffp, _BF16),
        conv=_pad_last(ffn_conv.astype(_F32), ffp),
        down=_cast_pad(w_down, ffp, w_down.shape[2], _BF16))


def _layer(stream, layer, batch, seq_len, wts, sc_conv, lam_vecs, diff_norm, dn_conv, dn_a_log,
           dn_dt_bias, dn_norm, fox_bias, ffn_norm, next_attn_norm):
    x, h, ssq = stream
    d = lam_vecs[0].shape[-1]
    gw = wts["out"].shape[1] // 4
    dn_heads, fox_heads = dn_a_log.shape[-1], fox_bias.shape[-1]
    diff_heads = gw // diff_norm.shape[-1]
    col = lambda idx: idx * gw

    p = _matmul([h], wts["w_in"], layer, _BF16, row_ssq=ssq, n_cols=wts["n_wide_a"], name="in_proj")
    p_fox = _matmul([h], wts["wide_b"], layer, _BF16, row_ssq=ssq, name="in_proj_fox")
    gates = _matmul([h], wts["narrow"], layer, _F32, row_ssq=ssq, name="in_proj_gates")

    y_sc = _short_conv(p, sc_conv, seq_len, 0, 1, 2, gw)
    lam_init = 0.8 - 0.6 * math.exp(-0.3 * layer)
    y_df = _diff_attention(p, lam_vecs, diff_norm, lam_init, batch, seq_len, col(3), col(4), col(5), diff_heads, d)

    gates_t = gates.reshape(batch, seq_len, gates.shape[1]).transpose(0, 2, 1)
    f0 = wts["fox_f_lane"]
    fox_rows = gates_t[:, f0:f0 + fox_heads].reshape(batch * fox_heads, seq_len)
    fox_bias_rows = jnp.tile(fox_bias.astype(_F32), batch).reshape(batch * fox_heads, 1)
    cum = _fox_cumsum(fox_rows, fox_bias_rows)
    y_fx = _fox_attention(p_fox, cum, batch, seq_len, col(0), col(1), col(2), fox_heads, d)

    y_dn = _gated_deltanet(p, gates, gates_t[:, :2 * dn_heads], dn_conv, dn_a_log, dn_dt_bias, dn_norm,
                           batch, seq_len, col(6), col(7), col(8), col(9), dn_heads, d)

    x, h, ssq = _matmul([y_sc, y_df, y_dn, y_fx], wts["out"], layer, _F32, residual=x, next_gain=ffn_norm,
                        tn_target=512, name="out_proj")
    act = _ffn_in(h, ssq, wts["gate"], wts["up"], wts["conv"], layer, seq_len)
    if next_attn_norm is None:
        return _matmul_acc(act, wts["down"], layer, x, name="ffn_out"), None, None
    return _matmul_acc(act, wts["down"], layer, x, next_gain=next_attn_norm, name="ffn_out")


def kernel(x, attn_norm, w_in, sc_conv, lam_q1, lam_k1, lam_q2, lam_k2, diff_norm, dn_conv, dn_a_log,
           dn_dt_bias, dn_norm, fox_bias, w_out, ffn_norm, w_gate, w_up, ffn_conv, w_down, final_norm):
    batch, seq_len, d_model = x.shape
    xf = x.reshape(batch * seq_len, d_model).astype(_F32)
    wts = _bf16_weights(w_in, w_out, w_gate, w_up, ffn_conv, w_down, w_out.shape[1] // 4, dn_a_log.shape[-1])
    layers = w_in.shape[0]
    stream = (xf,) + tuple(_stream_stats(xf, attn_norm[0]))
    for l in range(layers):
        stream = _layer(stream, l, batch, seq_len, wts, sc_conv[l],
                        (lam_q1[l], lam_k1[l], lam_q2[l], lam_k2[l]), diff_norm[l], dn_conv[l], dn_a_log[l],
                        dn_dt_bias[l], dn_norm[l], fox_bias[l], ffn_norm[l],
                        attn_norm[l + 1] if l + 1 < layers else None)
    out = _rmsnorm(stream[0], final_norm, x.dtype)
    return out.reshape(batch, seq_len, d_model)
```

```python
import functools
import math

import jax
import jax.numpy as jnp
from jax import lax
from jax.experimental import pallas as pl
from jax.experimental.pallas import tpu as pltpu

_F32 = jnp.float32
_BF16 = jnp.bfloat16
_EPS = 1e-6
_NEG_INF = float("-inf")

_V7X_VMEM_BYTES = 64 * 1024 * 1024
_V7X_LANES = 128
_V7X_SUBLANES = 8
_BF16_ROWS = 16
_DN_CHUNK = 128
_FF_ALIGN = 1024


def _tile(n, target, align):
    t = min(target, n)
    t -= t % align
    while t >= align:
        if n % t == 0:
            return t
        t -= align
    return n


def _params(semantics, vmem_bytes):
    limit = min(int(vmem_bytes) + (8 << 20), _V7X_VMEM_BYTES - (4 << 20))
    return pltpu.CompilerParams(dimension_semantics=semantics, vmem_limit_bytes=limit)


def _silu(x):
    return x * (1.0 / (1.0 + jnp.exp(-x)))


def _softplus(x):
    return jnp.maximum(x, 0.0) + jnp.log1p(jnp.exp(-jnp.abs(x)))


def _shift_rows(x, prev8, s):
    xs = pltpu.roll(x, s, 0)
    ps = pltpu.roll(prev8, s, 0)
    row = lax.broadcasted_iota(jnp.int32, prev8.shape, 0)
    head = jnp.where(row < s, ps, xs[:_V7X_SUBLANES])
    return jnp.concatenate([head, xs[_V7X_SUBLANES:]], axis=0)


def _causal_conv(x, prev8, w):
    k = w.shape[0]
    out = x * w[k - 1:k]
    for j in range(k - 1):
        out = out + _shift_rows(x, prev8, k - 1 - j) * w[j:j + 1]
    return out


def _rmsnorm_kernel(x_ref, g_ref, o_ref):
    x = x_ref[...]
    y = x * lax.rsqrt(jnp.mean(x * x, axis=-1, keepdims=True) + _EPS)
    o_ref[...] = (y * g_ref[...]).astype(o_ref.dtype)


def _rmsnorm(x, g, out_dtype):
    n, d = x.shape
    tm = _tile(n, 512, _BF16_ROWS)
    vmem = 2 * tm * d * (4 + jnp.dtype(out_dtype).itemsize) + 2 * d * 4
    return pl.pallas_call(
        _rmsnorm_kernel,
        grid=(n // tm,),
        in_specs=[pl.BlockSpec((tm, d), lambda i: (i, 0)), pl.BlockSpec((1, d), lambda i: (0, 0))],
        out_specs=pl.BlockSpec((tm, d), lambda i: (i, 0)),
        out_shape=jax.ShapeDtypeStruct((n, d), out_dtype),
        compiler_params=_params(("parallel",), vmem),
        name="rmsnorm",
    )(x, g.reshape(1, d).astype(_F32))


def _stream_stats_kernel(x_ref, g_ref, xb_ref, ssq_ref):
    x = x_ref[...]
    xb_ref[...] = (x * g_ref[...]).astype(xb_ref.dtype)
    ssq_ref[...] = jnp.broadcast_to(jnp.sum(x * x, axis=-1, keepdims=True), ssq_ref.shape)


def _stream_stats(x, gain):
    n, d = x.shape
    tm = _tile(n, 512, _BF16_ROWS)
    return pl.pallas_call(
        _stream_stats_kernel,
        grid=(n // tm,),
        in_specs=[pl.BlockSpec((tm, d), lambda i: (i, 0)), pl.BlockSpec((1, d), lambda i: (0, 0))],
        out_specs=[pl.BlockSpec((tm, d), lambda i: (i, 0)), pl.BlockSpec((tm, _V7X_LANES), lambda i: (i, 0))],
        out_shape=[jax.ShapeDtypeStruct((n, d), _BF16), jax.ShapeDtypeStruct((n, _V7X_LANES), _F32)],
        compiler_params=_params(("parallel",), 2 * tm * d * 6),
        name="stream_stats",
    )(x, gain.reshape(1, d).astype(_F32))


def _inv_rms(ssq, width):
    return lax.rsqrt(ssq * (1.0 / width) + _EPS)


def _emit_stream_stats(x_new, first, gain_ref, xb_ref, ssq_ref):
    xb_ref[...] = (x_new * gain_ref[...]).astype(xb_ref.dtype)
    part = jnp.sum(x_new * x_new, axis=-1, keepdims=True)

    @pl.when(first)
    def _():
        ssq_ref[...] = jnp.broadcast_to(part, ssq_ref.shape)

    @pl.when(jnp.logical_not(first))
    def _():
        ssq_ref[...] += part


def _mm_kernel(*refs, n_pairs, has_res, has_ssq, emit_stats):
    a_refs, w_refs = refs[:n_pairs], refs[n_pairs:2 * n_pairs]
    extra = list(refs[2 * n_pairs:])
    res_ref = extra.pop(0) if has_res else None
    ssq_in_ref = extra.pop(0) if has_ssq else None
    gain_ref = extra.pop(0) if emit_stats else None
    o_ref = extra.pop(0)
    acc = jnp.dot(a_refs[0][...], w_refs[0][...], preferred_element_type=_F32)
    for a_ref, w_ref in zip(a_refs[1:], w_refs[1:]):
        acc = acc + jnp.dot(a_ref[...], w_ref[...], preferred_element_type=_F32)
    if has_ssq:
        acc = acc * _lane_repeat(_inv_rms(ssq_in_ref[...], a_refs[0].shape[1]), acc.shape[1])
    if has_res:
        acc = acc + res_ref[...]
    o_ref[...] = acc.astype(o_ref.dtype)
    if emit_stats:
        _emit_stream_stats(acc, pl.program_id(1) == 0, gain_ref, *extra)


def _matmul(a_list, w, layer, out_dtype, residual=None, row_ssq=None, next_gain=None, n_cols=None,
            tm_target=1024, tn_target=1024, name="matmul"):
    emit_stats = next_gain is not None
    m, n = a_list[0].shape[0], n_cols or w.shape[2]
    kb = a_list[0].shape[1]
    assert all(a.shape[1] == kb for a in a_list) and w.shape[1] == kb * len(a_list)
    assert row_ssq is None or len(a_list) == 1
    tm = _tile(m, tm_target, _BF16_ROWS)
    tn = _tile(n, tn_target, _V7X_LANES)
    ktot = w.shape[1]
    in_specs = [pl.BlockSpec((tm, kb), lambda i, j: (i, 0)) for _ in a_list]
    in_specs += [pl.BlockSpec((None, kb, tn), functools.partial(lambda i, j, r: (layer, r, j), r=r))
                 for r in range(len(a_list))]
    args = list(a_list) + [w] * len(a_list)
    vmem = 2 * 2 * ktot * (tm + tn) + 2 * tm * tn * jnp.dtype(out_dtype).itemsize + tm * tn * 4
    if residual is not None:
        in_specs.append(pl.BlockSpec((tm, tn), lambda i, j: (i, j)))
        args.append(residual)
        vmem += 2 * tm * tn * 4
    stats_spec = pl.BlockSpec((tm, _V7X_LANES), lambda i, j: (i, 0))
    if row_ssq is not None:
        in_specs.append(stats_spec)
        args.append(row_ssq)
    out_specs = pl.BlockSpec((tm, tn), lambda i, j: (i, j))
    out_shape = jax.ShapeDtypeStruct((m, n), out_dtype)
    if emit_stats:
        in_specs.append(pl.BlockSpec((1, tn), lambda i, j: (0, j)))
        args.append(next_gain.reshape(1, n).astype(_F32))
        out_specs = [out_specs, pl.BlockSpec((tm, tn), lambda i, j: (i, j)), stats_spec]
        out_shape = [out_shape, jax.ShapeDtypeStruct((m, n), _BF16), jax.ShapeDtypeStruct((m, _V7X_LANES), _F32)]
        vmem += 2 * tm * tn * 2
    return pl.pallas_call(
        functools.partial(_mm_kernel, n_pairs=len(a_list), has_res=residual is not None,
                          has_ssq=row_ssq is not None, emit_stats=emit_stats),
        grid=(m // tm, n // tn),
        in_specs=in_specs,
        out_specs=out_specs,
        out_shape=out_shape,
        compiler_params=_params(("parallel", "arbitrary" if emit_stats else "parallel"), vmem),
        name=name,
    )(*args)


def _mm_acc_kernel(a_ref, w_ref, r_ref, *rest, nk, emit_stats):
    gain_ref = rest[0] if emit_stats else None
    o_ref, acc_ref = rest[1 if emit_stats else 0], rest[-1]
    k = pl.program_id(2)
    prod = lambda: jnp.dot(a_ref[...], w_ref[...], preferred_element_type=_F32)

    def finish(x_new):
        o_ref[...] = x_new
        if emit_stats:
            _emit_stream_stats(x_new, pl.program_id(1) == 0, gain_ref, rest[2], rest[3])

    if nk == 1:
        finish(r_ref[...] + prod())
        return

    @pl.when(k == 0)
    def _():
        acc_ref[...] = r_ref[...] + prod()

    @pl.when(jnp.logical_and(k > 0, k < nk - 1))
    def _():
        acc_ref[...] += prod()

    @pl.when(k == nk - 1)
    def _():
        finish(acc_ref[...] + prod())


def _matmul_acc(a, w, layer, residual, next_gain=None, tm_target=1024, tn_target=1024, tk_target=2816,
                name="matmul_acc"):
    emit_stats = next_gain is not None
    m, kdim = a.shape
    n = w.shape[2]
    tm = _tile(m, tm_target, _BF16_ROWS)
    tn = _tile(n, tn_target, _V7X_LANES)
    tk = _tile(kdim, tk_target, _V7X_LANES)
    vmem = 2 * 2 * tk * (tm + tn) + 5 * tm * tn * 4
    tile_spec = pl.BlockSpec((tm, tn), lambda i, j, k: (i, j))
    out_specs, out_shape = tile_spec, jax.ShapeDtypeStruct((m, n), _F32)
    in_specs = [pl.BlockSpec((tm, tk), lambda i, j, k: (i, k)),
                pl.BlockSpec((None, tk, tn), lambda i, j, k: (layer, k, j)), tile_spec]
    args = [a, w, residual]
    if emit_stats:
        in_specs.append(pl.BlockSpec((1, tn), lambda i, j, k: (0, j)))
        args.append(next_gain.reshape(1, n).astype(_F32))
        out_specs = [tile_spec, tile_spec, pl.BlockSpec((tm, _V7X_LANES), lambda i, j, k: (i, 0))]
        out_shape = [out_shape, jax.ShapeDtypeStruct((m, n), _BF16), jax.ShapeDtypeStruct((m, _V7X_LANES), _F32)]
        vmem += 2 * tm * tn * 2
    return pl.pallas_call(
        functools.partial(_mm_acc_kernel, nk=kdim // tk, emit_stats=emit_stats),
        grid=(m // tm, n // tn, kdim // tk),
        in_specs=in_specs,
        out_specs=out_specs,
        out_shape=out_shape,
        scratch_shapes=[pltpu.VMEM((tm, tn), _F32)],
        compiler_params=_params(("parallel", "arbitrary" if emit_stats else "parallel", "arbitrary"), vmem),
        name=name,
    )(*args)


def _sconv_kernel(h_ref, c_ref, b_ref, hh_ref, ch_ref, w_ref, o_ref, *, blocks_per_seq):
    i = pl.program_id(0)
    x = c_ref[...].astype(_F32) * h_ref[...].astype(_F32)
    halo = ch_ref[...].astype(_F32) * hh_ref[...].astype(_F32)
    prev8 = jnp.where(i % blocks_per_seq == 0, 0.0, halo[_BF16_ROWS - _V7X_SUBLANES:])
    y = b_ref[...].astype(_F32) * _causal_conv(x, prev8, w_ref[...])
    o_ref[...] = y.astype(o_ref.dtype)


def _short_conv(p, conv_w, seq_len, col_h, col_c, col_b, width):
    n = p.shape[0]
    tm = _tile(seq_len, 512, _BF16_ROWS)
    hb = tm // _BF16_ROWS
    cur = lambda c: pl.BlockSpec((tm, width), lambda i: (i, c))
    halo = lambda c: pl.BlockSpec((_BF16_ROWS, width), lambda i: (jnp.maximum(i * hb - 1, 0), c))
    k = conv_w.shape[0]
    vmem = 2 * (4 * tm + 2 * _BF16_ROWS) * width * 2 + 8 * tm * width * 4
    return pl.pallas_call(
        functools.partial(_sconv_kernel, blocks_per_seq=seq_len // tm),
        grid=(n // tm,),
        in_specs=[cur(col_h), cur(col_c), cur(col_b), halo(col_h), halo(col_c),
                  pl.BlockSpec((k, width), lambda i: (0, 0))],
        out_specs=pl.BlockSpec((tm, width), lambda i: (i, 0)),
        out_shape=jax.ShapeDtypeStruct((n, width), _BF16),
        compiler_params=_params(("parallel",), vmem),
        name="short_conv",
    )(p, p, p, p, p, conv_w.astype(_F32))


_LOG2E = math.log2(math.e)


def _lane_tiles(x, width):
    return [x[:, c:c + _V7X_LANES] for c in range(0, width, _V7X_LANES)]


def _lane_repeat(x, width):
    return jnp.concatenate([x] * (width // _V7X_LANES), axis=1)


def _softmax_block(s, m_prev):
    m_new = jnp.maximum(m_prev, jnp.max(s, axis=-1, keepdims=True))
    alpha = jnp.exp2(m_prev - m_new)
    p = jnp.concatenate([jnp.exp2(t - m_new) for t in _lane_tiles(s, s.shape[1])], axis=1)
    return m_new, alpha, p


def _causal_mask(s, q0, k0):
    row = q0 + lax.broadcasted_iota(jnp.int32, s.shape, 0)
    col = k0 + lax.broadcasted_iota(jnp.int32, s.shape, 1)
    return jnp.where(col <= row, s, _NEG_INF)


def _qk(q, k):
    return lax.dot_general(q, k, (((1,), (1,)), ((), ())), preferred_element_type=_F32)


def _flash_blocks(seq_len, kv_per_q):
    bq = _tile(seq_len, 1024, _V7X_LANES)
    bk = bq // kv_per_q if bq % (kv_per_q * _V7X_LANES) == 0 else bq
    return bq, bk


def _resident_spec(seq_len, width, col):
    return pl.BlockSpec((seq_len, width), lambda b, h, i: (b, col // width + h), pipeline_mode=pl.Buffered(1))


def _kv_sweep(qi, r, bk, step):
    def trip(i, carry):
        for d in range(r):
            step(i * r + d, False, 0)
        return carry

    lax.fori_loop(0, qi, trip, 0)
    for d in range(r):
        step(qi * r + d, True, d * bk)


def _diff_kernel(q_ref, k_ref, v_ref, lq1_ref, lk1_ref, lq2_ref, lk2_ref, g_ref, o_ref,
                 m_ref, l_ref, acc_ref, *, bq, bk, d, hp, scale, lam_init):
    qi = pl.program_id(2)
    hw = 2 * d
    m_ref[...] = jnp.full(m_ref.shape, _NEG_INF, _F32)
    l_ref[...] = jnp.zeros(l_ref.shape, _F32)
    acc_ref[...] = jnp.zeros(acc_ref.shape, _F32)
    q = (q_ref[...].astype(_F32) * (scale * _LOG2E)).astype(_BF16)

    def step(j, masked, row0):
        k0 = pl.multiple_of(j * bk, bk)
        k = k_ref[pl.ds(k0, bk), :]
        v = v_ref[pl.ds(k0, bk), :]
        rows = slice(row0, bq)
        for c in range(2 * hp):
            cols = slice(c * d, (c + 1) * d)
            s = _qk(q[rows, cols], k[:, cols])
            if masked:
                s = _causal_mask(s, qi * bq + row0, k0)
            m_new, alpha, p = _softmax_block(s, m_ref[c, rows])
            l_ref[c, rows] = alpha * l_ref[c, rows] + jnp.sum(p, axis=-1, keepdims=True)
            vh = v[:, (c // 2) * hw:(c // 2 + 1) * hw]
            acc_ref[c, rows] = (_lane_repeat(alpha, hw) * acc_ref[c, rows]
                                + jnp.dot(p.astype(_BF16), vh, preferred_element_type=_F32))
            m_ref[c, rows] = m_new

    _kv_sweep(qi, bq // bk, bk, step)
    lam = (jnp.exp(jnp.sum(lq1_ref[...] * lk1_ref[...], axis=-1, keepdims=True))
           - jnp.exp(jnp.sum(lq2_ref[...] * lk2_ref[...], axis=-1, keepdims=True)) + lam_init)
    for h in range(hp):
        o = (acc_ref[2 * h] / _lane_repeat(l_ref[2 * h], hw)
             - lam * (acc_ref[2 * h + 1] / _lane_repeat(l_ref[2 * h + 1], hw)))
        o = o * lax.rsqrt(jnp.mean(o * o, axis=-1, keepdims=True) + _EPS) * g_ref[...]
        o_ref[:, h * hw:(h + 1) * hw] = (o * (1.0 - lam_init)).astype(o_ref.dtype)


def _diff_attention(p, lam_vecs, norm_g, lam_init, batch, seq_len, col_q, col_k, col_v, heads, d):
    n = p.shape[0]
    hw = 2 * d
    bq, bk = _flash_blocks(seq_len, 4)
    nq = seq_len // bq
    hp = 2 if heads % 2 == 0 else 1
    w = hp * hw
    vec = pl.BlockSpec((1, d), lambda b, h, i: (0, 0))
    vmem = 2 * seq_len * w * 2 + 4 * bq * w * 2 + 2 * hp * bq * (hw + 2 * _V7X_LANES) * 4 + 12 * bq * bk * 4
    return pl.pallas_call(
        functools.partial(_diff_kernel, bq=bq, bk=bk, d=d, hp=hp, scale=d ** -0.5, lam_init=lam_init),
        grid=(batch, heads // hp, nq),
        in_specs=[pl.BlockSpec((bq, w), lambda b, h, i: (b * nq + i, col_q // w + h)),
                  _resident_spec(seq_len, w, col_k), _resident_spec(seq_len, w, col_v),
                  vec, vec, vec, vec,
                  pl.BlockSpec((1, hw), lambda b, h, i: (0, 0))],
        out_specs=pl.BlockSpec((bq, w), lambda b, h, i: (b * nq + i, h)),
        out_shape=jax.ShapeDtypeStruct((n, heads * hw), _BF16),
        scratch_shapes=[pltpu.VMEM((2 * hp, bq, _V7X_LANES), _F32), pltpu.VMEM((2 * hp, bq, _V7X_LANES), _F32),
                        pltpu.VMEM((2 * hp, bq, hw), _F32)],
        compiler_params=_params(("parallel", "parallel", "arbitrary"), vmem),
        name="diff_attention",
    )(p, p, p, *[v.reshape(1, d).astype(_F32) for v in lam_vecs], norm_g.reshape(1, hw).astype(_F32))


def _fox_cum_kernel(f_ref, bias_ref, o_ref):
    x = f_ref[...] + bias_ref[...]
    ls = jnp.minimum(x, 0.0) - jnp.log1p(jnp.exp(-jnp.abs(x)))
    r = lax.broadcasted_iota(jnp.int32, (_V7X_LANES, _V7X_LANES), 0)
    c = lax.broadcasted_iota(jnp.int32, (_V7X_LANES, _V7X_LANES), 1)
    upper = (r <= c).astype(_F32)
    carry = jnp.zeros((x.shape[0], 1), _F32)
    for t in range(x.shape[1] // _V7X_LANES):
        sl = slice(t * _V7X_LANES, (t + 1) * _V7X_LANES)
        w = jnp.dot(ls[:, sl], upper, precision=lax.Precision.HIGHEST, preferred_element_type=_F32) + carry
        o_ref[:, sl] = w
        carry = w[:, _V7X_LANES - 1:]


def _fox_cumsum(f_rows, bias_rows):
    return pl.pallas_call(
        _fox_cum_kernel,
        out_shape=jax.ShapeDtypeStruct(f_rows.shape, _F32),
        name="fox_cumsum",
    )(f_rows, bias_rows)


def _fox_kernel(q_ref, k_ref, v_ref, cq_ref, ck_ref, o_ref, m_ref, acc_ref, *, bq, bk, d, hp, scale):
    qi = pl.program_id(2)
    m_ref[...] = jnp.full(m_ref.shape, _NEG_INF, _F32)
    acc_ref[...] = jnp.zeros(acc_ref.shape, _F32)
    q = (q_ref[...].astype(_F32) * (scale * _LOG2E)).astype(_BF16)
    cq = [jnp.broadcast_to(cq_ref[0, h] * _LOG2E, (bq, _V7X_LANES)) for h in range(hp)]
    ones = jnp.ones((bk, _V7X_LANES), _BF16)

    def step(j, masked, row0):
        k0 = pl.multiple_of(j * bk, bk)
        rows = slice(row0, bq)
        k = k_ref[pl.ds(k0, bk), :]
        v = v_ref[pl.ds(k0, bk), :]
        for h in range(hp):
            cols = slice(h * d, (h + 1) * d)
            s = _qk(q[rows, cols], k[:, cols])
            ck = ck_ref[0, h, pl.ds(j, 1), :] * _LOG2E
            s = jnp.concatenate([st + (cq[h][rows] - ct)
                                 for st, ct in zip(_lane_tiles(s, bk), _lane_tiles(ck, bk))], axis=1)
            if masked:
                s = _causal_mask(s, qi * bq + row0, k0)
            m_new, alpha, p = _softmax_block(s, m_ref[h, rows])
            v_ones = jnp.concatenate([v[:, cols], ones], axis=1)
            acc_ref[h, rows] = (_lane_repeat(alpha, d + _V7X_LANES) * acc_ref[h, rows]
                                + jnp.dot(p.astype(_BF16), v_ones, preferred_element_type=_F32))
            m_ref[h, rows] = m_new

    _kv_sweep(qi, bq // bk, bk, step)
    for h in range(hp):
        acc = acc_ref[h]
        o_ref[:, h * d:(h + 1) * d] = (acc[:, :d] / _lane_repeat(acc[:, d:], d)).astype(o_ref.dtype)


def _fox_attention(p, cum, batch, seq_len, col_q, col_k, col_v, heads, d):
    n = p.shape[0]
    bq, bk = _flash_blocks(seq_len, 4)
    nq, nk = seq_len // bq, seq_len // bk
    hp = 4 if heads % 4 == 0 else 1
    w = hp * d
    cum_rows = cum.reshape(batch, heads, nk, bk)
    cum_cols = cum.reshape(batch, heads, seq_len, 1)
    vmem = (2 * seq_len * w * 2 + 4 * bq * w * 2 + 2 * hp * seq_len * 4 + 2 * hp * bq * _V7X_LANES * 4
            + hp * bq * (d + 2 * _V7X_LANES) * 4 + 12 * bq * bk * 4)
    return pl.pallas_call(
        functools.partial(_fox_kernel, bq=bq, bk=bk, d=d, hp=hp, scale=d ** -0.5),
        grid=(batch, heads // hp, nq),
        in_specs=[pl.BlockSpec((bq, w), lambda b, h, i: (b * nq + i, col_q // w + h)),
                  _resident_spec(seq_len, w, col_k), _resident_spec(seq_len, w, col_v),
                  pl.BlockSpec((1, hp, bq, 1), lambda b, h, i: (b, h, i, 0)),
                  pl.BlockSpec((1, hp, nk, bk), lambda b, h, i: (b, h, 0, 0))],
        out_specs=pl.BlockSpec((bq, w), lambda b, h, i: (b * nq + i, h)),
        out_shape=jax.ShapeDtypeStruct((n, heads * d), _BF16),
        scratch_shapes=[pltpu.VMEM((hp, bq, _V7X_LANES), _F32), pltpu.VMEM((hp, bq, d + _V7X_LANES), _F32)],
        compiler_params=_params(("parallel", "parallel", "arbitrary"), vmem),
        name="fox_attention",
    )(p, p, p, cum_cols, cum_rows)


def _lane_column(x, h):
    lane = lax.broadcasted_iota(jnp.int32, x.shape, 1)
    return jnp.sum(jnp.where(lane == h, x, 0.0), axis=-1, keepdims=True)


def _bdot(a, b):
    return jnp.dot(a.astype(_BF16), b.astype(_BF16), preferred_element_type=_F32)


_INV_BASE_LOG2 = 4


def _unit_lower_inverses(mats, ri, ci):
    c = mats[0].shape[0]
    same_block = lambda log2n: (ri >> log2n) == (ci >> log2n)
    eye = (ri == ci).astype(_F32)
    base = same_block(_INV_BASE_LOG2)
    pw = [jnp.where(base, -a, 0.0) for a in mats]
    tinv = [eye + x for x in pw]
    for _ in range(_INV_BASE_LOG2 - 1):
        pw = [_bdot(x, x) for x in pw]
        tinv = [t + _bdot(t, x) for t, x in zip(tinv, pw)]
    log2n = _INV_BASE_LOG2
    while (1 << log2n) < c:
        level = same_block(log2n + 1) & jnp.logical_not(same_block(log2n))
        off = [_bdot(t, jnp.where(level, a, 0.0)) for t, a in zip(tinv, mats)]
        tinv = [t - _bdot(o, t) for t, o in zip(tinv, off)]
        log2n += 1
    return tinv


def _deltanet_kernel(q_ref, k_ref, v_ref, z_ref, gc_ref, gr_ref, cw_ref, alog_c_ref, dtb_c_ref,
                     alog_r_ref, dtb_r_ref, ng_ref, o_ref, s_ref, tail_ref, *, nb, heads, d, c):
    t = pl.program_id(0)
    width = heads * d

    @pl.when(t == 0)
    def _():
        s_ref[...] = jnp.zeros(s_ref.shape, _F32)
        tail_ref[...] = jnp.zeros(tail_ref.shape, _F32)

    ri = lax.broadcasted_iota(jnp.int32, (c, c), 0)
    ci = lax.broadcasted_iota(jnp.int32, (c, c), 1)
    incl = ri >= ci
    strict = ri > ci
    lower = incl.astype(_F32)
    upper = (ri <= ci).astype(_F32)
    hp = lax.Precision.HIGHEST
    cw = cw_ref[...]
    unit = lambda x: x * lax.rsqrt(jnp.sum(x * x, axis=-1, keepdims=True) + _EPS)

    q, k, v, gc, beta, gcr = [], [], [], [], [], []
    for b in range(nb):
        conv = []
        for idx, ref in enumerate((q_ref, k_ref, v_ref)):
            raw = ref[b].astype(_F32)
            cols = slice(idx * width, (idx + 1) * width)
            conv.append(_silu(_causal_conv(raw, tail_ref[b, :, cols], cw[:, cols])))
            tail_ref[b, :, cols] = raw[c - _V7X_SUBLANES:]
        gates_c = gc_ref[b]
        g_cols = -jnp.exp(alog_c_ref[...]) * _softplus(gates_c + dtb_c_ref[...])
        beta_cols = 1.0 / (1.0 + jnp.exp(-gates_c))
        g_rows = -jnp.exp(alog_r_ref[...]) * _softplus(gr_ref[b][:heads] + dtb_r_ref[...])
        gcum_cols = jnp.dot(lower, g_cols, precision=hp, preferred_element_type=_F32)
        gcum_rows = jnp.dot(g_rows, upper, precision=hp, preferred_element_type=_F32)
        for h in range(heads):
            hs = slice(h * d, (h + 1) * d)
            q.append(unit(conv[0][:, hs]) * (d ** -0.5))
            k.append(unit(conv[1][:, hs]))
            v.append(conv[2][:, hs])
            gc.append(_lane_column(gcum_cols, h))
            beta.append(_lane_column(beta_cols, heads + h))
            gcr.append(gcum_rows[h:h + 1])
    chains = range(nb * heads)
    decay = [jnp.exp(jnp.where(incl, gc[n] - gcr[n], _NEG_INF)) for n in chains]
    k16 = [x.astype(_BF16) for x in k]
    kb = [k[n] * beta[n] for n in chains]
    a = [jnp.where(strict, _qk(kb[n].astype(_BF16), k16[n]) * decay[n], 0.0) for n in chains]
    tinv = _unit_lower_inverses(a, ri, ci)
    egc = [jnp.exp(x) for x in gc]
    wu = [_bdot(tinv[n], jnp.concatenate([kb[n] * egc[n], v[n] * beta[n]], axis=1)) for n in chains]
    aqk = [_qk(q[n].astype(_BF16), k16[n]) * decay[n] for n in chains]
    s = [s_ref[n // heads, n % heads] for n in chains]
    ws_qs = [_bdot(jnp.concatenate([wu[n][:, :d], q[n] * egc[n]], axis=0), s[n]) for n in chains]
    v_new = [wu[n][:, d:] - ws_qs[n][:c] for n in chains]
    o = [ws_qs[n][c:] + _bdot(aqk[n], v_new[n]) for n in chains]
    g_last = [x[c - 1:c] for x in gc]
    k_dec = [k[n] * jnp.exp(g_last[n] - gc[n]) for n in chains]
    for n in chains:
        s_ref[n // heads, n % heads] = s[n] * jnp.exp(g_last[n]) + lax.dot_general(
            k_dec[n].astype(_BF16), v_new[n].astype(_BF16), (((0,), (0,)), ((), ())), preferred_element_type=_F32)
    ng = ng_ref[...]
    for n in chains:
        hs = slice((n % heads) * d, (n % heads + 1) * d)
        on = o[n] * lax.rsqrt(jnp.mean(o[n] * o[n], axis=-1, keepdims=True) + _EPS) * ng
        o_ref[n // heads, :, hs] = (on * _silu(z_ref[n // heads, :, hs].astype(_F32))).astype(o_ref.dtype)


def _gated_deltanet(p, gates_cols, gates_rows, conv_w, a_log, dt_bias, norm_g, batch, seq_len,
                    col_q, col_k, col_v, col_z, heads, d):
    n = p.shape[0]
    c = _tile(seq_len, _DN_CHUNK, _BF16_ROWS)
    nt = seq_len // c
    width = heads * d
    lanes = _V7X_LANES
    kw = conv_w.shape[0]
    p3 = p.reshape(batch, seq_len, p.shape[1])
    col = lambda off: pl.BlockSpec((batch, c, width), lambda t: (0, t, off // width))
    const = lambda shape: pl.BlockSpec(shape, lambda t: (0,) * len(shape))
    pad_lane = lambda v: jnp.zeros((1, lanes), _F32).at[0, :heads].set(v.astype(_F32))
    vmem = batch * (2 * 5 * c * width * 2 + 14 * c * width * 4 + heads * d * d * 4) + 2 * kw * 3 * width * 4
    out = pl.pallas_call(
        functools.partial(_deltanet_kernel, nb=batch, heads=heads, d=d, c=c),
        grid=(nt,),
        in_specs=[col(col_q), col(col_k), col(col_v), col(col_z),
                  pl.BlockSpec((batch, c, lanes), lambda t: (0, t, 0)),
                  pl.BlockSpec((batch, gates_rows.shape[1], c), lambda t: (0, 0, t)),
                  const((kw, 3 * width)), const((1, lanes)), const((1, lanes)),
                  const((heads, 1)), const((heads, 1)), const((1, d))],
        out_specs=pl.BlockSpec((batch, c, width), lambda t: (0, t, 0)),
        out_shape=jax.ShapeDtypeStruct((batch, seq_len, width), _BF16),
        scratch_shapes=[pltpu.VMEM((batch, heads, d, d), _F32),
                        pltpu.VMEM((batch, _V7X_SUBLANES, 3 * width), _F32)],
        compiler_params=_params(("arbitrary",), vmem),
        name="gated_deltanet",
    )(p3, p3, p3, p3, gates_cols.reshape(batch, seq_len, gates_cols.shape[1]), gates_rows, conv_w.astype(_F32),
      pad_lane(a_log), pad_lane(dt_bias), a_log.reshape(heads, 1).astype(_F32),
      dt_bias.reshape(heads, 1).astype(_F32), norm_g.reshape(1, d).astype(_F32))
    return out.reshape(n, width)


def _ffn_in_kernel(h_ref, ssq_ref, wg_ref, wu_ref, cw_ref, o_ref, tail_ref, *, blocks_per_seq):
    i, j = pl.program_id(0), pl.program_id(1)
    tm, tn = o_ref.shape

    @pl.when(i % blocks_per_seq == 0)
    def _():
        tail_ref[j] = jnp.zeros(tail_ref.shape[1:], _F32)

    h = h_ref[...]
    inv_rms = _lane_repeat(_inv_rms(ssq_ref[...], h.shape[1]), tn)
    g = jnp.dot(h, wg_ref[...], preferred_element_type=_F32) * inv_rms
    sg = _silu(_causal_conv(g, tail_ref[j], cw_ref[...])) * inv_rms
    tail_ref[j] = g[tm - _V7X_SUBLANES:]
    u = jnp.dot(h, wu_ref[...], preferred_element_type=_F32)
    o_ref[...] = (sg * u).astype(o_ref.dtype)


def _ffn_in(h, row_ssq, w_gate, w_up, conv_w, layer, seq_len, tm_target=1024, tn_target=512):
    n, d = h.shape
    ff = w_gate.shape[2]
    tm = _tile(seq_len, tm_target, _BF16_ROWS)
    tn = _tile(ff, tn_target, _V7X_LANES)
    kw = conv_w.shape[1]
    wspec = pl.BlockSpec((None, d, tn), lambda i, j: (layer, 0, j))
    vmem = 2 * 2 * d * (tm + 2 * tn) + 2 * tm * tn * 2 + 6 * tm * tn * 4 + ff * _V7X_SUBLANES * 4
    return pl.pallas_call(
        functools.partial(_ffn_in_kernel, blocks_per_seq=seq_len // tm),
        grid=(n // tm, ff // tn),
        in_specs=[pl.BlockSpec((tm, d), lambda i, j: (i, 0)),
                  pl.BlockSpec((tm, _V7X_LANES), lambda i, j: (i, 0)), wspec, wspec,
                  pl.BlockSpec((None, kw, tn), lambda i, j: (layer, 0, j))],
        out_specs=pl.BlockSpec((tm, tn), lambda i, j: (i, j)),
        out_shape=jax.ShapeDtypeStruct((n, ff), _BF16),
        scratch_shapes=[pltpu.VMEM((ff // tn, _V7X_SUBLANES, tn), _F32)],
        compiler_params=_params(("arbitrary", "arbitrary"), vmem),
        name="ffn_in",
    )(h, row_ssq, w_gate, w_up, conv_w)


def _cast_pad_kernel(x_ref, o_ref, *, rows, cols):
    i, j = pl.program_id(1), pl.program_id(2)
    tr, tc = o_ref.shape
    r = i * tr + lax.broadcasted_iota(jnp.int32, (tr, tc), 0)
    c = j * tc + lax.broadcasted_iota(jnp.int32, (tr, tc), 1)
    o_ref[...] = jnp.where(jnp.logical_and(r < rows, c < cols), x_ref[...], 0.0).astype(o_ref.dtype)


def _cast_pad(w, rows_to, cols_to, dtype):
    layers, rows, cols = w.shape
    tr = _tile(rows_to, 1024, _BF16_ROWS)
    tc = _tile(cols_to, 1024, _V7X_LANES)
    assert rows_to - rows < tr and cols_to - cols < tc
    return pl.pallas_call(
        functools.partial(_cast_pad_kernel, rows=rows, cols=cols),
        grid=(layers, rows_to // tr, cols_to // tc),
        in_specs=[pl.BlockSpec((None, tr, tc), lambda l, i, j: (l, i, j))],
        out_specs=pl.BlockSpec((None, tr, tc), lambda l, i, j: (l, i, j)),
        out_shape=jax.ShapeDtypeStruct((layers, rows_to, cols_to), dtype),
        compiler_params=_params(("parallel", "parallel", "parallel"), 2 * tr * tc * 6),
        name="cast_pad",
    )(w)


def _pad_last(w, total):
    return jnp.pad(w, ((0, 0),) * (w.ndim - 1) + ((0, total - w.shape[-1]),))


def _bf16_weights(w_in, w_out, w_gate, w_up, ffn_conv, w_down, gw, dn_heads):
    o_dn_a = 10 * gw
    o_fx = o_dn_a + 2 * dn_heads
    o_fx_f = o_fx + 3 * gw
    tile2 = o_fx_f // _V7X_LANES * _V7X_LANES
    narrow = jnp.concatenate([w_in[..., o_dn_a:o_dn_a + _V7X_LANES], _pad_last(w_in[..., tile2:], _V7X_LANES)],
                             axis=-1).astype(_BF16)
    ff = w_gate.shape[-1]
    ffp = -(-ff // _FF_ALIGN) * _FF_ALIGN if ff > _FF_ALIGN else ff
    return dict(
        w_in=w_in.astype(_BF16), n_wide_a=o_dn_a, wide_b=w_in[..., o_fx:o_fx_f].astype(_BF16),
        narrow=narrow, fox_f_lane=_V7X_LANES + o_fx_f - tile2, out=w_out.astype(_BF16),
        gate=_cast_pad(w_gate, w_gate.shape[1], ffp, _BF16), up=_cast_pad(w_up, w_up.shape[1], ffp, _BF16),
        conv=_pad_last(ffn_conv.astype(_F32), ffp),
        down=_cast_pad(w_down, ffp, w_down.shape[2], _BF16))


def _layer(stream, layer, batch, seq_len, wts, sc_conv, lam_vecs, diff_norm, dn_conv, dn_a_log,
           dn_dt_bias, dn_norm, fox_bias, ffn_norm, next_attn_norm):
    x, h, ssq = stream
    d = lam_vecs[0].shape[-1]
    gw = wts["out"].shape[1] // 4
    dn_heads, fox_heads = dn_a_log.shape[-1], fox_bias.shape[-1]
    diff_heads = gw // diff_norm.shape[-1]
    col = lambda idx: idx * gw

    p = _matmul([h], wts["w_in"], layer, _BF16, row_ssq=ssq, n_cols=wts["n_wide_a"], name="in_proj")
    p_fox = _matmul([h], wts["wide_b"], layer, _BF16, row_ssq=ssq, name="in_proj_fox")
    gates = _matmul([h], wts["narrow"], layer, _F32, row_ssq=ssq, name="in_proj_gates")

    y_sc = _short_conv(p, sc_conv, seq_len, 0, 1, 2, gw)
    lam_init = 0.8 - 0.6 * math.exp(-0.3 * layer)
    y_df = _diff_attention(p, lam_vecs, diff_norm, lam_init, batch, seq_len, col(3), col(4), col(5), diff_heads, d)

    gates_t = gates.reshape(batch, seq_len, gates.shape[1]).transpose(0, 2, 1)
    f0 = wts["fox_f_lane"]
    fox_rows = gates_t[:, f0:f0 + fox_heads].reshape(batch * fox_heads, seq_len)
    fox_bias_rows = jnp.tile(fox_bias.astype(_F32), batch).reshape(batch * fox_heads, 1)
    cum = _fox_cumsum(fox_rows, fox_bias_rows)
    y_fx = _fox_attention(p_fox, cum, batch, seq_len, col(0), col(1), col(2), fox_heads, d)

    y_dn = _gated_deltanet(p, gates, gates_t[:, :2 * dn_heads], dn_conv, dn_a_log, dn_dt_bias, dn_norm,
                           batch, seq_len, col(6), col(7), col(8), col(9), dn_heads, d)

    x, h, ssq = _matmul([y_sc, y_df, y_dn, y_fx], wts["out"], layer, _F32, residual=x, next_gain=ffn_norm,
                        tn_target=1024, name="out_proj")
    act = _ffn_in(h, ssq, wts["gate"], wts["up"], wts["conv"], layer, seq_len)
    if next_attn_norm is None:
        return _matmul_acc(act, wts["down"], layer, x, name="ffn_out"), None, None
    return _matmul_acc(act, wts["down"], layer, x, next_gain=next_attn_norm, name="ffn_out")


def kernel(x, attn_norm, w_in, sc_conv, lam_q1, lam_k1, lam_q2, lam_k2, diff_norm, dn_conv, dn_a_log,
           dn_dt_bias, dn_norm, fox_bias, w_out, ffn_norm, w_gate, w_up, ffn_conv, w_down, final_norm):
    batch, seq_len, d_model = x.shape
    xf = x.reshape(batch * seq_len, d_model).astype(_F32)
    wts = _bf16_weights(w_in, w_out, w_gate, w_up, ffn_conv, w_down, w_out.shape[1] // 4, dn_a_log.shape[-1])
    layers = w_in.shape[0]
    stream = (xf,) + tuple(_stream_stats(xf, attn_norm[0]))
    for l in range(layers):
        stream = _layer(stream, l, batch, seq_len, wts, sc_conv[l],
                        (lam_q1[l], lam_k1[l], lam_q2[l], lam_k2[l]), diff_norm[l], dn_conv[l], dn_a_log[l],
                        dn_dt_bias[l], dn_norm[l], fox_bias[l], ffn_norm[l],
                        attn_norm[l + 1] if l + 1 < layers else None)
    out = _rmsnorm(stream[0], final_norm, x.dtype)
    return out.reshape(batch, seq_len, d_model)
```
